```python
import jax, jax.numpy as jnp
from jax import lax
import numpy as np

D_MODEL = 1024
BATCH = 2
SEQ = 8192
DEPTH = 2
DEC_BATCH = 128
DEC_SEQ = 1
PAST_LEN = 16384
PAGE_SIZE = 128

N_EVEN = (DEPTH + 1) // 2
N_ODD = DEPTH // 2
MLA_HEADS = 8
MLA_NOPE = 64
MLA_ROPE = 32
MLA_V = 64
MLA_Q_RANK = 768
MLA_KV_RANK = 256
MLA_WIDTH = MLA_HEADS * MLA_V
MLA_SCALE = (MLA_NOPE + MLA_ROPE) ** -0.5
NSA_HEADS = 8
NSA_KV_HEADS = 2
NSA_GROUP = NSA_HEADS // NSA_KV_HEADS
NSA_DH = 64
NSA_WIDTH = NSA_HEADS * NSA_DH
NSA_KV_W = NSA_KV_HEADS * 2 * NSA_DH
CMP_BLOCK = 32
CMP_STRIDE = 16
SLC_BLOCK = 64
N_SELECT = 16
WINDOW = 512
FORCE_SCORE = 1e4
RET_HEADS = 8
RET_DK = 128
RET_DV = 256
RET_WIDTH = RET_HEADS * RET_DV
RET_CHUNK = 128
ROPE_BASE = 10000.0
EPS = 1e-6
Q_BLOCK = 128
EVEN_SPLITS = (MLA_Q_RANK, MLA_KV_RANK, MLA_ROPE, MLA_WIDTH, NSA_WIDTH, NSA_KV_W, NSA_KV_W, NSA_KV_W, 3 * NSA_HEADS, NSA_WIDTH)
EVEN_OFFSETS = tuple(sum(EVEN_SPLITS[:i + 1]) for i in range(len(EVEN_SPLITS) - 1))
EVEN_IN = sum(EVEN_SPLITS)
ODD_IN = 2 * RET_HEADS * RET_DK + 2 * RET_WIDTH

kernel_name = "mla_nsa_retention_hybrid_step"


def rmsnorm(x, g):
    xf = x.astype(jnp.float32)
    y = xf * lax.rsqrt(jnp.mean(xf * xf, -1, keepdims=True) + EPS)
    return (y * g.astype(jnp.float32)).astype(x.dtype)


def rope(x, pos):
    half = x.shape[-1] // 2
    inv = ROPE_BASE ** (-jnp.arange(half, dtype=jnp.float32) / half)
    ang = pos.astype(jnp.float32)[:, None] * inv
    ang = ang.reshape(ang.shape[:1] + (1,) * (x.ndim - 3) + (half,))
    cos, sin = jnp.cos(ang), jnp.sin(ang)
    xf = x.astype(jnp.float32)
    x1, x2 = xf[..., :half], xf[..., half:]
    return jnp.concatenate([x1 * cos - x2 * sin, x1 * sin + x2 * cos], -1).astype(x.dtype)


def masked_softmax(s, mask):
    s = jnp.where(mask, s.astype(jnp.float32), -jnp.inf)
    m = jnp.max(s, -1, keepdims=True)
    m = jnp.where(jnp.isfinite(m), m, 0.0)
    p = jnp.exp(s - m)
    return p / jnp.maximum(jnp.sum(p, -1, keepdims=True), 1e-30)


def alibi_slopes(n):
    return 2.0 ** (-8.0 * (jnp.arange(n, dtype=jnp.float32) + 1.0) / n)


def gather_pages(pool, page_table):
    g = pool[page_table]
    return g.reshape((g.shape[0], g.shape[1] * g.shape[2]) + g.shape[3:])


def map_query_blocks(fn, *xs):
    b, t = xs[0].shape[:2]
    nb = t // Q_BLOCK
    blk = tuple(jnp.moveaxis(x.reshape((b, nb, Q_BLOCK) + x.shape[2:]), 1, 0) for x in xs)
    out = lax.map(lambda a: fn(a[0], *a[1]), (jnp.arange(nb), blk))
    return jnp.moveaxis(out, 0, 1).reshape((b, t) + out.shape[3:])


def mla_scores(qcat, rows):
    return jnp.einsum('bqhc,bsc->bqhs', qcat, rows) * MLA_SCALE


def mla_prompt_attn(qcat, rows):
    s_len = rows.shape[1]
    kpos = jnp.arange(s_len)

    def blk(i, qb):
        qpos = i * Q_BLOCK + jnp.arange(Q_BLOCK)
        p = masked_softmax(mla_scores(qb, rows), (kpos[None, :] <= qpos[:, None])[None, :, None, :])
        return jnp.einsum('bqhs,bsc->bqhc', p.astype(rows.dtype), rows)[..., :MLA_KV_RANK]

    return map_query_blocks(blk, qcat)


def mla_sample_attn(qcat, past_rows, new_rows):
    p_len, n = past_rows.shape[1], new_rows.shape[1]
    s = jnp.concatenate([mla_scores(qcat, past_rows), mla_scores(qcat, new_rows)], -1)
    causal_new = jnp.arange(n)[None, :] <= jnp.arange(n)[:, None]
    mask = jnp.concatenate([jnp.ones((n, p_len), bool), causal_new], -1)[None, :, None, :]
    p = masked_softmax(s, mask).astype(past_rows.dtype)
    out = jnp.einsum('bqhs,bsc->bqhc', p[..., :p_len], past_rows) + jnp.einsum('bqhs,bsc->bqhc', p[..., p_len:], new_rows)
    return out[..., :MLA_KV_RANK]


def cmp_halves(kv, w, pe):
    b, t = kv.shape[:2]
    r = kv.reshape(b, t // CMP_STRIDE, CMP_STRIDE, NSA_KV_HEADS, 2, NSA_DH)
    pe_t = jnp.swapaxes(pe, 0, 1)[:, None]
    h1 = jnp.einsum('bmjgcd,cjde->bmgce', r + pe_t[:CMP_STRIDE], w[:, :CMP_STRIDE])
    h2 = jnp.einsum('bmjgcd,cjde->bmgce', r + pe_t[CMP_STRIDE:], w[:, CMP_STRIDE:])
    return h1, h2


def nsa_core(q, qpos, kvc, cpos, fetch_sel, n_slc, kvw, wpos, gates, slopes):
    qf = q * (NSA_DH ** -0.5)
    slope = slopes[None, None, :, :, None]
    dist_c = (qpos[:, None] - cpos[None, :]).astype(jnp.float32)
    s_c = jnp.einsum('bqgrd,bngd->bqgrn', qf, kvc[..., 0, :]) - slope * dist_c[None, :, None, None, :]
    p_c = masked_softmax(s_c, (dist_c >= 0)[None, :, None, None, :])
    o_c = jnp.einsum('bqgrn,bngd->bqgrd', p_c.astype(q.dtype), kvc[..., 1, :])
    p_grp = jnp.sum(p_c, axis=3)
    n_cmp = p_grp.shape[-1]
    ratio = SLC_BLOCK // CMP_STRIDE
    lead = CMP_BLOCK // CMP_STRIDE - 1
    pad = jnp.pad(p_grp, ((0, 0), (0, 0), (0, 0), (lead, ratio * n_slc + ratio - n_cmp)))
    s_slc = jnp.zeros(p_grp.shape[:3] + (n_slc,), jnp.float32)
    for o in range(-lead, ratio):
        w_o = (min(CMP_STRIDE * o + CMP_BLOCK, SLC_BLOCK) - max(CMP_STRIDE * o, 0)) / CMP_BLOCK
        st = o + lead
        s_slc = s_slc + w_o * pad[..., st: st + ratio * n_slc: ratio]
    j = jnp.arange(n_slc)
    cur = (qpos // SLC_BLOCK)[:, None]
    valid = (j[None, :] * SLC_BLOCK) <= qpos[:, None]
    forced = (j[None, :] == 0) | (j[None, :] == cur) | (j[None, :] == cur - 1)
    score = jnp.where(forced[None, :, None, :], FORCE_SCORE, jnp.where(valid[None, :, None, :], s_slc, -jnp.inf))
    n_sel = min(N_SELECT, n_slc)
    _, sel = lax.top_k(score, n_sel)
    kpos = sel[..., None] * SLC_BLOCK + jnp.arange(SLC_BLOCK)
    kv_s = fetch_sel(kpos)
    b, nq = q.shape[:2]
    kv_s = kv_s.reshape(b, nq, NSA_KV_HEADS, n_sel * SLC_BLOCK, 2, NSA_DH)
    kpos = kpos.reshape(b, nq, NSA_KV_HEADS, n_sel * SLC_BLOCK)
    dist_s = (qpos[None, :, None, None] - kpos).astype(jnp.float32)
    s_s = jnp.einsum('bqgrd,bqgkd->bqgrk', qf, kv_s[..., 0, :]) - slope * dist_s[:, :, :, None, :]
    p_s = masked_softmax(s_s, (dist_s >= 0)[:, :, :, None, :])
    o_s = jnp.einsum('bqgrk,bqgkd->bqgrd', p_s.astype(q.dtype), kv_s[..., 1, :])
    dist_w = (qpos[:, None] - wpos[None, :]).astype(jnp.float32)
    mask_w = (dist_w >= 0) & (dist_w <= WINDOW) & (wpos >= 0)[None, :]
    s_w = jnp.einsum('bqgrd,bwgd->bqgrw', qf, kvw[..., 0, :]) - slope * dist_w[None, :, None, None, :]
    p_w = masked_softmax(s_w, mask_w[None, :, None, None, :])
    o_w = jnp.einsum('bqgrw,bwgd->bqgrd', p_w.astype(q.dtype), kvw[..., 1, :])
    return gates[..., 0:1] * o_c + gates[..., 1:2] * o_s + gates[..., 2:3] * o_w


def nsa_prompt(q, kv_cmp, kv_slc, kv_win, gates, w_cmp, pe_cmp, slopes):
    b, s_len = q.shape[:2]
    h1, h2 = cmp_halves(kv_cmp, w_cmp, pe_cmp)
    kvc = h1[:, :-1] + h2[:, 1:]
    cpos = jnp.arange(kvc.shape[1]) * CMP_STRIDE + CMP_BLOCK - 1
    n_slc = s_len // SLC_BLOCK
    kvw_pad = jnp.pad(kv_win, ((0, 0), (WINDOW, 0), (0, 0), (0, 0), (0, 0)))
    bi = jnp.arange(b)[:, None, None, None, None]
    gi = jnp.arange(NSA_KV_HEADS)[None, None, :, None, None]

    def fetch(kpos):
        return kv_slc[bi, kpos, gi]

    def blk(i, qb, gb):
        s0 = i * Q_BLOCK
        qpos = s0 + jnp.arange(Q_BLOCK)
        kvw = lax.dynamic_slice_in_dim(kvw_pad, s0, WINDOW + Q_BLOCK, axis=1)
        wpos = s0 - WINDOW + jnp.arange(WINDOW + Q_BLOCK)
        return nsa_core(qb, qpos, kvc, cpos, fetch, n_slc, kvw, wpos, gb, slopes)

    return map_query_blocks(blk, q, gates)


def nsa_sample(q, kv_cmp, kv_slc, kv_win, gates, cmp_pool, slc_pool, win_buf, page_table, w_cmp, pe_cmp, slopes):
    db, n = q.shape[:2]
    p_len = page_table.shape[1] * PAGE_SIZE
    qpos = p_len + jnp.arange(n)
    h1p, h2p = cmp_halves(gather_pages(cmp_pool, page_table), w_cmp, pe_cmp)
    n_pad = -n % CMP_STRIDE
    h1n, h2n = cmp_halves(jnp.pad(kv_cmp, ((0, 0), (0, n_pad), (0, 0), (0, 0), (0, 0))), w_cmp, pe_cmp)
    h1 = jnp.concatenate([h1p, h1n], 1)
    h2 = jnp.concatenate([h2p, h2n], 1)
    kvc = h1[:, :-1] + h2[:, 1:]
    cpos = jnp.arange(kvc.shape[1]) * CMP_STRIDE + CMP_BLOCK - 1
    n_slc = -(-(p_len + n) // SLC_BLOCK)
    bi = jnp.arange(db)[:, None, None, None, None]
    gi = jnp.arange(NSA_KV_HEADS)[None, None, :, None, None]

    def fetch(kpos):
        pp = jnp.clip(kpos, 0, p_len - 1)
        phys = page_table[bi, pp // PAGE_SIZE]
        past = slc_pool[phys, pp % PAGE_SIZE, gi]
        new = kv_slc[bi, jnp.clip(kpos - p_len, 0, n - 1), gi]
        return jnp.where((kpos < p_len)[..., None, None], past, new)

    kvw = jnp.concatenate([win_buf, kv_win], 1)
    wpos = p_len - win_buf.shape[1] + jnp.arange(kvw.shape[1])
    out = nsa_core(q, qpos, kvc, cpos, fetch, n_slc, kvw, wpos, gates, slopes)
    return out, kvw[:, kvw.shape[1] - win_buf.shape[1]:]


def even_project(h, pos, w_in, gq, gkv, wuq, wuk):
    b, t = h.shape[:2]
    z = jnp.einsum('btd,de->bte', h, w_in)
    cq, ckv, kr, g_mla, q_nsa, kv_cmp, kv_slc, kv_win, gl, g_nsa = jnp.split(z, EVEN_OFFSETS, axis=-1)
    qm = jnp.einsum('btc,chd->bthd', rmsnorm(cq, gq), wuq)
    q_lat = jnp.einsum('bthn,chn->bthc', qm[..., :MLA_NOPE], wuk)
    qcat = jnp.concatenate([q_lat, rope(qm[..., MLA_NOPE:], pos)], -1)
    rows = jnp.concatenate([rmsnorm(ckv, gkv), rope(kr, pos)], -1)
    kvs = (b, t, NSA_KV_HEADS, 2, NSA_DH)
    q_nsa = q_nsa.reshape(b, t, NSA_KV_HEADS, NSA_GROUP, NSA_DH)
    gates = jax.nn.sigmoid(gl.reshape(b, t, NSA_KV_HEADS, NSA_GROUP, 3))
    return (qcat, rows, g_mla, q_nsa, kv_cmp.reshape(kvs), kv_slc.reshape(kvs), kv_win.reshape(kvs), gates, g_nsa)


def even_output(lat, g_mla, o_nsa, g_nsa, wuv, w_out):
    b, t = lat.shape[:2]
    o_mla = jnp.einsum('bthc,chv->bthv', lat, wuv).reshape(b, t, MLA_WIDTH)
    o = jnp.concatenate([o_mla * jax.nn.silu(g_mla), o_nsa.reshape(b, t, NSA_WIDTH) * jax.nn.silu(g_nsa)], -1)
    return jnp.einsum('bte,ed->btd', o, w_out)


def retention(q, k, v, s0, chunk):
    b, t = q.shape[:2]
    nc = t // chunk
    log_g = jnp.log1p(-(2.0 ** (-5.0 - jnp.arange(RET_HEADS, dtype=jnp.float32))))
    idx = jnp.arange(chunk, dtype=jnp.float32)
    diff = idx[:, None] - idx[None, :]
    dmat = jnp.where(diff >= 0, jnp.exp(log_g[:, None, None] * jnp.maximum(diff, 0.0)), 0.0)
    xi = jnp.exp(log_g[None, :] * (idx[:, None] + 1.0))[None, :, :, None]
    zeta = jnp.exp(log_g[None, :] * (chunk - 1.0 - idx)[:, None])[None, :, :, None]
    g_chunk = jnp.exp(log_g * chunk)[None, :, None, None]

    def step(s, inp):
        qc, kc, vc = inp
        inner = jnp.einsum('bnhd,bmhd->bhnm', qc, kc) * dmat
        o = jnp.einsum('bhnm,bmhv->bnhv', inner, vc) + jnp.einsum('bnhd,bhdv->bnhv', qc, s) * xi
        s = g_chunk * s + jnp.einsum('bmhd,bmhv->bhdv', kc, vc * zeta)
        return s, o

    xs = tuple(jnp.moveaxis(a.astype(jnp.float32).reshape((b, nc, chunk) + a.shape[2:]), 1, 0) for a in (q, k, v))
    s_fin, o = lax.scan(step, s0.astype(jnp.float32), xs)
    o = jnp.moveaxis(o, 0, 1).reshape(b, t, RET_HEADS, RET_DV)
    return o, s_fin.astype(s0.dtype)


def retention_layer(h, pos, s0, chunk, w_in, gn, w_out):
    b, t = h.shape[:2]
    z = jnp.einsum('btd,de->bte', h, w_in)
    qk = RET_HEADS * RET_DK
    q, k, v, g = jnp.split(z, (qk, 2 * qk, 2 * qk + RET_WIDTH), axis=-1)
    q = rope(q.reshape(b, t, RET_HEADS, RET_DK), pos)
    k = rope(k.reshape(b, t, RET_HEADS, RET_DK), pos) * (RET_DK ** -0.5)
    o, s_new = retention(q, k, v.reshape(b, t, RET_HEADS, RET_DV), s0, chunk)
    mu = jnp.mean(o, -1, keepdims=True)
    var = jnp.mean(jnp.square(o - mu), -1, keepdims=True)
    o = (o - mu) * lax.rsqrt(var + EPS) * gn.astype(jnp.float32)
    y = jax.nn.silu(g) * o.reshape(b, t, RET_WIDTH).astype(h.dtype)
    return jnp.einsum('bte,ed->btd', y, w_out), s_new


def setup_inputs(seed: int = 0) -> dict:
    key = jax.random.key(seed)
    ks = iter(jax.random.split(key, 32))

    def nrm(shape, scale):
        return jax.random.normal(next(ks), shape, jnp.float32) * scale

    n_pages = PAST_LEN // PAGE_SIZE
    pool_pages = (DEC_BATCH * n_pages * 5) // 4
    win_buf = min(WINDOW, PAST_LEN)
    perm = jax.random.permutation(next(ks), pool_pages)[:DEC_BATCH * n_pages]
    page_table = perm.reshape(DEC_BATCH, n_pages).astype(jnp.int32)
    kv_row = (NSA_KV_HEADS, 2, NSA_DH)
    return {
        'x_prompt': nrm((BATCH, SEQ, D_MODEL), 1.0),
        'x_sample': nrm((DEC_BATCH, DEC_SEQ, D_MODEL), 1.0),
        'cache_mla': nrm((N_EVEN, pool_pages, PAGE_SIZE, MLA_KV_RANK + MLA_ROPE), 1.0),
        'cache_nsa_cmp': nrm((N_EVEN, pool_pages, PAGE_SIZE) + kv_row, 1.0),
        'cache_nsa_slc': nrm((N_EVEN, pool_pages, PAGE_SIZE) + kv_row, 1.0),
        'state_nsa_win': nrm((N_EVEN, DEC_BATCH, win_buf) + kv_row, 1.0),
        'state_ret': nrm((N_ODD, DEC_BATCH, RET_HEADS, RET_DK, RET_DV), 0.5),
        'page_table': page_table,
        'norm_even': 1.0 + nrm((N_EVEN, D_MODEL), 0.02),
        'w_in_even': nrm((N_EVEN, D_MODEL, EVEN_IN), D_MODEL ** -0.5),
        'mla_gq': 1.0 + nrm((N_EVEN, MLA_Q_RANK), 0.02),
        'mla_gkv': 1.0 + nrm((N_EVEN, MLA_KV_RANK), 0.02),
        'mla_wuq': nrm((N_EVEN, MLA_Q_RANK, MLA_HEADS, MLA_NOPE + MLA_ROPE), MLA_Q_RANK ** -0.5),
        'mla_wuk': nrm((N_EVEN, MLA_KV_RANK, MLA_HEADS, MLA_NOPE), MLA_KV_RANK ** -0.5),
        'mla_wuv': nrm((N_EVEN, MLA_KV_RANK, MLA_HEADS, MLA_V), MLA_KV_RANK ** -0.5),
        'nsa_cmp_w': nrm((N_EVEN, 2, CMP_BLOCK, NSA_DH, NSA_DH), (CMP_BLOCK * NSA_DH) ** -0.5),
        'nsa_cmp_pe': nrm((N_EVEN, 2, CMP_BLOCK, NSA_DH), 0.5),
        'w_out_even': nrm((N_EVEN, MLA_WIDTH + NSA_WIDTH, D_MODEL), (MLA_WIDTH + NSA_WIDTH) ** -0.5),
        'norm_odd': 1.0 + nrm((N_ODD, D_MODEL), 0.02),
        'w_in_odd': nrm((N_ODD, D_MODEL, ODD_IN), D_MODEL ** -0.5),
        'ret_gn': 1.0 + nrm((N_ODD, RET_HEADS, RET_DV), 0.02),
        'w_out_odd': nrm((N_ODD, RET_WIDTH, D_MODEL), RET_WIDTH ** -0.5),
        'final_norm': 1.0 + nrm((D_MODEL,), 0.02),
    }


def reference(x_prompt, x_sample, cache_mla, cache_nsa_cmp, cache_nsa_slc, state_nsa_win, state_ret, page_table,
              norm_even, w_in_even, mla_gq, mla_gkv, mla_wuq, mla_wuk, mla_wuv, nsa_cmp_w, nsa_cmp_pe, w_out_even,
              norm_odd, w_in_odd, ret_gn, w_out_odd, final_norm):
    b, s_len = x_prompt.shape[:2]
    db, n_new = x_sample.shape[:2]
    p_len = page_table.shape[1] * PAGE_SIZE
    pos_p = jnp.arange(s_len)
    pos_s = p_len + jnp.arange(n_new)
    slopes = alibi_slopes(NSA_HEADS).reshape(NSA_KV_HEADS, NSA_GROUP)
    xp, xs = x_prompt, x_sample
    mla_p, cmp_p, slc_p, win_p, ret_p = [], [], [], [], []
    mla_s, cmp_s, slc_s, win_s, ret_s = [], [], [], [], []
    for layer in range(DEPTH):
        li = layer // 2
        if layer % 2 == 0:
            qcat, rows, g_mla, q_nsa, kv_cmp, kv_slc, kv_win, gates, g_nsa = even_project(
                rmsnorm(xp, norm_even[li]), pos_p, w_in_even[li], mla_gq[li], mla_gkv[li], mla_wuq[li], mla_wuk[li])
            lat = mla_prompt_attn(qcat, rows)
            o_nsa = nsa_prompt(q_nsa, kv_cmp, kv_slc, kv_win, gates, nsa_cmp_w[li], nsa_cmp_pe[li], slopes)
            o_nsa = o_nsa.reshape(b, s_len, NSA_WIDTH)
            xp = xp + even_output(lat, g_mla, o_nsa, g_nsa, mla_wuv[li], w_out_even[li])
            mla_p.append(rows)
            cmp_p.append(kv_cmp)
            slc_p.append(kv_slc)
            win_p.append(kv_win[:, s_len - min(WINDOW, s_len):])
            qcat, rows, g_mla, q_nsa, kv_cmp, kv_slc, kv_win, gates, g_nsa = even_project(
                rmsnorm(xs, norm_even[li]), pos_s, w_in_even[li], mla_gq[li], mla_gkv[li], mla_wuq[li], mla_wuk[li])
            lat = mla_sample_attn(qcat, gather_pages(cache_mla[li], page_table), rows)
            o_nsa, new_win = nsa_sample(q_nsa, kv_cmp, kv_slc, kv_win, gates, cache_nsa_cmp[li], cache_nsa_slc[li],
                                        state_nsa_win[li], page_table, nsa_cmp_w[li], nsa_cmp_pe[li], slopes)
            o_nsa = o_nsa.reshape(db, n_new, NSA_WIDTH)
            xs = xs + even_output(lat, g_mla, o_nsa, g_nsa, mla_wuv[li], w_out_even[li])
            mla_s.append(rows)
            cmp_s.append(kv_cmp)
            slc_s.append(kv_slc)
            win_s.append(new_win)
        else:
            s0 = jnp.zeros((b, RET_HEADS, RET_DK, RET_DV), x_prompt.dtype)
            yp, sp = retention_layer(rmsnorm(xp, norm_odd[li]), pos_p, s0, RET_CHUNK, w_in_odd[li], ret_gn[li], w_out_odd[li])
            xp = xp + yp
            ret_p.append(sp)
            ys, ss = retention_layer(rmsnorm(xs, norm_odd[li]), pos_s, state_ret[li], n_new, w_in_odd[li], ret_gn[li], w_out_odd[li])
            xs = xs + ys
            ret_s.append(ss)
    y_prompt = rmsnorm(xp, final_norm)
    y_sample = rmsnorm(xs, final_norm)
    return (y_prompt, y_sample,
            jnp.stack(mla_p), jnp.stack(cmp_p), jnp.stack(slc_p), jnp.stack(win_p), jnp.stack(ret_p),
            jnp.stack(mla_s), jnp.stack(cmp_s), jnp.stack(slc_s), jnp.stack(win_s), jnp.stack(ret_s))
```

```python
import functools

import numpy as np
import jax
import jax.numpy as jnp
from jax import lax
from jax.experimental import pallas as pl
from jax.experimental.pallas import tpu as pltpu

F32 = jnp.float32
BF16 = jnp.bfloat16

D_MODEL = 1024
PAGE_SIZE = 128
MLA_HEADS = 8
MLA_NOPE = 64
MLA_ROPE = 32
MLA_V = 64
MLA_Q_RANK = 768
MLA_KV_RANK = 256
MLA_WIDTH = MLA_HEADS * MLA_V
MLA_SCALE = (MLA_NOPE + MLA_ROPE) ** -0.5
NSA_HEADS = 8
NSA_KV_HEADS = 2
NSA_GROUP = NSA_HEADS // NSA_KV_HEADS
NSA_DH = 64
NSA_WIDTH = NSA_HEADS * NSA_DH
NSA_KV_W = NSA_KV_HEADS * 2 * NSA_DH
CMP_BLOCK = 32
CMP_STRIDE = 16
SLC_BLOCK = 64
N_SELECT = 16
WINDOW = 512
FORCE_SCORE = 1e4
RET_HEADS = 8
RET_DK = 128
RET_DV = 256
RET_WIDTH = RET_HEADS * RET_DV
RET_CHUNK = 128
ROPE_BASE = 10000.0
EPS = 1e-6
EVEN_SPLITS = (MLA_Q_RANK, MLA_KV_RANK, MLA_ROPE, MLA_WIDTH, NSA_WIDTH, NSA_KV_W, NSA_KV_W, NSA_KV_W,
               3 * NSA_HEADS, NSA_WIDTH)

LANES = 128
VMEM_LIMIT_BYTES = 56 * 1024 * 1024
NEG_BIG = -1e30
MASK_BIG = 16384.0

HALF_ROPE = MLA_ROPE // 2


def _cparams(sem):
    return pltpu.CompilerParams(dimension_semantics=sem, vmem_limit_bytes=VMEM_LIMIT_BYTES)


def _full_spec(a):
    nd = a.ndim
    return pl.BlockSpec(a.shape, lambda *_: (0,) * nd)


def _rms(x, g):
    y = x * lax.rsqrt(jnp.mean(x * x, axis=-1, keepdims=True) + EPS)
    return y * g


def _dot(a, b):
    return jnp.dot(a, b, preferred_element_type=F32)


def _dot_nt(a, b):
    return lax.dot_general(a, b, (((1,), (1,)), ((), ())), preferred_element_type=F32)


_EVEN_GROUPS = (
    ("cq", MLA_Q_RANK), ("ckv", MLA_KV_RANK), ("kr1", LANES), ("kr2", LANES),
    ("gmla", MLA_HEADS * LANES), ("qn", NSA_HEADS * LANES), ("cmp", NSA_KV_W),
    ("slc", NSA_KV_W), ("slcp", 4 * LANES), ("win", NSA_KV_W), ("winp", 4 * LANES),
    ("gl", LANES), ("gnsa", NSA_HEADS * LANES),
)
_EVEN_OFF = {}
_o = 0
for _n, _w in _EVEN_GROUPS:
    _EVEN_OFF[_n] = (_o, _o + _w)
    _o += _w
EVEN_EXT = _o


def _pad_heads(w, nh, dh):
    k = w.shape[0]
    w = w.reshape(k, nh, dh)
    return jnp.pad(w, ((0, 0), (0, 0), (0, LANES - dh))).reshape(k, nh * LANES)


def _kv_pad(w):
    k = w.shape[0]
    w4 = w.reshape(k, NSA_KV_HEADS, 2, NSA_DH)
    w4 = jnp.transpose(w4, (0, 2, 1, 3))
    return jnp.pad(w4, ((0, 0), (0, 0), (0, 0), (0, LANES - NSA_DH))).reshape(k, 4 * LANES)


def _prep_even_weights(w_in, wuq, wuk, wuv, w_out):
    offs = np.cumsum((0,) + EVEN_SPLITS)
    cq, ckv, kr, g_mla, q_nsa, kv_cmp, kv_slc, kv_win, gl, g_nsa = [
        w_in[:, offs[i]:offs[i + 1]] for i in range(len(EVEN_SPLITS))]
    parts = {
        "cq": cq, "ckv": ckv,
        "kr1": jnp.tile(kr[:, :HALF_ROPE], (1, MLA_HEADS)),
        "kr2": jnp.tile(kr[:, HALF_ROPE:], (1, MLA_HEADS)),
        "gmla": _pad_heads(g_mla, MLA_HEADS, MLA_V),
        "qn": _pad_heads(q_nsa, NSA_HEADS, NSA_DH),
        "cmp": kv_cmp, "slc": kv_slc, "slcp": _kv_pad(kv_slc),
        "win": kv_win, "winp": _kv_pad(kv_win),
        "gl": jnp.pad(gl, ((0, 0), (0, LANES - gl.shape[1]))),
        "gnsa": _pad_heads(g_nsa, NSA_HEADS, NSA_DH),
    }
    w_ext = jnp.concatenate([parts[n] for n, _ in _EVEN_GROUPS], axis=1).astype(BF16)
    c = wuq.shape[0]
    nope = jnp.pad(wuq[:, :, :MLA_NOPE], ((0, 0), (0, 0), (0, LANES - MLA_NOPE))).reshape(c, MLA_HEADS * LANES)
    r1 = wuq[:, :, MLA_NOPE:MLA_NOPE + HALF_ROPE].reshape(c, MLA_HEADS * HALF_ROPE)
    r2 = wuq[:, :, MLA_NOPE + HALF_ROPE:].reshape(c, MLA_HEADS * HALF_ROPE)
    wuq_ext = jnp.concatenate([nope, r1, r2], axis=1).astype(BF16)
    wuk_ext = jnp.pad(jnp.transpose(wuk, (1, 2, 0)), ((0, 0), (0, LANES - MLA_NOPE), (0, 0))).astype(BF16)
    wuv_ext = jnp.pad(jnp.transpose(wuv, (1, 0, 2)), ((0, 0), (0, 0), (0, LANES - MLA_V))).astype(BF16)
    d = w_out.shape[1]
    wo = w_out.reshape(MLA_HEADS + NSA_HEADS, MLA_V, d)
    wo_ext = jnp.pad(wo, ((0, 0), (0, LANES - MLA_V), (0, 0))).reshape((MLA_HEADS + NSA_HEADS) * LANES, d).astype(BF16)
    return w_ext, wuq_ext, wuk_ext, wuv_ext, wo_ext


def _rope_tables_mla(pos):
    inv = ROPE_BASE ** (-jnp.arange(HALF_ROPE, dtype=F32) / HALF_ROPE)
    ang = pos.astype(F32)[:, None] * inv
    return jnp.tile(jnp.cos(ang), (1, LANES // HALF_ROPE)), jnp.tile(jnp.sin(ang), (1, LANES // HALF_ROPE))


def _even_proj_kernel(x_ref, gn_ref, w_ref, gq_ref, gkv_ref, wuq_ref, wuk_ref, cos_ref, sin_ref,
                      qext_ref, rows_ref, kext_ref, gmla_ref, qn_ref, cmp_ref, slc_ref, slcp_ref,
                      win_ref, winp_ref, gl_ref, gnsa_ref):
    h = _rms(x_ref[...], gn_ref[...]).astype(BF16)

    def proj(name):
        a, b = _EVEN_OFF[name]
        return _dot(h, w_ref[:, a:b])

    cos = cos_ref[...]
    sin = sin_ref[...]
    cqn = _rms(proj("cq"), gq_ref[...]).astype(BF16)
    nh = MLA_HEADS * LANES
    r1 = _dot(cqn, wuq_ref[:, nh:nh + LANES])
    r2 = _dot(cqn, wuq_ref[:, nh + LANES:nh + 2 * LANES])
    o1 = r1 * cos - r2 * sin
    o2 = r1 * sin + r2 * cos
    head_of_lane = lax.broadcasted_iota(jnp.int32, (1, LANES), 1) // HALF_ROPE
    for hd in range(MLA_HEADS):
        nope = _dot(cqn, wuq_ref[:, hd * LANES:(hd + 1) * LANES]).astype(BF16)
        qext_ref[hd, :, 0:MLA_KV_RANK] = _dot(nope, wuk_ref[hd]).astype(BF16)
        sel = head_of_lane == hd
        qext_ref[hd, :, MLA_KV_RANK:MLA_KV_RANK + LANES] = jnp.where(sel, o1, 0.0).astype(BF16)
        qext_ref[hd, :, MLA_KV_RANK + LANES:] = jnp.where(sel, o2, 0.0).astype(BF16)
    latn = _rms(proj("ckv"), gkv_ref[...])
    kr1 = proj("kr1")
    kr2 = proj("kr2")
    k1 = kr1 * cos - kr2 * sin
    k2 = kr1 * sin + kr2 * cos
    rows_ref[:, 0:MLA_KV_RANK] = latn
    rows_ref[:, MLA_KV_RANK:MLA_KV_RANK + LANES] = k1
    rows_ref[:, MLA_KV_RANK + LANES:] = k2
    kext_ref[:, 0:MLA_KV_RANK] = latn.astype(BF16)
    kext_ref[:, MLA_KV_RANK:MLA_KV_RANK + LANES] = k1.astype(BF16)
    kext_ref[:, MLA_KV_RANK + LANES:] = k2.astype(BF16)
    gmla_ref[...] = proj("gmla")
    qn_ref[...] = (proj("qn") * (NSA_DH ** -0.5)).astype(BF16)
    cmp_ref[...] = proj("cmp")
    slc_ref[...] = proj("slc")
    slcp_ref[...] = proj("slcp").astype(BF16)
    win_ref[...] = proj("win")
    winp_ref[...] = proj("winp").astype(BF16)
    gl_ref[...] = proj("gl")
    gnsa_ref[...] = proj("gnsa")


def _even_project(x, gn, w_ext, gq, gkv, wuq_ext, wuk_ext, cos, sin, tm):
    t = x.shape[0]
    nt = t // tm
    ntab = cos.shape[0] // tm
    row = lambda w: pl.BlockSpec((tm, w), lambda i: (i, 0))
    tab = pl.BlockSpec((tm, LANES), lambda i: (i % ntab, 0))
    kext_w = MLA_KV_RANK + 2 * LANES
    out_shapes = (
        jax.ShapeDtypeStruct((MLA_HEADS, t, kext_w), BF16),
        jax.ShapeDtypeStruct((t, kext_w), F32),
        jax.ShapeDtypeStruct((t, kext_w), BF16),
        jax.ShapeDtypeStruct((t, MLA_HEADS * LANES), F32),
        jax.ShapeDtypeStruct((t, NSA_HEADS * LANES), BF16),
        jax.ShapeDtypeStruct((t, NSA_KV_W), F32),
        jax.ShapeDtypeStruct((t, NSA_KV_W), F32),
        jax.ShapeDtypeStruct((t, 4 * LANES), BF16),
        jax.ShapeDtypeStruct((t, NSA_KV_W), F32),
        jax.ShapeDtypeStruct((t, 4 * LANES), BF16),
        jax.ShapeDtypeStruct((t, LANES), F32),
        jax.ShapeDtypeStruct((t, NSA_HEADS * LANES), F32),
    )
    out_specs = (
        pl.BlockSpec((MLA_HEADS, tm, kext_w), lambda i: (0, i, 0)),
        row(kext_w), row(kext_w), row(MLA_HEADS * LANES), row(NSA_HEADS * LANES), row(NSA_KV_W),
        row(NSA_KV_W), row(4 * LANES), row(NSA_KV_W), row(4 * LANES), row(LANES), row(NSA_HEADS * LANES),
    )
    return pl.pallas_call(
        _even_proj_kernel,
        grid=(nt,),
        in_specs=[row(D_MODEL), _full_spec(gn), _full_spec(w_ext), _full_spec(gq), _full_spec(gkv),
                  _full_spec(wuq_ext), _full_spec(wuk_ext), tab, tab],
        out_specs=out_specs,
        out_shape=out_shapes,
        compiler_params=_cparams(("parallel",)),
        name="even_proj",
    )(x, gn, w_ext, gq, gkv, wuq_ext, wuk_ext, cos, sin)


def _mla_attn_kernel(q_ref, k_ref, o_ref, m_sc, l_sc, acc_sc, *, tq, tk):
    qi = pl.program_id(1)
    ki = pl.program_id(2)
    n_needed = ((qi + 1) * tq + tk - 1) // tk
    m_rows = MLA_HEADS * tq

    @pl.when(ki == 0)
    def _():
        m_sc[...] = jnp.full(m_sc.shape, NEG_BIG, F32)
        l_sc[...] = jnp.zeros(l_sc.shape, F32)
        acc_sc[...] = jnp.zeros(acc_sc.shape, F32)

    def step(masked):
        q = q_ref[...].reshape(m_rows, q_ref.shape[-1])
        k = k_ref[...]
        s = _dot_nt(q, k) * MLA_SCALE
        if masked:
            qpos = qi * tq + lax.broadcasted_iota(jnp.int32, (MLA_HEADS, tq, tk), 1).reshape(m_rows, tk)
            kpos = ki * tk + lax.broadcasted_iota(jnp.int32, (m_rows, tk), 1)
            s = jnp.where(kpos <= qpos, s, NEG_BIG)
        m_prev = m_sc[...]
        m_new = jnp.maximum(m_prev, jnp.max(s, axis=1, keepdims=True))
        alpha = jnp.exp(m_prev - m_new)
        p = jnp.exp(s - m_new)
        if masked:
            p = jnp.where(kpos <= qpos, p, 0.0)
        l_sc[...] = alpha * l_sc[...] + jnp.sum(p, axis=1, keepdims=True)
        acc_sc[...] = alpha * acc_sc[...] + _dot(p.astype(BF16), k[:, :MLA_KV_RANK])
        m_sc[...] = m_new

    crosses = (ki + 1) * tk - 1 > qi * tq

    @pl.when(jnp.logical_and(ki < n_needed, crosses))
    def _():
        step(True)

    @pl.when(jnp.logical_and(ki < n_needed, jnp.logical_not(crosses)))
    def _():
        step(False)

    @pl.when(ki == n_needed - 1)
    def _():
        out = acc_sc[...] / jnp.maximum(l_sc[...], 1e-30)
        o_ref[...] = out.reshape(MLA_HEADS, tq, MLA_KV_RANK).astype(o_ref.dtype)


def _mla_prompt_attention(q_ext, k_ext, batch, seq, tq, tk):
    nq, nk = seq // tq, seq // tk
    kw = q_ext.shape[-1]

    def k_map(b, qi, ki):
        last = ((qi + 1) * tq + tk - 1) // tk - 1
        return (b * nk + jnp.minimum(ki, last), 0)

    return pl.pallas_call(
        functools.partial(_mla_attn_kernel, tq=tq, tk=tk),
        grid=(batch, nq, nk),
        in_specs=[pl.BlockSpec((MLA_HEADS, tq, kw), lambda b, qi, ki: (0, b * nq + qi, 0)),
                  pl.BlockSpec((tk, kw), k_map)],
        out_specs=pl.BlockSpec((MLA_HEADS, tq, MLA_KV_RANK), lambda b, qi, ki: (0, b * nq + qi, 0)),
        out_shape=jax.ShapeDtypeStruct((MLA_HEADS, batch * seq, MLA_KV_RANK), BF16),
        scratch_shapes=[pltpu.VMEM((MLA_HEADS * tq, 1), F32), pltpu.VMEM((MLA_HEADS * tq, 1), F32),
                        pltpu.VMEM((MLA_HEADS * tq, MLA_KV_RANK), F32)],
        compiler_params=_cparams(("parallel", "parallel", "arbitrary")),
        name="mla_prompt_attn",
    )(q_ext, k_ext)


SEG_W = CMP_STRIDE * NSA_KV_W
KVP_W = 4 * LANES


def _prep_cmp_weights(w_cmp, pe_cmp):
    wh = w_cmp.reshape(2, 2, CMP_STRIDE, NSA_DH, NSA_DH)
    big = jnp.zeros((2, CMP_STRIDE, NSA_KV_HEADS, 2, NSA_DH, 2, NSA_KV_HEADS, LANES), F32)
    for g in range(NSA_KV_HEADS):
        for c in range(2):
            big = big.at[:, :, g, c, :, c, g, :NSA_DH].set(wh[c])
    w_big = big.reshape(2, SEG_W, KVP_W).astype(BF16)
    peh = pe_cmp.reshape(2, 2, CMP_STRIDE, NSA_DH)
    pe_big = jnp.broadcast_to(jnp.transpose(peh, (1, 2, 0, 3))[:, :, None], (2, CMP_STRIDE, NSA_KV_HEADS, 2, NSA_DH))
    return w_big, pe_big.reshape(2, 1, SEG_W)


def _slc_overlap_matrix(n_cmp_rows, n_lanes):
    ratio = SLC_BLOCK // CMP_STRIDE
    lead = CMP_BLOCK // CMP_STRIDE - 1
    a = np.zeros((n_cmp_rows, n_lanes), np.float32)
    for j in range(n_lanes):
        for o in range(-lead, ratio):
            n = ratio * j + o
            if 0 <= n < n_cmp_rows:
                a[n, j] = (min(CMP_STRIDE * o + CMP_BLOCK, SLC_BLOCK) - max(CMP_STRIDE * o, 0)) / CMP_BLOCK
    return a


def _key_position_features(n_keys, n_lanes):
    kc = np.zeros((n_keys, LANES + n_lanes), np.float32)
    pos = np.arange(n_keys)
    kc[:, NSA_DH] = (pos // SLC_BLOCK) * SLC_BLOCK
    kc[:, NSA_DH + 1] = pos % SLC_BLOCK
    kc[pos, LANES + pos // SLC_BLOCK] = 1.0
    return kc


def _slopes_col(g, rows, tq):
    r = lax.shift_right_logical(lax.broadcasted_iota(jnp.int32, (rows, 1), 0), int(np.log2(tq)))
    col = jnp.zeros((rows, 1), F32)
    for rr in range(NSA_GROUP):
        col = jnp.where(r == rr, 2.0 ** (-(g * NSA_GROUP + rr + 1.0)), col)
    return col


def _softmax_rows(s, mask):
    s = jnp.where(mask, s, NEG_BIG)
    m = jnp.max(s, axis=1, keepdims=True)
    e = jnp.where(mask, jnp.exp(s - m), 0.0)
    return e / jnp.maximum(jnp.sum(e, axis=1, keepdims=True), 1e-30)


def _top_blocks(score, n_pick):
    lane = lax.broadcasted_iota(jnp.int32, score.shape, 1)
    sel = jnp.zeros(score.shape, F32)
    for _ in range(n_pick):
        m = jnp.max(score, axis=1, keepdims=True)
        idx = jnp.min(jnp.where(score == m, lane, score.shape[1]), axis=1, keepdims=True)
        hit = lane == idx
        sel = jnp.where(jnp.logical_and(hit, m > 0.5 * NEG_BIG), 1.0, sel)
        score = jnp.where(hit, 2.0 * NEG_BIG, score)
    return sel


def _split3(x):
    hi = x.astype(BF16)
    r1 = x - hi.astype(F32)
    mid = r1.astype(BF16)
    lo = (r1 - mid.astype(F32)).astype(BF16)
    return hi, mid, lo


def _cmp_prompt_kernel(x_ref, pe_ref, w_ref, kvc_ref):
    x = x_ref[...]
    h1 = _dot((x + pe_ref[0]).astype(BF16), w_ref[0])
    h2 = _dot((x + pe_ref[1]).astype(BF16), w_ref[1])
    n = x.shape[0]
    kvc_ref[...] = (h1 + pltpu.roll(h2, n - 1, 0)).astype(kvc_ref.dtype)


def _cmp_prompt(kv_cmp, w_big, pe_big, batch, seq):
    nseg = seq // CMP_STRIDE
    x = kv_cmp.reshape(batch * nseg, SEG_W)
    return pl.pallas_call(
        _cmp_prompt_kernel,
        grid=(batch,),
        in_specs=[pl.BlockSpec((nseg, SEG_W), lambda b: (b, 0)), _full_spec(pe_big), _full_spec(w_big)],
        out_specs=pl.BlockSpec((nseg, KVP_W), lambda b: (b, 0)),
        out_shape=jax.ShapeDtypeStruct((batch * nseg, KVP_W), BF16),
        compiler_params=_cparams(("parallel",)),
        name="nsa_cmp_prompt",
    )(x, pe_big, w_big)


N_WIN_BLOCKS = WINDOW // 128 + 1


def _nsa_prompt_kernel(qn_ref, gl_ref, kvc_ref, amat_ref, slcp_ref, kc_ref, *rest, tq, tk, nseg):
    win_refs = rest[:N_WIN_BLOCKS]
    o_ref = rest[N_WIN_BLOCKS]
    m_sc, l_sc, acc_sc = rest[N_WIN_BLOCKS + 1:]
    qi = pl.program_id(1)
    s0 = qi * tq
    rows = NSA_GROUP * tq
    nl = amat_ref.shape[1]
    qpos = s0 + jnp.bitwise_and(lax.broadcasted_iota(jnp.int32, (rows, 1), 0), tq - 1)
    qpos_t = s0 + lax.broadcasted_iota(jnp.int32, (tq, 1), 0)
    gates = jax.nn.sigmoid(gl_ref[...])
    n_tiles = (s0 + tq + tk - 1) // tk

    for g in range(NSA_KV_HEADS):
        slope = _slopes_col(g, rows, tq)
        q_heads = [qn_ref[:, (g * NSA_GROUP + r) * LANES:(g * NSA_GROUP + r + 1) * LANES] for r in range(NSA_GROUP)]
        qg = jnp.concatenate(q_heads, axis=0)
        kc = kvc_ref[:, g * LANES:(g + 1) * LANES]
        vc = kvc_ref[:, (2 + g) * LANES:(3 + g) * LANES]
        cpos = lax.broadcasted_iota(jnp.int32, (1, nseg), 1) * CMP_STRIDE + (CMP_BLOCK - 1)
        dist = (qpos - cpos).astype(F32)
        p_c = _softmax_rows(_dot_nt(qg, kc) - slope * dist, dist >= 0)
        o_c = _dot(p_c.astype(BF16), vc)
        p_grp = p_c[0:tq]
        for r in range(1, NSA_GROUP):
            p_grp = p_grp + p_c[r * tq:(r + 1) * tq]
        amat = amat_ref[...]
        s_slc = sum(_dot(t, amat) for t in _split3(p_grp))
        blk = lax.broadcasted_iota(jnp.int32, (1, nl), 1)
        cur = lax.shift_right_logical(qpos_t, int(np.log2(SLC_BLOCK)))
        valid = blk * SLC_BLOCK <= qpos_t
        forced = jnp.logical_or(blk == 0, jnp.logical_or(blk == cur, blk == cur - 1))
        score = jnp.where(forced, FORCE_SCORE, jnp.where(valid, s_slc, NEG_BIG))
        sel = _top_blocks(score, N_SELECT)
        selneg = ((sel - 1.0) * MASK_BIG).astype(BF16)
        lane = lax.broadcasted_iota(jnp.int32, (1, LANES), 1)
        is_pos_lane = jnp.logical_or(lane == NSA_DH, lane == NSA_DH + 1)
        q_aug = jnp.concatenate(
            [jnp.concatenate([jnp.where(is_pos_lane, (2.0 ** (-(g * NSA_GROUP + r + 1.0))), q_heads[r].astype(F32)).astype(BF16),
                              selneg], axis=1) for r in range(NSA_GROUP)], axis=0)
        m_sc[...] = jnp.full(m_sc.shape, NEG_BIG, F32)
        l_sc[...] = jnp.zeros(l_sc.shape, F32)
        acc_sc[...] = jnp.zeros(acc_sc.shape, F32)

        def slc_tile(kt, causal):
            k0 = pl.multiple_of(kt * tk, tk)
            kk = slcp_ref[pl.ds(k0, tk), g * LANES:(g + 1) * LANES] + kc_ref[pl.ds(k0, tk), 0:LANES]
            k_aug = jnp.concatenate([kk, kc_ref[pl.ds(k0, tk), LANES:]], axis=1)
            vv = slcp_ref[pl.ds(k0, tk), (2 + g) * LANES:(3 + g) * LANES]
            s = _dot_nt(q_aug, k_aug)
            if causal:
                ok = (k0 + lax.broadcasted_iota(jnp.int32, (1, tk), 1)) <= qpos
                s = jnp.where(ok, s, NEG_BIG)
            m_prev = m_sc[...]
            m_new = jnp.maximum(m_prev, jnp.max(s, axis=1, keepdims=True))
            alpha = jnp.exp(m_prev - m_new)
            p = jnp.exp(s - m_new)
            if causal:
                p = jnp.where(ok, p, 0.0)
            l_sc[...] = alpha * l_sc[...] + jnp.sum(p, axis=1, keepdims=True)
            acc_sc[...] = alpha * acc_sc[...] + _dot(p.astype(BF16), vv)
            m_sc[...] = m_new

        def body(kt, carry):
            slc_tile(kt, False)
            return carry

        lax.fori_loop(0, n_tiles - 1, body, 0)
        slc_tile(n_tiles - 1, True)
        o_s = acc_sc[...] / jnp.maximum(l_sc[...], 1e-30)
        kw = jnp.concatenate([w[:, g * LANES:(g + 1) * LANES] for w in win_refs], axis=0)
        vw = jnp.concatenate([w[:, (2 + g) * LANES:(3 + g) * LANES] for w in win_refs], axis=0)
        nw = N_WIN_BLOCKS * 128
        wpos = s0 - WINDOW + lax.broadcasted_iota(jnp.int32, (1, nw), 1)
        dist_w = (qpos - wpos).astype(F32)
        mask_w = jnp.logical_and(jnp.logical_and(dist_w >= 0, dist_w <= WINDOW), wpos >= 0)
        p_w = _softmax_rows(_dot_nt(qg, kw) - slope * dist_w, mask_w)
        o_w = _dot(p_w.astype(BF16), vw)
        for r in range(NSA_GROUP):
            hd = g * NSA_GROUP + r
            sl = slice(r * tq, (r + 1) * tq)
            o_ref[:, hd * LANES:(hd + 1) * LANES] = (
                gates[:, 3 * hd:3 * hd + 1] * o_c[sl] + gates[:, 3 * hd + 1:3 * hd + 2] * o_s[sl]
                + gates[:, 3 * hd + 2:3 * hd + 3] * o_w[sl])


def _nsa_prompt(qn, gl, kvc, slcp, winp, batch, seq, tq, tk):
    assert tq == 128 and seq % tk == 0 and tk % tq == 0
    nseg = seq // CMP_STRIDE
    n_slc = seq // SLC_BLOCK
    nl = LANES * ((n_slc + LANES - 1) // LANES)
    assert nl == LANES, "selection blocks must fit one lane group"
    nq = seq // tq
    amat = jnp.asarray(_slc_overlap_matrix(nseg, nl), BF16)
    kc = jnp.asarray(_key_position_features(seq, nl), BF16)

    def win_spec(j):
        return pl.BlockSpec((128, KVP_W), lambda b, qi: (b * nq + jnp.maximum(qi - (N_WIN_BLOCKS - 1) + j, 0), 0))

    rows = NSA_GROUP * tq
    return pl.pallas_call(
        functools.partial(_nsa_prompt_kernel, tq=tq, tk=tk, nseg=nseg),
        grid=(batch, nq),
        in_specs=[pl.BlockSpec((tq, NSA_HEADS * LANES), lambda b, qi: (b * nq + qi, 0)),
                  pl.BlockSpec((tq, LANES), lambda b, qi: (b * nq + qi, 0)),
                  pl.BlockSpec((nseg, KVP_W), lambda b, qi: (b, 0)),
                  _full_spec(amat),
                  pl.BlockSpec((seq, KVP_W), lambda b, qi: (b, 0)),
                  _full_spec(kc)] + [win_spec(j) for j in range(N_WIN_BLOCKS)],
        out_specs=pl.BlockSpec((tq, NSA_HEADS * LANES), lambda b, qi: (b * nq + qi, 0)),
        out_shape=jax.ShapeDtypeStruct((batch * seq, NSA_HEADS * LANES), F32),
        scratch_shapes=[pltpu.VMEM((rows, 1), F32), pltpu.VMEM((rows, 1), F32), pltpu.VMEM((rows, LANES), F32)],
        compiler_params=_cparams(("parallel", "arbitrary")),
        name="nsa_prompt",
    )(qn, gl, kvc, amat, slcp, kc, *([winp] * N_WIN_BLOCKS))


def _silu(x):
    return x * jax.nn.sigmoid(x)


def _even_out_kernel(x_ref, lat_ref, gmla_ref, onsa_ref, gnsa_ref, wuv_ref, wo_ref, o_ref):
    o_mla = jnp.concatenate([_dot(lat_ref[hd], wuv_ref[hd]) for hd in range(MLA_HEADS)], axis=1)
    a = (o_mla * _silu(gmla_ref[...])).astype(BF16)
    b = (onsa_ref[...] * _silu(gnsa_ref[...])).astype(BF16)
    nm = MLA_HEADS * LANES
    o_ref[...] = x_ref[...] + _dot(a, wo_ref[0:nm]) + _dot(b, wo_ref[nm:])


def _even_output(x, lat, gmla, onsa, gnsa, wuv_ext, wo_ext, tm):
    t = x.shape[0]
    row = lambda w: pl.BlockSpec((tm, w), lambda i: (i, 0))
    return pl.pallas_call(
        _even_out_kernel,
        grid=(t // tm,),
        in_specs=[row(D_MODEL), pl.BlockSpec((MLA_HEADS, tm, MLA_KV_RANK), lambda i: (0, i, 0)),
                  row(MLA_HEADS * LANES), row(NSA_HEADS * LANES), row(NSA_HEADS * LANES),
                  _full_spec(wuv_ext), _full_spec(wo_ext)],
        out_specs=row(D_MODEL),
        out_shape=jax.ShapeDtypeStruct((t, D_MODEL), F32),
        compiler_params=_cparams(("parallel",)),
        name="even_out",
    )(x, lat, gmla, onsa, gnsa, wuv_ext, wo_ext)


_RET_QK = RET_HEADS * RET_DK
_RET_LOG_G = [float(np.log1p(-(2.0 ** (-5.0 - h)))) for h in range(RET_HEADS)]


def _rope_tables_ret(pos):
    half = RET_DK // 2
    inv = ROPE_BASE ** (-jnp.arange(half, dtype=F32) / half)
    ang = pos.astype(F32)[:, None] * inv
    cos, sin = jnp.cos(ang), jnp.sin(ang)
    return jnp.concatenate([cos, cos], 1), jnp.concatenate([-sin, sin], 1)


def _odd_proj_kernel(x_ref, gn_ref, w_ref, c_ref, s_ref, q_ref, k_ref, v_ref, g_ref):
    h = _rms(x_ref[...], gn_ref[...]).astype(BF16)
    c = c_ref[...]
    s = s_ref[...]
    half = RET_DK // 2

    def rot(z):
        return z * c + pltpu.roll(z, half, 1) * s

    for hd in range(RET_HEADS):
        sl = slice(hd * RET_DK, (hd + 1) * RET_DK)
        q_ref[:, sl] = rot(_dot(h, w_ref[:, sl])).astype(BF16)
        ks = slice(_RET_QK + hd * RET_DK, _RET_QK + (hd + 1) * RET_DK)
        k_ref[:, sl] = (rot(_dot(h, w_ref[:, ks])) * (RET_DK ** -0.5)).astype(BF16)
    v_ref[...] = _dot(h, w_ref[:, 2 * _RET_QK:2 * _RET_QK + RET_WIDTH])
    g_ref[...] = _dot(h, w_ref[:, 2 * _RET_QK + RET_WIDTH:])


def _odd_project(x, gn, w_bf, cos, sin, tm):
    t = x.shape[0]
    ntab = cos.shape[0] // tm
    row = lambda w: pl.BlockSpec((tm, w), lambda i: (i, 0))
    tab = pl.BlockSpec((tm, RET_DK), lambda i: (i % ntab, 0))
    return pl.pallas_call(
        _odd_proj_kernel,
        grid=(t // tm,),
        in_specs=[row(D_MODEL), _full_spec(gn), _full_spec(w_bf), tab, tab],
        out_specs=(row(_RET_QK), row(_RET_QK), row(RET_WIDTH), row(RET_WIDTH)),
        out_shape=(jax.ShapeDtypeStruct((t, _RET_QK), BF16), jax.ShapeDtypeStruct((t, _RET_QK), BF16),
                   jax.ShapeDtypeStruct((t, RET_WIDTH), F32), jax.ShapeDtypeStruct((t, RET_WIDTH), F32)),
        compiler_params=_cparams(("parallel",)),
        name="odd_proj",
    )(x, gn, w_bf, cos, sin)


def _group_norm_gate(o, gn_row, g):
    mu = jnp.mean(o, axis=-1, keepdims=True)
    var = jnp.mean(jnp.square(o - mu), axis=-1, keepdims=True)
    return _silu(g) * ((o - mu) * lax.rsqrt(var + EPS) * gn_row)


def _ret_prompt_kernel(q_ref, k_ref, v_ref, g_ref, gn_ref, y_ref, sfin_ref, s_sc, *, chunk):
    ci = pl.program_id(1)

    @pl.when(ci == 0)
    def _():
        s_sc[...] = jnp.zeros(s_sc.shape, F32)

    n_col = lax.broadcasted_iota(jnp.int32, (chunk, 1), 0).astype(F32)
    diff = n_col - lax.broadcasted_iota(jnp.int32, (1, chunk), 1).astype(F32)
    for hd in range(RET_HEADS):
        lg = _RET_LOG_G[hd]
        dmat = jnp.where(diff >= 0, jnp.exp(lg * jnp.maximum(diff, 0.0)), 0.0)
        xi = jnp.exp(lg * (n_col + 1.0))
        zeta = jnp.exp(lg * (chunk - 1.0 - n_col))
        qh = q_ref[:, hd * RET_DK:(hd + 1) * RET_DK]
        kh = k_ref[:, hd * RET_DK:(hd + 1) * RET_DK]
        vh = v_ref[:, hd * RET_DV:(hd + 1) * RET_DV]
        s_prev = s_sc[hd]
        inner = _dot_nt(qh, kh) * dmat
        o = _dot(inner.astype(BF16), vh.astype(BF16)) + _dot(qh, s_prev.astype(BF16)) * xi
        kv = lax.dot_general(kh, (vh * zeta).astype(BF16), (((0,), (0,)), ((), ())), preferred_element_type=F32)
        s_sc[hd] = float(np.exp(lg * chunk)) * s_prev + kv
        y_ref[:, hd * RET_DV:(hd + 1) * RET_DV] = _group_norm_gate(
            o, gn_ref[hd:hd + 1, :], g_ref[:, hd * RET_DV:(hd + 1) * RET_DV]).astype(y_ref.dtype)

    @pl.when(ci == pl.num_programs(1) - 1)
    def _():
        sfin_ref[0] = s_sc[...]


def _retention_prompt(q, k, v, g, gn, batch, seq):
    chunk = RET_CHUNK
    nc = seq // chunk
    row = lambda w: pl.BlockSpec((chunk, w), lambda b, c: (b * nc + c, 0))
    return pl.pallas_call(
        functools.partial(_ret_prompt_kernel, chunk=chunk),
        grid=(batch, nc),
        in_specs=[row(_RET_QK), row(_RET_QK), row(RET_WIDTH), row(RET_WIDTH), _full_spec(gn)],
        out_specs=(row(RET_WIDTH),
                   pl.BlockSpec((1, RET_HEADS, RET_DK, RET_DV), lambda b, c: (b, 0, 0, 0))),
        out_shape=(jax.ShapeDtypeStruct((batch * seq, RET_WIDTH), BF16),
                   jax.ShapeDtypeStruct((batch, RET_HEADS, RET_DK, RET_DV), F32)),
        scratch_shapes=[pltpu.VMEM((RET_HEADS, RET_DK, RET_DV), F32)],
        compiler_params=_cparams(("parallel", "arbitrary")),
        name="retention_prompt",
    )(q, k, v, g, gn)


def _odd_out_kernel(x_ref, y_ref, w_ref, fn_ref, o_ref):
    x2 = x_ref[...] + _dot(y_ref[...], w_ref[...])
    o_ref[...] = _rms(x2, fn_ref[...])


def _odd_output(x, y, w_bf, fn, tm):
    t = x.shape[0]
    row = lambda w: pl.BlockSpec((tm, w), lambda i: (i, 0))
    return pl.pallas_call(
        _odd_out_kernel,
        grid=(t // tm,),
        in_specs=[row(D_MODEL), row(RET_WIDTH), _full_spec(w_bf), _full_spec(fn)],
        out_specs=row(D_MODEL),
        out_shape=jax.ShapeDtypeStruct((t, D_MODEL), F32),
        compiler_params=_cparams(("parallel",)),
        name="odd_out",
    )(x, y, w_bf, fn)


def _ret_decode_kernel(q_ref, k_ref, v_ref, g_ref, gn_ref, s_ref, y_ref, snew_ref):
    eye = (lax.broadcasted_iota(jnp.int32, (RET_DK, RET_DK), 0)
           == lax.broadcasted_iota(jnp.int32, (RET_DK, RET_DK), 1))
    for hd in range(RET_HEADS):
        gam = float(np.exp(_RET_LOG_G[hd]))
        qh = q_ref[0, :, hd * RET_DK:(hd + 1) * RET_DK]
        kh = k_ref[0, :, hd * RET_DK:(hd + 1) * RET_DK]
        vh = v_ref[0, :, hd * RET_DV:(hd + 1) * RET_DV].astype(BF16).astype(F32)
        s_prev = s_ref[0, hd]
        inner = jnp.sum(qh.astype(F32) * kh.astype(F32), axis=1, keepdims=True).astype(BF16).astype(F32)
        qs = _dot(jnp.broadcast_to(qh, (8, RET_DK)), s_prev.astype(BF16))[0:1]
        o = inner * vh + qs * gam
        k_col = jnp.sum(jnp.where(eye, jnp.broadcast_to(kh.astype(F32), (RET_DK, RET_DK)), 0.0), axis=1, keepdims=True)
        snew_ref[0, hd] = gam * s_prev + k_col * vh
        y_ref[0, :, hd * RET_DV:(hd + 1) * RET_DV] = _group_norm_gate(
            o, gn_ref[hd:hd + 1, :], g_ref[0, :, hd * RET_DV:(hd + 1) * RET_DV]).astype(y_ref.dtype)


def _retention_decode(q, k, v, g, gn, state):
    db = q.shape[0]
    r3 = lambda a: a.reshape(db, 1, a.shape[-1])
    vec = lambda w: pl.BlockSpec((1, 1, w), lambda b: (b, 0, 0))
    st = pl.BlockSpec((1, RET_HEADS, RET_DK, RET_DV), lambda b: (b, 0, 0, 0))
    y, s_new = pl.pallas_call(
        _ret_decode_kernel,
        grid=(db,),
        in_specs=[vec(_RET_QK), vec(_RET_QK), vec(RET_WIDTH), vec(RET_WIDTH), _full_spec(gn), st],
        out_specs=(vec(RET_WIDTH), st),
        out_shape=(jax.ShapeDtypeStruct((db, 1, RET_WIDTH), BF16),
                   jax.ShapeDtypeStruct(state.shape, state.dtype)),
        compiler_params=_cparams(("parallel",)),
        name="retention_decode",
    )(r3(q), r3(k), r3(v), r3(g), gn, state)
    return y.reshape(db, RET_WIDTH), s_new


PAGES_PER_STEP = 16
SUB = 8


def _mla_decode_kernel(pt_ref, q_ref, new_ref, *rest):
    del pt_ref
    page_refs = rest[:PAGES_PER_STEP]
    o_ref, m_sc, l_sc, acc_sc = rest[PAGES_PER_STEP:]
    i = pl.program_id(1)

    @pl.when(i == 0)
    def _():
        m_sc[...] = jnp.full(m_sc.shape, NEG_BIG, F32)
        l_sc[...] = jnp.zeros(l_sc.shape, F32)
        acc_sc[...] = jnp.zeros(acc_sc.shape, F32)

    q = q_ref[0]
    rows = jnp.concatenate([p[...] for p in page_refs], axis=0).astype(BF16)
    s = _dot_nt(q, rows) * MLA_SCALE
    m_prev = m_sc[...]
    m_new = jnp.maximum(m_prev, jnp.max(s, axis=1, keepdims=True))
    alpha = jnp.exp(m_prev - m_new)
    p = jnp.exp(s - m_new)
    l_sc[...] = alpha * l_sc[...] + jnp.sum(p, axis=1, keepdims=True)
    acc_sc[...] = alpha * acc_sc[...] + _dot(p.astype(BF16), rows[:, :MLA_KV_RANK])
    m_sc[...] = m_new

    @pl.when(i == pl.num_programs(1) - 1)
    def _():
        new = new_ref[0].astype(BF16).astype(F32)
        s_n = jnp.sum(q.astype(F32) * new, axis=1, keepdims=True) * MLA_SCALE
        m_prev = m_sc[...]
        m_new = jnp.maximum(m_prev, s_n)
        alpha = jnp.exp(m_prev - m_new)
        p_n = jnp.exp(s_n - m_new)
        l = alpha * l_sc[...] + p_n
        acc = alpha * acc_sc[...] + p_n.astype(BF16).astype(F32) * new[:, :MLA_KV_RANK]
        o_ref[0] = (acc / jnp.maximum(l, 1e-30)).astype(o_ref.dtype)


def _mla_decode(q_dec, rows_new, pool, page_table):
    db, n_pages = page_table.shape
    assert n_pages % PAGES_PER_STEP == 0
    steps = n_pages // PAGES_PER_STEP
    w = pool.shape[-1]

    def page_spec(j):
        return pl.BlockSpec((None, PAGE_SIZE, w),
                            lambda b, i, pt: (pt[b * n_pages + i * PAGES_PER_STEP + j], 0, 0))

    grid_spec = pltpu.PrefetchScalarGridSpec(
        num_scalar_prefetch=1,
        grid=(db, steps),
        in_specs=[pl.BlockSpec((1, MLA_HEADS, w), lambda b, i, pt: (b, 0, 0)),
                  pl.BlockSpec((1, 1, w), lambda b, i, pt: (b, 0, 0))]
                 + [page_spec(j) for j in range(PAGES_PER_STEP)],
        out_specs=pl.BlockSpec((1, MLA_HEADS, MLA_KV_RANK), lambda b, i, pt: (b, 0, 0)),
        scratch_shapes=[pltpu.VMEM((MLA_HEADS, 1), F32), pltpu.VMEM((MLA_HEADS, 1), F32),
                        pltpu.VMEM((MLA_HEADS, MLA_KV_RANK), F32)],
    )
    return pl.pallas_call(
        _mla_decode_kernel,
        grid_spec=grid_spec,
        out_shape=jax.ShapeDtypeStruct((db, MLA_HEADS, MLA_KV_RANK), BF16),
        compiler_params=_cparams(("parallel", "arbitrary")),
        name="mla_decode",
    )(page_table.reshape(-1), q_dec, rows_new.reshape(db, 1, w), *([pool] * PAGES_PER_STEP))


def _decode_blocks(p_len, n_new):
    n_slc = -(-(p_len + n_new) // SLC_BLOCK)
    return n_slc, LANES * ((n_slc + LANES - 1) // LANES)


def _cmp_decode_kernel(pt_ref, qn_ref, cnew_ref, pe_ref, w_ref, amat_ref, *rest, p_len):
    del pt_ref
    page_refs = rest[:PAGES_PER_STEP]
    oc_ref, sslc_ref, h1_sc, h2_sc = rest[PAGES_PER_STEP:]
    i = pl.program_id(1)
    seg_per_page = PAGE_SIZE // CMP_STRIDE
    n_step = PAGES_PER_STEP * seg_per_page
    nseg = h1_sc.shape[0]
    x = jnp.concatenate([p[...] for p in page_refs], axis=0)
    r0 = pl.multiple_of(i * n_step, n_step)
    h1_sc[pl.ds(r0, n_step), :] = _dot((x + pe_ref[0]).astype(BF16), w_ref[0])
    h2_sc[pl.ds(r0, n_step), :] = _dot((x + pe_ref[1]).astype(BF16), w_ref[1])

    @pl.when(i == pl.num_programs(1) - 1)
    def _():
        h2n = _dot((jnp.broadcast_to(cnew_ref[0], (SUB, SEG_W)) + pe_ref[1]).astype(BF16), w_ref[1])[0:1]
        h2 = h2_sc[...]
        seg = lax.broadcasted_iota(jnp.int32, (nseg, 1), 0)
        kvc = (h1_sc[...] + jnp.where(seg == nseg - 1, h2n, pltpu.roll(h2, nseg - 1, 0))).astype(BF16)
        qpos = p_len
        cpos = lax.broadcasted_iota(jnp.int32, (1, nseg), 1) * CMP_STRIDE + (CMP_BLOCK - 1)
        dist = (qpos - cpos).astype(F32)
        row = lax.broadcasted_iota(jnp.int32, (SUB, 1), 0)
        amat = amat_ref[...]
        for g in range(NSA_KV_HEADS):
            slope = jnp.zeros((SUB, 1), F32)
            for rr in range(NSA_GROUP):
                slope = jnp.where(row == rr, 2.0 ** (-(g * NSA_GROUP + rr + 1.0)), slope)
            qg = qn_ref[0, g]
            p_c = _softmax_rows(_dot_nt(qg, kvc[:, g * LANES:(g + 1) * LANES]) - slope * dist, dist >= 0)
            p_c = jnp.where(row < NSA_GROUP, p_c, 0.0)
            oc_ref[0, g] = _dot(p_c.astype(BF16), kvc[:, (2 + g) * LANES:(3 + g) * LANES])
            s_rows = sum(_dot(t, amat) for t in _split3(p_c))
            sslc_ref[0, g] = jnp.broadcast_to(jnp.sum(s_rows, axis=0, keepdims=True), s_rows.shape)


def _cmp_decode(qn_dec, cmp_new, pool, page_table, w_big, pe_big):
    db, n_pages = page_table.shape
    steps = n_pages // PAGES_PER_STEP
    p_len = n_pages * PAGE_SIZE
    nseg = p_len // CMP_STRIDE
    n_slc, nl = _decode_blocks(p_len, 1)
    amat = jnp.asarray(_slc_overlap_matrix(nseg, nl), BF16)
    seg_per_page = PAGE_SIZE // CMP_STRIDE
    cnew_seg = jnp.pad(cmp_new, ((0, 0), (0, SEG_W - NSA_KV_W))).reshape(db, 1, SEG_W)

    def page_spec(j):
        return pl.BlockSpec((None, seg_per_page, SEG_W),
                            lambda b, i, pt: (pt[b * n_pages + i * PAGES_PER_STEP + j], 0, 0))

    grid_spec = pltpu.PrefetchScalarGridSpec(
        num_scalar_prefetch=1,
        grid=(db, steps),
        in_specs=[pl.BlockSpec((1, NSA_KV_HEADS, SUB, LANES), lambda b, i, pt: (b, 0, 0, 0)),
                  pl.BlockSpec((1, 1, SEG_W), lambda b, i, pt: (b, 0, 0)),
                  pl.BlockSpec(pe_big.shape, lambda b, i, pt: (0, 0, 0)),
                  pl.BlockSpec(w_big.shape, lambda b, i, pt: (0, 0, 0)),
                  pl.BlockSpec(amat.shape, lambda b, i, pt: (0, 0))]
                 + [page_spec(j) for j in range(PAGES_PER_STEP)],
        out_specs=(pl.BlockSpec((1, NSA_KV_HEADS, SUB, LANES), lambda b, i, pt: (b, 0, 0, 0)),
                   pl.BlockSpec((1, NSA_KV_HEADS, SUB, nl), lambda b, i, pt: (b, 0, 0, 0))),
        scratch_shapes=[pltpu.VMEM((nseg, KVP_W), F32), pltpu.VMEM((nseg, KVP_W), F32)],
    )
    return pl.pallas_call(
        functools.partial(_cmp_decode_kernel, p_len=p_len),
        grid_spec=grid_spec,
        out_shape=(jax.ShapeDtypeStruct((db, NSA_KV_HEADS, SUB, LANES), F32),
                   jax.ShapeDtypeStruct((db, NSA_KV_HEADS, SUB, nl), F32)),
        compiler_params=_cparams(("parallel", "arbitrary")),
        name="nsa_cmp_decode",
    )(page_table.reshape(-1), qn_dec, cnew_seg, pe_big, w_big, amat, *([pool] * PAGES_PER_STEP))


def _topk_decode_kernel(s_ref, idx_ref, *, qpos):
    s_slc = s_ref[...]
    blk = lax.broadcasted_iota(jnp.int32, s_slc.shape, 1)
    cur = qpos // SLC_BLOCK
    valid = blk * SLC_BLOCK <= qpos
    forced = jnp.logical_or(blk == 0, jnp.logical_or(blk == cur, blk == cur - 1))
    score = jnp.where(forced, FORCE_SCORE, jnp.where(valid, s_slc, NEG_BIG))
    lane = lax.broadcasted_iota(jnp.int32, idx_ref.shape, 1)
    out = jnp.full(idx_ref.shape, -1, jnp.int32)
    for t in range(N_SELECT):
        m = jnp.max(score, axis=1, keepdims=True)
        idx = jnp.min(jnp.where(score == m, blk, score.shape[1]), axis=1, keepdims=True)
        out = jnp.where(lane == t, jnp.where(m > 0.5 * NEG_BIG, idx, -1), out)
        score = jnp.where(blk == idx, 2.0 * NEG_BIG, score)
    idx_ref[...] = out


def _topk_decode(s_slc, qpos):
    rows = s_slc.shape[0]
    return pl.pallas_call(
        functools.partial(_topk_decode_kernel, qpos=qpos),
        grid=(1,),
        in_specs=[_full_spec(s_slc)],
        out_specs=pl.BlockSpec((rows, LANES), lambda i: (0, 0)),
        out_shape=jax.ShapeDtypeStruct((rows, LANES), jnp.int32),
        compiler_params=_cparams(("arbitrary",)),
        name="nsa_topk_decode",
    )(s_slc)


def _sel_decode_kernel(pt_ref, sel_ref, qn_ref, gl_ref, oc_ref, snew_ref, wnew_ref, win_ref, *rest, p_len):
    del pt_ref
    nb = NSA_KV_HEADS * N_SELECT
    blk_refs = rest[:nb]
    o_ref, nwin_ref = rest[nb:]
    b = pl.program_id(0)
    qpos = p_len
    n_past_blocks = p_len // SLC_BLOCK
    row = lax.broadcasted_iota(jnp.int32, (SUB, 1), 0)
    lane = lax.broadcasted_iota(jnp.int32, (SUB, LANES), 1)
    gates = jnp.broadcast_to(jax.nn.sigmoid(gl_ref[0]), (SUB, LANES))
    nw = win_ref.shape[1]
    win = win_ref[0]
    wnew = wnew_ref[0]
    snew = snew_ref[0]

    def merge_new(s, mask, kv_bf, q8, new_row, slope, use_new):
        new_f = new_row.astype(BF16).astype(F32)
        s_n = jnp.sum(q8.astype(F32) * new_f, axis=1, keepdims=True)
        s = jnp.where(mask, s, NEG_BIG)
        m = jnp.maximum(jnp.max(s, axis=1, keepdims=True), jnp.where(use_new, s_n, NEG_BIG))
        e = jnp.where(mask, jnp.exp(s - m), 0.0)
        e_n = jnp.where(use_new, jnp.exp(s_n - m), 0.0)
        denom = jnp.maximum(jnp.sum(e, axis=1, keepdims=True) + e_n, 1e-30)
        p = e / denom
        p_n = e_n / denom
        o = _dot(p.astype(BF16), kv_bf) + p_n.astype(BF16).astype(F32) * new_f
        return pltpu.roll(o, LANES - NSA_DH, 1)

    for g in range(NSA_KV_HEADS):
        slope = jnp.zeros((SUB, 1), F32)
        for rr in range(NSA_GROUP):
            slope = jnp.where(row == rr, 2.0 ** (-(g * NSA_GROUP + rr + 1.0)), slope)
        q8 = qn_ref[0, g]
        kv = jnp.concatenate([r[...] for r in blk_refs[g * N_SELECT:(g + 1) * N_SELECT]], axis=0).astype(BF16)
        nk = N_SELECT * SLC_BLOCK
        slot = lax.shift_right_logical(lax.broadcasted_iota(jnp.int32, (1, nk), 1), int(np.log2(SLC_BLOCK)))
        off = jnp.bitwise_and(lax.broadcasted_iota(jnp.int32, (1, nk), 1), SLC_BLOCK - 1)
        kblk = jnp.zeros((1, nk), jnp.int32)
        use_new = False
        for t in range(N_SELECT):
            st = sel_ref[(b * NSA_KV_HEADS + g) * N_SELECT + t]
            kblk = jnp.where(slot == t, st, kblk)
            use_new = jnp.logical_or(use_new, st == n_past_blocks)
        kpos = kblk * SLC_BLOCK + off
        ok = jnp.logical_and(jnp.logical_and(kblk >= 0, kblk < n_past_blocks), kpos <= qpos)
        dist = (qpos - kpos).astype(F32)
        o_s = merge_new(_dot_nt(q8, kv) - slope * dist, ok, kv, q8, snew[:, g * LANES:(g + 1) * LANES], slope, use_new)
        kvw = win[:, g * LANES:(g + 1) * LANES].astype(BF16)
        wpos = p_len - nw + lax.broadcasted_iota(jnp.int32, (1, nw), 1)
        dist_w = (qpos - wpos).astype(F32)
        mask_w = jnp.logical_and(jnp.logical_and(dist_w >= 0, dist_w <= WINDOW), wpos >= 0)
        o_w = merge_new(_dot_nt(q8, kvw) - slope * dist_w, mask_w, kvw, q8, wnew[:, g * LANES:(g + 1) * LANES], slope, True)
        hd = g * NSA_GROUP + row
        gate = lambda c: jnp.sum(jnp.where(lane == 3 * hd + c, gates, 0.0), axis=1, keepdims=True)
        mix = gate(0) * oc_ref[0, g] + gate(1) * o_s + gate(2) * o_w
        o_ref[0, g * NSA_GROUP:(g + 1) * NSA_GROUP, :] = mix[0:NSA_GROUP]
    r_idx = lax.broadcasted_iota(jnp.int32, (nw, 1), 0)
    nwin_ref[0] = jnp.where(r_idx == nw - 1, wnew, pltpu.roll(win, nw - 1, 0))


def _sel_decode(qn_dec, gl, o_c, sel_idx, slc_new, win_new, slc_pool, win_buf, page_table):
    db, n_pages = page_table.shape
    p_len = n_pages * PAGE_SIZE
    n_past_blocks = p_len // SLC_BLOCK
    per_page = PAGE_SIZE // SLC_BLOCK
    nw = win_buf.shape[1]

    def blk_spec(g, t):
        def imap(b, pt, sel):
            j = jnp.clip(sel[(b * NSA_KV_HEADS + g) * N_SELECT + t], 0, n_past_blocks - 1)
            return (pt[b * n_pages + j // per_page] * per_page + j % per_page, 0, g)
        return pl.BlockSpec((None, SLC_BLOCK, LANES), imap)

    vec = lambda w: pl.BlockSpec((1, 1, w), lambda b, pt, sel: (b, 0, 0))
    grp = pl.BlockSpec((1, NSA_KV_HEADS, SUB, LANES), lambda b, pt, sel: (b, 0, 0, 0))
    grid_spec = pltpu.PrefetchScalarGridSpec(
        num_scalar_prefetch=2,
        grid=(db,),
        in_specs=[grp, vec(LANES), grp, vec(NSA_KV_W), vec(NSA_KV_W),
                  pl.BlockSpec((1, nw, NSA_KV_W), lambda b, pt, sel: (b, 0, 0))]
                 + [blk_spec(g, t) for g in range(NSA_KV_HEADS) for t in range(N_SELECT)],
        out_specs=(pl.BlockSpec((1, NSA_HEADS, LANES), lambda b, pt, sel: (b, 0, 0)),
                   pl.BlockSpec((1, nw, NSA_KV_W), lambda b, pt, sel: (b, 0, 0))),
    )
    r3 = lambda a: a.reshape(db, 1, a.shape[-1])
    return pl.pallas_call(
        functools.partial(_sel_decode_kernel, p_len=p_len),
        grid_spec=grid_spec,
        out_shape=(jax.ShapeDtypeStruct((db, NSA_HEADS, LANES), F32),
                   jax.ShapeDtypeStruct(win_buf.shape, win_buf.dtype)),
        compiler_params=_cparams(("parallel",)),
        name="nsa_sel_decode",
    )(page_table.reshape(-1), sel_idx, qn_dec, r3(gl), o_c, r3(slc_new), r3(win_new), win_buf,
      *([slc_pool] * (NSA_KV_HEADS * N_SELECT)))


TM_PROMPT = 256
TQ_MLA, TK_MLA = 256, 512
TQ_NSA, TK_NSA = 128, 512


def _mla_rows(rows):
    return jnp.concatenate([rows[:, :MLA_KV_RANK], rows[:, MLA_KV_RANK:MLA_KV_RANK + HALF_ROPE],
                            rows[:, MLA_KV_RANK + LANES:MLA_KV_RANK + LANES + HALF_ROPE]], axis=1)


def kernel(x_prompt, x_sample, cache_mla, cache_nsa_cmp, cache_nsa_slc, state_nsa_win, state_ret, page_table,
           norm_even, w_in_even, mla_gq, mla_gkv, mla_wuq, mla_wuk, mla_wuv, nsa_cmp_w, nsa_cmp_pe, w_out_even,
           norm_odd, w_in_odd, ret_gn, w_out_odd, final_norm):
    b, s, d = x_prompt.shape
    db, n_new, _ = x_sample.shape
    assert n_new == 1 and norm_even.shape[0] == 1 and norm_odd.shape[0] == 1
    n_pages = page_table.shape[1]
    p_len = n_pages * PAGE_SIZE
    kv_row = (NSA_KV_HEADS, 2, NSA_DH)
    tm = min(TM_PROMPT, b * s)

    w_ext, wuq_ext, wuk_ext, wuv_ext, wo_ext = _prep_even_weights(
        w_in_even[0], mla_wuq[0], mla_wuk[0], mla_wuv[0], w_out_even[0])
    w_big, pe_big = _prep_cmp_weights(nsa_cmp_w[0], nsa_cmp_pe[0])
    w_odd = w_in_odd[0].astype(BF16)
    wo_odd = w_out_odd[0].astype(BF16)
    gn_e, gq, gkv = norm_even[0][None], mla_gq[0][None], mla_gkv[0][None]
    gn_o, fn = norm_odd[0][None], final_norm[None]

    xp = x_prompt.reshape(b * s, d)
    pos_p = jnp.arange(s)
    cos_m, sin_m = _rope_tables_mla(pos_p)
    (qext, rows, kext, gmla, qn, cmp, slc, slcp, win, winp, gl, gnsa) = _even_project(
        xp, gn_e, w_ext, gq, gkv, wuq_ext, wuk_ext, cos_m, sin_m, tm)
    lat = _mla_prompt_attention(qext, kext, b, s, min(TQ_MLA, s), min(TK_MLA, s))
    kvc = _cmp_prompt(cmp, w_big, pe_big, b, s)
    onsa = _nsa_prompt(qn, gl, kvc, slcp, winp, b, s, TQ_NSA, min(TK_NSA, s))
    x1 = _even_output(xp, lat, gmla, onsa, gnsa, wuv_ext, wo_ext, tm)
    cos_r, sin_r = _rope_tables_ret(pos_p)
    q, k, v, g = _odd_project(x1, gn_o, w_odd, cos_r, sin_r, tm)
    y, ret_p = _retention_prompt(q, k, v, g, ret_gn[0], b, s)
    y_prompt = _odd_output(x1, y, wo_odd, fn, tm).reshape(b, s, d)
    nwin = min(WINDOW, s)
    mla_p = _mla_rows(rows).reshape(1, b, s, MLA_KV_RANK + MLA_ROPE)
    cmp_p = cmp.reshape((1, b, s) + kv_row)
    slc_p = slc.reshape((1, b, s) + kv_row)
    win_p = win.reshape((b, s) + kv_row)[:, s - nwin:][None]

    xs = x_sample.reshape(db, d)
    pos_s = p_len + jnp.arange(n_new)
    cos_s, sin_s = [jnp.broadcast_to(t, (db, LANES)) for t in _rope_tables_mla(pos_s)]
    (qext_s, rows_s, _, gmla_s, qn_s, cmp_s, slc_s, _, win_s, _, gl_s, gnsa_s) = _even_project(
        xs, gn_e, w_ext, gq, gkv, wuq_ext, wuk_ext, cos_s, sin_s, db)
    rows_new = _mla_rows(rows_s)
    rope = lambda a: a.astype(F32).reshape(MLA_HEADS, db, MLA_HEADS, HALF_ROPE).sum(2).astype(BF16)
    q_dec = jnp.concatenate([qext_s[:, :, :MLA_KV_RANK], rope(qext_s[:, :, MLA_KV_RANK:MLA_KV_RANK + LANES]),
                             rope(qext_s[:, :, MLA_KV_RANK + LANES:])], axis=-1)
    lat_s = _mla_decode(jnp.transpose(q_dec, (1, 0, 2)), rows_new, cache_mla.reshape(cache_mla.shape[1:]), page_table)
    qn_dec = jnp.pad(qn_s.reshape(db, NSA_KV_HEADS, NSA_GROUP, LANES), ((0, 0), (0, 0), (0, SUB - NSA_GROUP), (0, 0)))
    pool = cache_nsa_cmp.shape[1]
    o_c, s_slc = _cmp_decode(qn_dec, cmp_s, cache_nsa_cmp.reshape(pool, PAGE_SIZE // CMP_STRIDE, SEG_W),
                             page_table, w_big, pe_big)
    sel = _topk_decode(s_slc[:, :, 0, :].reshape(db * NSA_KV_HEADS, -1), p_len)
    sel_idx = sel[:, :N_SELECT].reshape(-1)
    onsa_s, win_new = _sel_decode(
        qn_dec, gl_s, o_c, sel_idx, slc_s, win_s,
        cache_nsa_slc.reshape(pool * (PAGE_SIZE // SLC_BLOCK), SLC_BLOCK, NSA_KV_W),
        state_nsa_win.reshape(db, state_nsa_win.shape[2], NSA_KV_W), page_table)
    x1s = _even_output(xs, jnp.transpose(lat_s, (1, 0, 2)), gmla_s, onsa_s.reshape(db, NSA_HEADS * LANES), gnsa_s,
                       wuv_ext, wo_ext, db)
    cos_rs, sin_rs = [jnp.broadcast_to(t, (db, RET_DK)) for t in _rope_tables_ret(pos_s)]
    q, k, v, g = _odd_project(x1s, gn_o, w_odd, cos_rs, sin_rs, db)
    ys, ret_s = _retention_decode(q, k, v, g, ret_gn[0], state_ret.reshape(state_ret.shape[1:]))
    y_sample = _odd_output(x1s, ys, wo_odd, fn, db).reshape(db, n_new, d)
    mla_s = rows_new.reshape(1, db, n_new, MLA_KV_RANK + MLA_ROPE)
    cmp_so = cmp_s.reshape((1, db, n_new) + kv_row)
    slc_so = slc_s.reshape((1, db, n_new) + kv_row)
    win_so = win_new.reshape((1, db, win_new.shape[1]) + kv_row)
    return (y_prompt, y_sample, mla_p, cmp_p, slc_p, win_p, ret_p[None],
            mla_s, cmp_so, slc_so, win_so, ret_s[None])
```

```python
import functools

import numpy as np
import jax
import jax.numpy as jnp
from jax import lax
from jax.experimental import pallas as pl
from jax.experimental.pallas import tpu as pltpu

F32 = jnp.float32
BF16 = jnp.bfloat16

D_MODEL = 1024
PAGE_SIZE = 128
MLA_HEADS = 8
MLA_NOPE = 64
MLA_ROPE = 32
MLA_V = 64
MLA_Q_RANK = 768
MLA_KV_RANK = 256
MLA_WIDTH = MLA_HEADS * MLA_V
MLA_SCALE = (MLA_NOPE + MLA_ROPE) ** -0.5
NSA_HEADS = 8
NSA_KV_HEADS = 2
NSA_GROUP = NSA_HEADS // NSA_KV_HEADS
NSA_DH = 64
NSA_WIDTH = NSA_HEADS * NSA_DH
NSA_KV_W = NSA_KV_HEADS * 2 * NSA_DH
CMP_BLOCK = 32
CMP_STRIDE = 16
SLC_BLOCK = 64
N_SELECT = 16
WINDOW = 512
FORCE_SCORE = 1e4
RET_HEADS = 8
RET_DK = 128
RET_DV = 256
RET_WIDTH = RET_HEADS * RET_DV
RET_CHUNK = 128
ROPE_BASE = 10000.0
EPS = 1e-6
EVEN_SPLITS = (MLA_Q_RANK, MLA_KV_RANK, MLA_ROPE, MLA_WIDTH, NSA_WIDTH, NSA_KV_W, NSA_KV_W, NSA_KV_W,
               3 * NSA_HEADS, NSA_WIDTH)

LANES = 128
VMEM_LIMIT_BYTES = 56 * 1024 * 1024
NEG_BIG = -1e30
MASK_BIG = 16384.0

HALF_ROPE = MLA_ROPE // 2


def _cparams(sem):
    return pltpu.CompilerParams(dimension_semantics=sem, vmem_limit_bytes=VMEM_LIMIT_BYTES)


def _full_spec(a):
    nd = a.ndim
    return pl.BlockSpec(a.shape, lambda *_: (0,) * nd)


def _rms(x, g):
    y = x * lax.rsqrt(jnp.mean(x * x, axis=-1, keepdims=True) + EPS)
    return y * g


def _dot(a, b):
    return jnp.dot(a, b, preferred_element_type=F32)


def _dot_nt(a, b):
    return lax.dot_general(a, b, (((1,), (1,)), ((), ())), preferred_element_type=F32)


def _fold_lanes(x, op):
    n = x.shape[1]
    if n % LANES:
        return x
    parts = [x[:, i:i + LANES] for i in range(0, n, LANES)]
    while len(parts) > 1:
        parts = [op(parts[i], parts[i + 1]) if i + 1 < len(parts) else parts[i] for i in range(0, len(parts), 2)]
    return parts[0]


def _row_max(x):
    return jnp.max(_fold_lanes(x, jnp.maximum), axis=1, keepdims=True)


def _row_sum(x):
    return jnp.sum(_fold_lanes(x, jnp.add), axis=1, keepdims=True)


_EVEN_GROUPS = (
    ("cq", MLA_Q_RANK), ("ckv", MLA_KV_RANK), ("kr1", LANES), ("kr2", LANES),
    ("gmla", MLA_HEADS * LANES), ("qn", NSA_HEADS * LANES), ("cmp", NSA_KV_W),
    ("slc", NSA_KV_W), ("slcp", 4 * LANES), ("win", NSA_KV_W), ("winp", 4 * LANES),
    ("gl", LANES), ("gnsa", NSA_HEADS * LANES),
)
_EVEN_OFF = {}
_o = 0
for _n, _w in _EVEN_GROUPS:
    _EVEN_OFF[_n] = (_o, _o + _w)
    _o += _w
EVEN_EXT = _o


def _pad_heads(w, nh, dh):
    k = w.shape[0]
    w = w.reshape(k, nh, dh)
    return jnp.pad(w, ((0, 0), (0, 0), (0, LANES - dh))).reshape(k, nh * LANES)


def _kv_pad(w):
    k = w.shape[0]
    w4 = w.reshape(k, NSA_KV_HEADS, 2, NSA_DH)
    w4 = jnp.transpose(w4, (0, 2, 1, 3))
    return jnp.pad(w4, ((0, 0), (0, 0), (0, 0), (0, LANES - NSA_DH))).reshape(k, 4 * LANES)


def _prep_even_weights(w_in, wuq, wuk, wuv, w_out):
    offs = np.cumsum((0,) + EVEN_SPLITS)
    cq, ckv, kr, g_mla, q_nsa, kv_cmp, kv_slc, kv_win, gl, g_nsa = [
        w_in[:, offs[i]:offs[i + 1]] for i in range(len(EVEN_SPLITS))]
    parts = {
        "cq": cq, "ckv": ckv,
        "kr1": jnp.tile(kr[:, :HALF_ROPE], (1, MLA_HEADS)),
        "kr2": jnp.tile(kr[:, HALF_ROPE:], (1, MLA_HEADS)),
        "gmla": _pad_heads(g_mla, MLA_HEADS, MLA_V),
        "qn": _pad_heads(q_nsa, NSA_HEADS, NSA_DH),
        "cmp": kv_cmp, "slc": kv_slc, "slcp": _kv_pad(kv_slc),
        "win": kv_win, "winp": _kv_pad(kv_win),
        "gl": jnp.pad(gl, ((0, 0), (0, LANES - gl.shape[1]))),
        "gnsa": _pad_heads(g_nsa, NSA_HEADS, NSA_DH),
    }
    w_ext = jnp.concatenate([parts[n] for n, _ in _EVEN_GROUPS], axis=1).astype(BF16)
    c = wuq.shape[0]
    nope = jnp.pad(wuq[:, :, :MLA_NOPE], ((0, 0), (0, 0), (0, LANES - MLA_NOPE))).reshape(c, MLA_HEADS * LANES)
    r1 = wuq[:, :, MLA_NOPE:MLA_NOPE + HALF_ROPE].reshape(c, MLA_HEADS * HALF_ROPE)
    r2 = wuq[:, :, MLA_NOPE + HALF_ROPE:].reshape(c, MLA_HEADS * HALF_ROPE)
    wuq_ext = jnp.concatenate([nope, r1, r2], axis=1).astype(BF16)
    wuk_ext = jnp.pad(jnp.transpose(wuk, (1, 2, 0)), ((0, 0), (0, LANES - MLA_NOPE), (0, 0))).astype(BF16)
    wuv_ext = jnp.pad(jnp.transpose(wuv, (1, 0, 2)), ((0, 0), (0, 0), (0, LANES - MLA_V))).astype(BF16)
    d = w_out.shape[1]
    wo = w_out.reshape(MLA_HEADS + NSA_HEADS, MLA_V, d)
    wo_ext = jnp.pad(wo, ((0, 0), (0, LANES - MLA_V), (0, 0))).reshape((MLA_HEADS + NSA_HEADS) * LANES, d).astype(BF16)
    return w_ext, wuq_ext, wuk_ext, wuv_ext, wo_ext


def _rope_tables_mla(pos):
    inv = ROPE_BASE ** (-jnp.arange(HALF_ROPE, dtype=F32) / HALF_ROPE)
    ang = pos.astype(F32)[:, None] * inv
    return jnp.tile(jnp.cos(ang), (1, LANES // HALF_ROPE)), jnp.tile(jnp.sin(ang), (1, LANES // HALF_ROPE))


def _even_proj_kernel(x_ref, gn_ref, w_ref, gq_ref, gkv_ref, wuq_ref, wuk_ref, cos_ref, sin_ref,
                      qext_ref, rows_ref, kext_ref, gmla_ref, qn_ref, cmp_ref, slc_ref, slcp_ref,
                      win_ref, winp_ref, gl_ref, gnsa_ref):
    h = _rms(x_ref[...], gn_ref[...]).astype(BF16)

    def proj(name):
        a, b = _EVEN_OFF[name]
        return _dot(h, w_ref[:, a:b])

    cos = cos_ref[...]
    sin = sin_ref[...]
    cqn = _rms(proj("cq"), gq_ref[...]).astype(BF16)
    nh = MLA_HEADS * LANES
    r1 = _dot(cqn, wuq_ref[:, nh:nh + LANES])
    r2 = _dot(cqn, wuq_ref[:, nh + LANES:nh + 2 * LANES])
    o1 = r1 * cos - r2 * sin
    o2 = r1 * sin + r2 * cos
    head_of_lane = lax.broadcasted_iota(jnp.int32, (1, LANES), 1) // HALF_ROPE
    for hd in range(MLA_HEADS):
        nope = _dot(cqn, wuq_ref[:, hd * LANES:(hd + 1) * LANES]).astype(BF16)
        qext_ref[hd, :, 0:MLA_KV_RANK] = _dot(nope, wuk_ref[hd]).astype(BF16)
        sel = head_of_lane == hd
        qext_ref[hd, :, MLA_KV_RANK:MLA_KV_RANK + LANES] = jnp.where(sel, o1, 0.0).astype(BF16)
        qext_ref[hd, :, MLA_KV_RANK + LANES:] = jnp.where(sel, o2, 0.0).astype(BF16)
    latn = _rms(proj("ckv"), gkv_ref[...])
    kr1 = proj("kr1")
    kr2 = proj("kr2")
    k1 = kr1 * cos - kr2 * sin
    k2 = kr1 * sin + kr2 * cos
    rows_ref[:, 0:MLA_KV_RANK] = latn
    rows_ref[:, MLA_KV_RANK:MLA_KV_RANK + LANES] = k1
    rows_ref[:, MLA_KV_RANK + LANES:] = k2
    kext_ref[:, 0:MLA_KV_RANK] = latn.astype(BF16)
    kext_ref[:, MLA_KV_RANK:MLA_KV_RANK + LANES] = k1.astype(BF16)
    kext_ref[:, MLA_KV_RANK + LANES:] = k2.astype(BF16)
    gmla_ref[...] = proj("gmla")
    qn_ref[...] = (proj("qn") * (NSA_DH ** -0.5)).astype(BF16)
    cmp_ref[...] = proj("cmp")
    slc_ref[...] = proj("slc")
    slcp_ref[...] = proj("slcp").astype(BF16)
    win_ref[...] = proj("win")
    winp_ref[...] = proj("winp").astype(BF16)
    gl_ref[...] = proj("gl")
    gnsa_ref[...] = proj("gnsa")


def _even_project(x, gn, w_ext, gq, gkv, wuq_ext, wuk_ext, cos, sin, tm):
    t = x.shape[0]
    nt = t // tm
    ntab = cos.shape[0] // tm
    row = lambda w: pl.BlockSpec((tm, w), lambda i: (i, 0))
    tab = pl.BlockSpec((tm, LANES), lambda i: (i % ntab, 0))
    kext_w = MLA_KV_RANK + 2 * LANES
    out_shapes = (
        jax.ShapeDtypeStruct((MLA_HEADS, t, kext_w), BF16),
        jax.ShapeDtypeStruct((t, kext_w), F32),
        jax.ShapeDtypeStruct((t, kext_w), BF16),
        jax.ShapeDtypeStruct((t, MLA_HEADS * LANES), F32),
        jax.ShapeDtypeStruct((t, NSA_HEADS * LANES), BF16),
        jax.ShapeDtypeStruct((t, NSA_KV_W), F32),
        jax.ShapeDtypeStruct((t, NSA_KV_W), F32),
        jax.ShapeDtypeStruct((t, 4 * LANES), BF16),
        jax.ShapeDtypeStruct((t, NSA_KV_W), F32),
        jax.ShapeDtypeStruct((t, 4 * LANES), BF16),
        jax.ShapeDtypeStruct((t, LANES), F32),
        jax.ShapeDtypeStruct((t, NSA_HEADS * LANES), F32),
    )
    out_specs = (
        pl.BlockSpec((MLA_HEADS, tm, kext_w), lambda i: (0, i, 0)),
        row(kext_w), row(kext_w), row(MLA_HEADS * LANES), row(NSA_HEADS * LANES), row(NSA_KV_W),
        row(NSA_KV_W), row(4 * LANES), row(NSA_KV_W), row(4 * LANES), row(LANES), row(NSA_HEADS * LANES),
    )
    return pl.pallas_call(
        _even_proj_kernel,
        grid=(nt,),
        in_specs=[row(D_MODEL), _full_spec(gn), _full_spec(w_ext), _full_spec(gq), _full_spec(gkv),
                  _full_spec(wuq_ext), _full_spec(wuk_ext), tab, tab],
        out_specs=out_specs,
        out_shape=out_shapes,
        compiler_params=_cparams(("parallel",)),
        name="even_proj",
    )(x, gn, w_ext, gq, gkv, wuq_ext, wuk_ext, cos, sin)


LOG2E = float(np.log2(np.e))
MLA_HEADS_PER_CHUNK = 2


def _mla_attn_kernel(q_ref, k_ref, o_ref, m_sc, l_sc, acc_sc, *, tq, tk):
    qi = pl.program_id(1)
    ki = pl.program_id(2)
    n_needed = ((qi + 1) * tq + tk - 1) // tk
    m_rows = MLA_HEADS * tq

    @pl.when(ki == 0)
    def _():
        m_sc[...] = jnp.full(m_sc.shape, NEG_BIG, F32)
        l_sc[...] = jnp.zeros(l_sc.shape, F32)
        acc_sc[...] = jnp.zeros(acc_sc.shape, F32)

    def step(masked):
        rows_c = MLA_HEADS_PER_CHUNK * tq
        k = k_ref[...]
        v = k[:, :MLA_KV_RANK]
        c_exp = MLA_SCALE * LOG2E
        if masked:
            qpos = qi * tq + jnp.bitwise_and(lax.broadcasted_iota(jnp.int32, (rows_c, 1), 0), tq - 1)
            ok = (ki * tk + lax.broadcasted_iota(jnp.int32, (1, tk), 1)) <= qpos

        def chunk(c, carry):
            q = q_ref[pl.ds(c * MLA_HEADS_PER_CHUNK, MLA_HEADS_PER_CHUNK)].reshape(rows_c, q_ref.shape[-1])
            rs = pl.ds(pl.multiple_of(c * rows_c, rows_c), rows_c)
            s = _dot_nt(q, k)
            if masked:
                s = jnp.where(ok, s, NEG_BIG)
            m_prev = m_sc[rs, :]
            m_new = jnp.maximum(m_prev, _row_max(s))
            alpha = jnp.exp2((m_prev - m_new) * c_exp)
            p = jnp.exp2((s - m_new) * c_exp)
            l_sc[rs, :] = alpha * l_sc[rs, :] + _row_sum(p)
            acc_sc[rs, :] = alpha * acc_sc[rs, :] + _dot(p.astype(BF16), v)
            m_sc[rs, :] = m_new
            return carry

        lax.fori_loop(0, MLA_HEADS // MLA_HEADS_PER_CHUNK, chunk, 0)

    crosses = (ki + 1) * tk - 1 > qi * tq

    @pl.when(jnp.logical_and(ki < n_needed, crosses))
    def _():
        step(True)

    @pl.when(jnp.logical_and(ki < n_needed, jnp.logical_not(crosses)))
    def _():
        step(False)

    @pl.when(ki == n_needed - 1)
    def _():
        out = acc_sc[...] / jnp.maximum(l_sc[...], 1e-30)
        o_ref[...] = out.reshape(MLA_HEADS, tq, MLA_KV_RANK).astype(o_ref.dtype)


def _mla_prompt_attention(q_ext, k_ext, batch, seq, tq, tk):
    nq, nk = seq // tq, seq // tk
    kw = q_ext.shape[-1]

    def k_map(b, qi, ki):
        last = ((qi + 1) * tq + tk - 1) // tk - 1
        return (b * nk + jnp.minimum(ki, last), 0)

    return pl.pallas_call(
        functools.partial(_mla_attn_kernel, tq=tq, tk=tk),
        grid=(batch, nq, nk),
        in_specs=[pl.BlockSpec((MLA_HEADS, tq, kw), lambda b, qi, ki: (0, b * nq + qi, 0)),
                  pl.BlockSpec((tk, kw), k_map)],
        out_specs=pl.BlockSpec((MLA_HEADS, tq, MLA_KV_RANK), lambda b, qi, ki: (0, b * nq + qi, 0)),
        out_shape=jax.ShapeDtypeStruct((MLA_HEADS, batch * seq, MLA_KV_RANK), BF16),
        scratch_shapes=[pltpu.VMEM((MLA_HEADS * tq, 1), F32), pltpu.VMEM((MLA_HEADS * tq, 1), F32),
                        pltpu.VMEM((MLA_HEADS * tq, MLA_KV_RANK), F32)],
        compiler_params=_cparams(("parallel", "parallel", "arbitrary")),
        name="mla_prompt_attn",
    )(q_ext, k_ext)


SEG_W = CMP_STRIDE * NSA_KV_W
KVP_W = 4 * LANES


def _prep_cmp_weights(w_cmp, pe_cmp):
    wh = w_cmp.reshape(2, 2, CMP_STRIDE, NSA_DH, NSA_DH)
    big = jnp.zeros((2, CMP_STRIDE, NSA_KV_HEADS, 2, NSA_DH, 2, NSA_KV_HEADS, LANES), F32)
    for g in range(NSA_KV_HEADS):
        for c in range(2):
            big = big.at[:, :, g, c, :, c, g, :NSA_DH].set(wh[c])
    w_big = big.reshape(2, SEG_W, KVP_W).astype(BF16)
    peh = pe_cmp.reshape(2, 2, CMP_STRIDE, NSA_DH)
    pe_big = jnp.broadcast_to(jnp.transpose(peh, (1, 2, 0, 3))[:, :, None], (2, CMP_STRIDE, NSA_KV_HEADS, 2, NSA_DH))
    return w_big, pe_big.reshape(2, 1, SEG_W)


def _slc_overlap_matrix(n_cmp_rows, n_lanes):
    ratio = SLC_BLOCK // CMP_STRIDE
    lead = CMP_BLOCK // CMP_STRIDE - 1
    a = np.zeros((n_cmp_rows, n_lanes), np.float32)
    for j in range(n_lanes):
        for o in range(-lead, ratio):
            n = ratio * j + o
            if 0 <= n < n_cmp_rows:
                a[n, j] = (min(CMP_STRIDE * o + CMP_BLOCK, SLC_BLOCK) - max(CMP_STRIDE * o, 0)) / CMP_BLOCK
    return a


def _key_position_features(n_keys, n_lanes):
    kc = np.zeros((n_keys, LANES + n_lanes), np.float32)
    pos = np.arange(n_keys)
    kc[:, NSA_DH] = (pos // SLC_BLOCK) * SLC_BLOCK
    kc[:, NSA_DH + 1] = pos % SLC_BLOCK
    kc[pos, LANES + pos // SLC_BLOCK] = 1.0
    return kc


def _slopes_col(g, rows, tq):
    r = lax.shift_right_logical(lax.broadcasted_iota(jnp.int32, (rows, 1), 0), int(np.log2(tq)))
    col = jnp.zeros((rows, 1), F32)
    for rr in range(NSA_GROUP):
        col = jnp.where(r == rr, 2.0 ** (-(g * NSA_GROUP + rr + 1.0)), col)
    return col


def _softmax_rows(s, mask):
    s = jnp.where(mask, s, NEG_BIG)
    m = _row_max(s)
    e = jnp.where(mask, jnp.exp(s - m), 0.0)
    return e / jnp.maximum(_row_sum(e), 1e-30)


def _top_blocks(score, n_pick):
    lane = lax.broadcasted_iota(jnp.int32, score.shape, 1).astype(F32)
    sel = jnp.zeros(score.shape, F32)
    for _ in range(n_pick):
        m = jnp.max(score, axis=1, keepdims=True)
        idx = jnp.min(jnp.where(score == m, lane, float(score.shape[1])), axis=1, keepdims=True)
        hit = lane == idx
        sel = jnp.where(jnp.logical_and(hit, m > 0.5 * NEG_BIG), 1.0, sel)
        score = jnp.where(hit, 2.0 * NEG_BIG, score)
    return sel


def _split3(x):
    hi = x.astype(BF16)
    r1 = x - hi.astype(F32)
    mid = r1.astype(BF16)
    lo = (r1 - mid.astype(F32)).astype(BF16)
    return hi, mid, lo


def _cmp_prompt_kernel(x_ref, pe_ref, w_ref, kvc_ref):
    x = x_ref[...]
    h1 = _dot((x + pe_ref[0]).astype(BF16), w_ref[0])
    h2 = _dot((x + pe_ref[1]).astype(BF16), w_ref[1])
    n = x.shape[0]
    kvc_ref[...] = (h1 + pltpu.roll(h2, n - 1, 0)).astype(kvc_ref.dtype)


def _cmp_prompt(kv_cmp, w_big, pe_big, batch, seq):
    nseg = seq // CMP_STRIDE
    x = kv_cmp.reshape(batch * nseg, SEG_W)
    return pl.pallas_call(
        _cmp_prompt_kernel,
        grid=(batch,),
        in_specs=[pl.BlockSpec((nseg, SEG_W), lambda b: (b, 0)), _full_spec(pe_big), _full_spec(w_big)],
        out_specs=pl.BlockSpec((nseg, KVP_W), lambda b: (b, 0)),
        out_shape=jax.ShapeDtypeStruct((batch * nseg, KVP_W), BF16),
        compiler_params=_cparams(("parallel",)),
        name="nsa_cmp_prompt",
    )(x, pe_big, w_big)


N_WIN_BLOCKS = WINDOW // 128 + 1


def _nsa_prompt_kernel(qn_ref, gl_ref, kvc_ref, amat_ref, slcp_ref, kc_ref, *rest, tq, tk, nseg):
    win_refs = rest[:N_WIN_BLOCKS]
    o_ref = rest[N_WIN_BLOCKS]
    m_sc, l_sc, acc_sc, flag_ref = rest[N_WIN_BLOCKS + 1:]
    nt_all = slcp_ref.shape[0] // tk
    qi = pl.program_id(1)
    s0 = qi * tq
    rows = NSA_GROUP * tq
    nl = amat_ref.shape[1]
    qpos = s0 + jnp.bitwise_and(lax.broadcasted_iota(jnp.int32, (rows, 1), 0), tq - 1)
    qpos_t = s0 + lax.broadcasted_iota(jnp.int32, (tq, 1), 0)
    gates = jax.nn.sigmoid(gl_ref[...])
    n_tiles = (s0 + tq + tk - 1) // tk

    for g in range(NSA_KV_HEADS):
        slope = _slopes_col(g, rows, tq)
        q_heads = [qn_ref[:, (g * NSA_GROUP + r) * LANES:(g * NSA_GROUP + r + 1) * LANES] for r in range(NSA_GROUP)]
        qg = jnp.concatenate(q_heads, axis=0)
        kc = kvc_ref[:, g * LANES:(g + 1) * LANES]
        vc = kvc_ref[:, (2 + g) * LANES:(3 + g) * LANES]
        cpos = lax.broadcasted_iota(jnp.int32, (1, nseg), 1) * CMP_STRIDE + (CMP_BLOCK - 1)
        dist = (qpos - cpos).astype(F32)
        p_c = _softmax_rows(_dot_nt(qg, kc) - slope * dist, dist >= 0)
        o_c = _dot(p_c.astype(BF16), vc)
        p_grp = p_c[0:tq]
        for r in range(1, NSA_GROUP):
            p_grp = p_grp + p_c[r * tq:(r + 1) * tq]
        amat = amat_ref[...]
        s_slc = sum(_dot(t, amat) for t in _split3(p_grp))
        blk = lax.broadcasted_iota(jnp.int32, (1, nl), 1)
        cur = lax.shift_right_logical(qpos_t, int(np.log2(SLC_BLOCK)))
        valid = blk * SLC_BLOCK <= qpos_t
        forced = jnp.logical_or(blk == 0, jnp.logical_or(blk == cur, blk == cur - 1))
        score = jnp.where(forced, FORCE_SCORE, jnp.where(valid, s_slc, NEG_BIG))
        sel = _top_blocks(score, N_SELECT)
        selneg = ((sel - 1.0) * MASK_BIG).astype(BF16)
        any_sel = jnp.max(sel, axis=0, keepdims=True)
        tile_of_blk = lax.shift_right_logical(blk, int(np.log2(tk // SLC_BLOCK)))
        for t in range(nt_all):
            flag_ref[t] = jnp.max(jnp.where(tile_of_blk == t, any_sel, 0.0)).astype(jnp.int32)
        lane = lax.broadcasted_iota(jnp.int32, (1, LANES), 1)
        is_pos_lane = jnp.logical_or(lane == NSA_DH, lane == NSA_DH + 1)
        q_aug = jnp.concatenate(
            [jnp.concatenate([jnp.where(is_pos_lane, (2.0 ** (-(g * NSA_GROUP + r + 1.0))), q_heads[r].astype(F32)).astype(BF16),
                              selneg], axis=1) for r in range(NSA_GROUP)], axis=0)
        m_sc[...] = jnp.full(m_sc.shape, NEG_BIG, F32)
        l_sc[...] = jnp.zeros(l_sc.shape, F32)
        acc_sc[...] = jnp.zeros(acc_sc.shape, F32)

        def slc_tile(kt, causal):
            k0 = pl.multiple_of(kt * tk, tk)
            kk = slcp_ref[pl.ds(k0, tk), g * LANES:(g + 1) * LANES] + kc_ref[pl.ds(k0, tk), 0:LANES]
            k_aug = jnp.concatenate([kk, kc_ref[pl.ds(k0, tk), LANES:]], axis=1)
            vv = slcp_ref[pl.ds(k0, tk), (2 + g) * LANES:(3 + g) * LANES]
            s = _dot_nt(q_aug, k_aug)
            if causal:
                ok = (k0 + lax.broadcasted_iota(jnp.int32, (1, tk), 1)) <= qpos
                s = jnp.where(ok, s, NEG_BIG)
            m_prev = m_sc[...]
            m_new = jnp.maximum(m_prev, _row_max(s))
            alpha = jnp.exp(m_prev - m_new)
            p = jnp.exp(s - m_new)
            if causal:
                p = jnp.where(ok, p, 0.0)
            l_sc[...] = alpha * l_sc[...] + _row_sum(p)
            acc_sc[...] = alpha * acc_sc[...] + _dot(p.astype(BF16), vv)
            m_sc[...] = m_new

        def body(kt, carry):
            @pl.when(flag_ref[kt] > 0)
            def _():
                slc_tile(kt, False)
            return carry

        lax.fori_loop(0, n_tiles - 1, body, 0)
        slc_tile(n_tiles - 1, True)
        o_s = acc_sc[...] / jnp.maximum(l_sc[...], 1e-30)
        kw = jnp.concatenate([w[:, g * LANES:(g + 1) * LANES] for w in win_refs], axis=0)
        vw = jnp.concatenate([w[:, (2 + g) * LANES:(3 + g) * LANES] for w in win_refs], axis=0)
        nw = N_WIN_BLOCKS * 128
        wpos = s0 - WINDOW + lax.broadcasted_iota(jnp.int32, (1, nw), 1)
        dist_w = (qpos - wpos).astype(F32)
        mask_w = jnp.logical_and(jnp.logical_and(dist_w >= 0, dist_w <= WINDOW), wpos >= 0)
        p_w = _softmax_rows(_dot_nt(qg, kw) - slope * dist_w, mask_w)
        o_w = _dot(p_w.astype(BF16), vw)
        for r in range(NSA_GROUP):
            hd = g * NSA_GROUP + r
            sl = slice(r * tq, (r + 1) * tq)
            o_ref[:, hd * LANES:(hd + 1) * LANES] = (
                gates[:, 3 * hd:3 * hd + 1] * o_c[sl] + gates[:, 3 * hd + 1:3 * hd + 2] * o_s[sl]
                + gates[:, 3 * hd + 2:3 * hd + 3] * o_w[sl])


def _nsa_prompt(qn, gl, kvc, slcp, winp, batch, seq, tq, tk):
    assert tq == 128 and seq % tk == 0 and tk % tq == 0
    nseg = seq // CMP_STRIDE
    n_slc = seq // SLC_BLOCK
    nl = LANES * ((n_slc + LANES - 1) // LANES)
    assert nl == LANES, "selection blocks must fit one lane group"
    nq = seq // tq
    amat = jnp.asarray(_slc_overlap_matrix(nseg, nl), BF16)
    kc = jnp.asarray(_key_position_features(seq, nl), BF16)

    def win_spec(j):
        return pl.BlockSpec((128, KVP_W), lambda b, qi: (b * nq + jnp.maximum(qi - (N_WIN_BLOCKS - 1) + j, 0), 0))

    rows = NSA_GROUP * tq
    return pl.pallas_call(
        functools.partial(_nsa_prompt_kernel, tq=tq, tk=tk, nseg=nseg),
        grid=(batch, nq),
        in_specs=[pl.BlockSpec((tq, NSA_HEADS * LANES), lambda b, qi: (b * nq + qi, 0)),
                  pl.BlockSpec((tq, LANES), lambda b, qi: (b * nq + qi, 0)),
                  pl.BlockSpec((nseg, KVP_W), lambda b, qi: (b, 0)),
                  _full_spec(amat),
                  pl.BlockSpec((seq, KVP_W), lambda b, qi: (b, 0)),
                  _full_spec(kc)] + [win_spec(j) for j in range(N_WIN_BLOCKS)],
        out_specs=pl.BlockSpec((tq, NSA_HEADS * LANES), lambda b, qi: (b * nq + qi, 0)),
        out_shape=jax.ShapeDtypeStruct((batch * seq, NSA_HEADS * LANES), F32),
        scratch_shapes=[pltpu.VMEM((rows, 1), F32), pltpu.VMEM((rows, 1), F32), pltpu.VMEM((rows, LANES), F32),
                        pltpu.SMEM((seq // tk,), jnp.int32)],
        compiler_params=_cparams(("parallel", "arbitrary")),
        name="nsa_prompt",
    )(qn, gl, kvc, amat, slcp, kc, *([winp] * N_WIN_BLOCKS))


def _silu(x):
    return x * jax.nn.sigmoid(x)


def _even_out_kernel(x_ref, lat_ref, gmla_ref, onsa_ref, gnsa_ref, wuv_ref, wo_ref, o_ref):
    o_mla = jnp.concatenate([_dot(lat_ref[hd], wuv_ref[hd]) for hd in range(MLA_HEADS)], axis=1)
    a = (o_mla * _silu(gmla_ref[...])).astype(BF16)
    b = (onsa_ref[...] * _silu(gnsa_ref[...])).astype(BF16)
    nm = MLA_HEADS * LANES
    o_ref[...] = x_ref[...] + _dot(a, wo_ref[0:nm]) + _dot(b, wo_ref[nm:])


def _even_output(x, lat, gmla, onsa, gnsa, wuv_ext, wo_ext, tm):
    t = x.shape[0]
    row = lambda w: pl.BlockSpec((tm, w), lambda i: (i, 0))
    return pl.pallas_call(
        _even_out_kernel,
        grid=(t // tm,),
        in_specs=[row(D_MODEL), pl.BlockSpec((MLA_HEADS, tm, MLA_KV_RANK), lambda i: (0, i, 0)),
                  row(MLA_HEADS * LANES), row(NSA_HEADS * LANES), row(NSA_HEADS * LANES),
                  _full_spec(wuv_ext), _full_spec(wo_ext)],
        out_specs=row(D_MODEL),
        out_shape=jax.ShapeDtypeStruct((t, D_MODEL), F32),
        compiler_params=_cparams(("parallel",)),
        name="even_out",
    )(x, lat, gmla, onsa, gnsa, wuv_ext, wo_ext)


_RET_QK = RET_HEADS * RET_DK
_RET_LOG_G = [float(np.log1p(-(2.0 ** (-5.0 - h)))) for h in range(RET_HEADS)]


def _rope_tables_ret(pos):
    half = RET_DK // 2
    inv = ROPE_BASE ** (-jnp.arange(half, dtype=F32) / half)
    ang = pos.astype(F32)[:, None] * inv
    cos, sin = jnp.cos(ang), jnp.sin(ang)
    return jnp.concatenate([cos, cos], 1), jnp.concatenate([-sin, sin], 1)


def _odd_proj_kernel(x_ref, gn_ref, w_ref, c_ref, s_ref, q_ref, k_ref, v_ref, g_ref):
    h = _rms(x_ref[...], gn_ref[...]).astype(BF16)
    c = c_ref[...]
    s = s_ref[...]
    half = RET_DK // 2

    def rot(z):
        return z * c + pltpu.roll(z, half, 1) * s

    for hd in range(RET_HEADS):
        sl = slice(hd * RET_DK, (hd + 1) * RET_DK)
        q_ref[:, sl] = rot(_dot(h, w_ref[:, sl])).astype(BF16)
        ks = slice(_RET_QK + hd * RET_DK, _RET_QK + (hd + 1) * RET_DK)
        k_ref[:, sl] = (rot(_dot(h, w_ref[:, ks])) * (RET_DK ** -0.5)).astype(BF16)
    v_ref[...] = _dot(h, w_ref[:, 2 * _RET_QK:2 * _RET_QK + RET_WIDTH])
    g_ref[...] = _dot(h, w_ref[:, 2 * _RET_QK + RET_WIDTH:])


def _odd_project(x, gn, w_bf, cos, sin, tm):
    t = x.shape[0]
    ntab = cos.shape[0] // tm
    row = lambda w: pl.BlockSpec((tm, w), lambda i: (i, 0))
    tab = pl.BlockSpec((tm, RET_DK), lambda i: (i % ntab, 0))
    return pl.pallas_call(
        _odd_proj_kernel,
        grid=(t // tm,),
        in_specs=[row(D_MODEL), _full_spec(gn), _full_spec(w_bf), tab, tab],
        out_specs=(row(_RET_QK), row(_RET_QK), row(RET_WIDTH), row(RET_WIDTH)),
        out_shape=(jax.ShapeDtypeStruct((t, _RET_QK), BF16), jax.ShapeDtypeStruct((t, _RET_QK), BF16),
                   jax.ShapeDtypeStruct((t, RET_WIDTH), F32), jax.ShapeDtypeStruct((t, RET_WIDTH), F32)),
        compiler_params=_cparams(("parallel",)),
        name="odd_proj",
    )(x, gn, w_bf, cos, sin)


def _group_norm_gate(o, gn_row, g):
    mu = jnp.mean(o, axis=-1, keepdims=True)
    var = jnp.mean(jnp.square(o - mu), axis=-1, keepdims=True)
    return _silu(g) * ((o - mu) * lax.rsqrt(var + EPS) * gn_row)


def _ret_prompt_kernel(q_ref, k_ref, v_ref, g_ref, gn_ref, y_ref, sfin_ref, s_sc, *, chunk):
    ci = pl.program_id(1)

    @pl.when(ci == 0)
    def _():
        s_sc[...] = jnp.zeros(s_sc.shape, F32)

    n_col = lax.broadcasted_iota(jnp.int32, (chunk, 1), 0).astype(F32)
    diff = n_col - lax.broadcasted_iota(jnp.int32, (1, chunk), 1).astype(F32)
    for hd in range(RET_HEADS):
        lg = _RET_LOG_G[hd]
        dmat = jnp.where(diff >= 0, jnp.exp(lg * jnp.maximum(diff, 0.0)), 0.0)
        xi = jnp.exp(lg * (n_col + 1.0))
        zeta = jnp.exp(lg * (chunk - 1.0 - n_col))
        qh = q_ref[:, hd * RET_DK:(hd + 1) * RET_DK]
        kh = k_ref[:, hd * RET_DK:(hd + 1) * RET_DK]
        vh = v_ref[:, hd * RET_DV:(hd + 1) * RET_DV]
        s_prev = s_sc[hd]
        inner = _dot_nt(qh, kh) * dmat
        o = _dot(inner.astype(BF16), vh.astype(BF16)) + _dot(qh, s_prev.astype(BF16)) * xi
        kv = lax.dot_general(kh, (vh * zeta).astype(BF16), (((0,), (0,)), ((), ())), preferred_element_type=F32)
        s_sc[hd] = float(np.exp(lg * chunk)) * s_prev + kv
        y_ref[:, hd * RET_DV:(hd + 1) * RET_DV] = _group_norm_gate(
            o, gn_ref[hd:hd + 1, :], g_ref[:, hd * RET_DV:(hd + 1) * RET_DV]).astype(y_ref.dtype)

    @pl.when(ci == pl.num_programs(1) - 1)
    def _():
        sfin_ref[0] = s_sc[...]


def _retention_prompt(q, k, v, g, gn, batch, seq):
    chunk = RET_CHUNK
    nc = seq // chunk
    row = lambda w: pl.BlockSpec((chunk, w), lambda b, c: (b * nc + c, 0))
    return pl.pallas_call(
        functools.partial(_ret_prompt_kernel, chunk=chunk),
        grid=(batch, nc),
        in_specs=[row(_RET_QK), row(_RET_QK), row(RET_WIDTH), row(RET_WIDTH), _full_spec(gn)],
        out_specs=(row(RET_WIDTH),
                   pl.BlockSpec((1, RET_HEADS, RET_DK, RET_DV), lambda b, c: (b, 0, 0, 0))),
        out_shape=(jax.ShapeDtypeStruct((batch * seq, RET_WIDTH), BF16),
                   jax.ShapeDtypeStruct((batch, RET_HEADS, RET_DK, RET_DV), F32)),
        scratch_shapes=[pltpu.VMEM((RET_HEADS, RET_DK, RET_DV), F32)],
        compiler_params=_cparams(("parallel", "arbitrary")),
        name="retention_prompt",
    )(q, k, v, g, gn)


def _odd_out_kernel(x_ref, y_ref, w_ref, fn_ref, o_ref):
    x2 = x_ref[...] + _dot(y_ref[...], w_ref[...])
    o_ref[...] = _rms(x2, fn_ref[...])


def _odd_output(x, y, w_bf, fn, tm):
    t = x.shape[0]
    row = lambda w: pl.BlockSpec((tm, w), lambda i: (i, 0))
    return pl.pallas_call(
        _odd_out_kernel,
        grid=(t // tm,),
        in_specs=[row(D_MODEL), row(RET_WIDTH), _full_spec(w_bf), _full_spec(fn)],
        out_specs=row(D_MODEL),
        out_shape=jax.ShapeDtypeStruct((t, D_MODEL), F32),
        compiler_params=_cparams(("parallel",)),
        name="odd_out",
    )(x, y, w_bf, fn)


def _ret_decode_kernel(q_ref, k_ref, v_ref, g_ref, gn_ref, s_ref, y_ref, snew_ref):
    eye = (lax.broadcasted_iota(jnp.int32, (RET_DK, RET_DK), 0)
           == lax.broadcasted_iota(jnp.int32, (RET_DK, RET_DK), 1))
    for hd in range(RET_HEADS):
        gam = float(np.exp(_RET_LOG_G[hd]))
        qh = q_ref[0, :, hd * RET_DK:(hd + 1) * RET_DK]
        kh = k_ref[0, :, hd * RET_DK:(hd + 1) * RET_DK]
        vh = v_ref[0, :, hd * RET_DV:(hd + 1) * RET_DV].astype(BF16).astype(F32)
        s_prev = s_ref[0, hd]
        inner = jnp.sum(qh.astype(F32) * kh.astype(F32), axis=1, keepdims=True).astype(BF16).astype(F32)
        qs = _dot(jnp.broadcast_to(qh, (8, RET_DK)), s_prev.astype(BF16))[0:1]
        o = inner * vh + qs * gam
        k_col = jnp.sum(jnp.where(eye, jnp.broadcast_to(kh.astype(F32), (RET_DK, RET_DK)), 0.0), axis=1, keepdims=True)
        snew_ref[0, hd] = gam * s_prev + k_col * vh
        y_ref[0, :, hd * RET_DV:(hd + 1) * RET_DV] = _group_norm_gate(
            o, gn_ref[hd:hd + 1, :], g_ref[0, :, hd * RET_DV:(hd + 1) * RET_DV]).astype(y_ref.dtype)


def _retention_decode(q, k, v, g, gn, state):
    db = q.shape[0]
    r3 = lambda a: a.reshape(db, 1, a.shape[-1])
    vec = lambda w: pl.BlockSpec((1, 1, w), lambda b: (b, 0, 0))
    st = pl.BlockSpec((1, RET_HEADS, RET_DK, RET_DV), lambda b: (b, 0, 0, 0))
    y, s_new = pl.pallas_call(
        _ret_decode_kernel,
        grid=(db,),
        in_specs=[vec(_RET_QK), vec(_RET_QK), vec(RET_WIDTH), vec(RET_WIDTH), _full_spec(gn), st],
        out_specs=(vec(RET_WIDTH), st),
        out_shape=(jax.ShapeDtypeStruct((db, 1, RET_WIDTH), BF16),
                   jax.ShapeDtypeStruct(state.shape, state.dtype)),
        compiler_params=_cparams(("parallel",)),
        name="retention_decode",
    )(r3(q), r3(k), r3(v), r3(g), gn, state)
    return y.reshape(db, RET_WIDTH), s_new


PAGES_PER_STEP = 16
SUB = 8


def _mla_decode_kernel(pt_ref, q_ref, new_ref, *rest):
    del pt_ref
    page_refs = rest[:PAGES_PER_STEP]
    o_ref, m_sc, l_sc, acc_sc = rest[PAGES_PER_STEP:]
    i = pl.program_id(1)

    @pl.when(i == 0)
    def _():
        m_sc[...] = jnp.full(m_sc.shape, NEG_BIG, F32)
        l_sc[...] = jnp.zeros(l_sc.shape, F32)
        acc_sc[...] = jnp.zeros(acc_sc.shape, F32)

    q = q_ref[0]
    rows_t = jnp.concatenate([p[...] for p in page_refs], axis=1).astype(BF16)
    s = _dot(q, rows_t) * MLA_SCALE
    m_prev = m_sc[...]
    m_new = jnp.maximum(m_prev, _row_max(s))
    alpha = jnp.exp(m_prev - m_new)
    p = jnp.exp(s - m_new)
    l_sc[...] = alpha * l_sc[...] + _row_sum(p)
    acc_sc[...] = alpha * acc_sc[...] + _dot_nt(p.astype(BF16), rows_t[:MLA_KV_RANK, :])
    m_sc[...] = m_new

    @pl.when(i == pl.num_programs(1) - 1)
    def _():
        new = new_ref[0].astype(BF16).astype(F32)
        s_n = jnp.sum(q.astype(F32) * new, axis=1, keepdims=True) * MLA_SCALE
        m_prev = m_sc[...]
        m_new = jnp.maximum(m_prev, s_n)
        alpha = jnp.exp(m_prev - m_new)
        p_n = jnp.exp(s_n - m_new)
        l = alpha * l_sc[...] + p_n
        acc = alpha * acc_sc[...] + p_n.astype(BF16).astype(F32) * new[:, :MLA_KV_RANK]
        o_ref[0] = (acc / jnp.maximum(l, 1e-30)).astype(o_ref.dtype)


def _mla_decode(q_dec, rows_new, pool, page_table):
    db, n_pages = page_table.shape
    assert n_pages % PAGES_PER_STEP == 0
    steps = n_pages // PAGES_PER_STEP
    w = pool.shape[1]

    def page_spec(j):
        return pl.BlockSpec((None, w, PAGE_SIZE),
                            lambda b, i, pt: (pt[b * n_pages + i * PAGES_PER_STEP + j], 0, 0))

    grid_spec = pltpu.PrefetchScalarGridSpec(
        num_scalar_prefetch=1,
        grid=(db, steps),
        in_specs=[pl.BlockSpec((1, MLA_HEADS, w), lambda b, i, pt: (b, 0, 0)),
                  pl.BlockSpec((1, 1, w), lambda b, i, pt: (b, 0, 0))]
                 + [page_spec(j) for j in range(PAGES_PER_STEP)],
        out_specs=pl.BlockSpec((1, MLA_HEADS, MLA_KV_RANK), lambda b, i, pt: (b, 0, 0)),
        scratch_shapes=[pltpu.VMEM((MLA_HEADS, 1), F32), pltpu.VMEM((MLA_HEADS, 1), F32),
                        pltpu.VMEM((MLA_HEADS, MLA_KV_RANK), F32)],
    )
    return pl.pallas_call(
        _mla_decode_kernel,
        grid_spec=grid_spec,
        out_shape=jax.ShapeDtypeStruct((db, MLA_HEADS, MLA_KV_RANK), BF16),
        compiler_params=_cparams(("parallel", "arbitrary")),
        name="mla_decode",
    )(page_table.reshape(-1), q_dec, rows_new.reshape(db, 1, w), *([pool] * PAGES_PER_STEP))


def _decode_blocks(p_len, n_new):
    n_slc = -(-(p_len + n_new) // SLC_BLOCK)
    return n_slc, LANES * ((n_slc + LANES - 1) // LANES)


CMP_PAIRS = CMP_STRIDE // 2
GRP_W = 2 * NSA_DH


def _prep_cmp_pair_weights(w_cmp, pe_cmp):
    wh = w_cmp.reshape(2, 2, CMP_PAIRS, 2, NSA_DH, NSA_DH)
    big = jnp.zeros((CMP_PAIRS, 2, 2, NSA_DH, 2, 2, NSA_DH), F32)
    for c in range(2):
        big = big.at[:, :, c, :, :, c, :].set(jnp.transpose(wh[c], (1, 2, 3, 0, 4)))
    w_pair = big.reshape(CMP_PAIRS, 2 * GRP_W, 2 * GRP_W).astype(BF16)
    peh = pe_cmp.reshape(2, 2, CMP_PAIRS, 2, NSA_DH)
    pe_pair = jnp.transpose(peh, (1, 2, 3, 0, 4)).reshape(2, CMP_PAIRS, 1, 2 * GRP_W)
    return w_pair, pe_pair


def _cmp_bias_kernel(pe_ref, w_ref, b_ref):
    for half in range(2):
        acc = jnp.zeros((SUB, GRP_W), F32)
        for jp in range(CMP_PAIRS):
            w = w_ref[jp][:, half * GRP_W:(half + 1) * GRP_W]
            for t in _split3(jnp.broadcast_to(pe_ref[half, jp], (SUB, 2 * GRP_W))):
                acc = acc + _dot(t, w)
        b_ref[half] = acc


def _cmp_bias(w_pair, pe_pair):
    return pl.pallas_call(
        _cmp_bias_kernel,
        grid=(1,),
        in_specs=[_full_spec(pe_pair), _full_spec(w_pair)],
        out_specs=pl.BlockSpec((2, SUB, GRP_W), lambda i: (0, 0, 0)),
        out_shape=jax.ShapeDtypeStruct((2, SUB, GRP_W), F32),
        compiler_params=_cparams(("arbitrary",)),
        name="nsa_cmp_bias",
    )(pe_pair, w_pair)


def _cmp_decode_kernel(pt_ref, qn_ref, cnew_ref, bias_ref, w_ref, amat_ref, *rest, p_len):
    del pt_ref
    page_refs = rest[:PAGES_PER_STEP]
    oc_ref, sslc_ref, x_sc, h_sc = rest[PAGES_PER_STEP:]
    i = pl.program_id(1)
    n_tok = PAGES_PER_STEP * PAGE_SIZE
    n_step = n_tok // CMP_STRIDE
    nseg = h_sc.shape[1]
    eye = (lax.broadcasted_iota(jnp.int32, (PAGE_SIZE, PAGE_SIZE), 0)
           == lax.broadcasted_iota(jnp.int32, (PAGE_SIZE, PAGE_SIZE), 1)).astype(BF16)
    for j, p in enumerate(page_refs):
        for g in range(NSA_KV_HEADS):
            x_sc[g, j * PAGE_SIZE:(j + 1) * PAGE_SIZE, :] = _dot_nt(
                eye, p[g * GRP_W:(g + 1) * GRP_W, :].astype(BF16))
    r0 = pl.multiple_of(i * n_step, n_step)
    for g in range(NSA_KV_HEADS):
        acc = jnp.zeros((n_step, 2 * GRP_W), F32)
        for jp in range(CMP_PAIRS):
            rows = [x_sc[g, pl.ds(2 * jp + jj, n_step, stride=CMP_STRIDE), :] for jj in range(2)]
            acc = acc + _dot(jnp.concatenate(rows, axis=1).astype(BF16), w_ref[jp])
        h_sc[g, pl.ds(r0, n_step), :] = acc

    @pl.when(i == pl.num_programs(1) - 1)
    def _():
        qpos = p_len
        cpos = lax.broadcasted_iota(jnp.int32, (1, nseg), 1) * CMP_STRIDE + (CMP_BLOCK - 1)
        dist = (qpos - cpos).astype(F32)
        row = lax.broadcasted_iota(jnp.int32, (SUB, 1), 0)
        seg = lax.broadcasted_iota(jnp.int32, (nseg, 1), 0)
        amat = amat_ref[...]
        b1 = bias_ref[0, 0:1, :]
        b2 = bias_ref[1, 0:1, :]
        for g in range(NSA_KV_HEADS):
            h = h_sc[g]
            new_seg = jnp.concatenate([cnew_ref[0, :, g * GRP_W:(g + 1) * GRP_W], jnp.zeros((1, GRP_W), F32)], axis=1)
            h2n = _dot(jnp.broadcast_to(new_seg, (SUB, 2 * GRP_W)).astype(BF16), w_ref[0])[0:1, GRP_W:] + b2
            h2 = jnp.where(seg == nseg - 1, h2n, pltpu.roll(h[:, GRP_W:], nseg - 1, 0) + b2)
            kvc = (h[:, :GRP_W] + b1 + h2).astype(BF16)
            slope = jnp.zeros((SUB, 1), F32)
            for rr in range(NSA_GROUP):
                slope = jnp.where(row == rr, 2.0 ** (-(g * NSA_GROUP + rr + 1.0)), slope)
            qg = qn_ref[0, g]
            p_c = _softmax_rows(_dot_nt(qg, kvc) - slope * dist, dist >= 0)
            p_c = jnp.where(row < NSA_GROUP, p_c, 0.0)
            oc_ref[0, g] = pltpu.roll(_dot(p_c.astype(BF16), kvc), LANES - NSA_DH, 1)
            s_rows = sum(_dot(t, amat) for t in _split3(p_c))
            sslc_ref[0, g] = jnp.broadcast_to(jnp.sum(s_rows, axis=0, keepdims=True), s_rows.shape)


def _cmp_decode(qn_dec, cmp_new, pool_t, page_table, w_pair, bias):
    db, n_pages = page_table.shape
    steps = n_pages // PAGES_PER_STEP
    p_len = n_pages * PAGE_SIZE
    nseg = p_len // CMP_STRIDE
    n_slc, nl = _decode_blocks(p_len, 1)
    amat = jnp.asarray(_slc_overlap_matrix(nseg, nl), BF16)

    def page_spec(j):
        return pl.BlockSpec((None, NSA_KV_W, PAGE_SIZE),
                            lambda b, i, pt: (pt[b * n_pages + i * PAGES_PER_STEP + j], 0, 0))

    grid_spec = pltpu.PrefetchScalarGridSpec(
        num_scalar_prefetch=1,
        grid=(db, steps),
        in_specs=[pl.BlockSpec((1, NSA_KV_HEADS, SUB, LANES), lambda b, i, pt: (b, 0, 0, 0)),
                  pl.BlockSpec((1, 1, NSA_KV_W), lambda b, i, pt: (b, 0, 0)),
                  pl.BlockSpec(bias.shape, lambda b, i, pt: (0, 0, 0)),
                  pl.BlockSpec(w_pair.shape, lambda b, i, pt: (0, 0, 0)),
                  pl.BlockSpec(amat.shape, lambda b, i, pt: (0, 0))]
                 + [page_spec(j) for j in range(PAGES_PER_STEP)],
        out_specs=(pl.BlockSpec((1, NSA_KV_HEADS, SUB, LANES), lambda b, i, pt: (b, 0, 0, 0)),
                   pl.BlockSpec((1, NSA_KV_HEADS, SUB, nl), lambda b, i, pt: (b, 0, 0, 0))),
        scratch_shapes=[pltpu.VMEM((NSA_KV_HEADS, PAGES_PER_STEP * PAGE_SIZE, GRP_W), F32),
                        pltpu.VMEM((NSA_KV_HEADS, nseg, 2 * GRP_W), F32)],
    )
    return pl.pallas_call(
        functools.partial(_cmp_decode_kernel, p_len=p_len),
        grid_spec=grid_spec,
        out_shape=(jax.ShapeDtypeStruct((db, NSA_KV_HEADS, SUB, LANES), F32),
                   jax.ShapeDtypeStruct((db, NSA_KV_HEADS, SUB, nl), F32)),
        compiler_params=_cparams(("parallel", "arbitrary")),
        name="nsa_cmp_decode",
    )(page_table.reshape(-1), qn_dec, cmp_new.reshape(db, 1, NSA_KV_W), bias, w_pair, amat,
      *([pool_t] * PAGES_PER_STEP))


def _topk_decode_kernel(s_ref, idx_ref, *, qpos):
    s_slc = s_ref[...]
    blk = lax.broadcasted_iota(jnp.int32, s_slc.shape, 1)
    cur = qpos // SLC_BLOCK
    valid = blk * SLC_BLOCK <= qpos
    forced = jnp.logical_or(blk == 0, jnp.logical_or(blk == cur, blk == cur - 1))
    score = jnp.where(forced, FORCE_SCORE, jnp.where(valid, s_slc, NEG_BIG))
    lane = lax.broadcasted_iota(jnp.int32, idx_ref.shape, 1)
    out = jnp.full(idx_ref.shape, -1, jnp.int32)
    for t in range(N_SELECT):
        m = jnp.max(score, axis=1, keepdims=True)
        idx = jnp.min(jnp.where(score == m, blk, score.shape[1]), axis=1, keepdims=True)
        out = jnp.where(lane == t, jnp.where(m > 0.5 * NEG_BIG, idx, -1), out)
        score = jnp.where(blk == idx, 2.0 * NEG_BIG, score)
    idx_ref[...] = out


def _topk_decode(s_slc, qpos):
    rows = s_slc.shape[0]
    return pl.pallas_call(
        functools.partial(_topk_decode_kernel, qpos=qpos),
        grid=(1,),
        in_specs=[_full_spec(s_slc)],
        out_specs=pl.BlockSpec((rows, LANES), lambda i: (0, 0)),
        out_shape=jax.ShapeDtypeStruct((rows, LANES), jnp.int32),
        compiler_params=_cparams(("arbitrary",)),
        name="nsa_topk_decode",
    )(s_slc)


def _sel_decode_kernel(pt_ref, sel_ref, qn_ref, gl_ref, oc_ref, snew_ref, wnew_ref, wnewt_ref, win_ref, *rest, p_len):
    del pt_ref
    nb = NSA_KV_HEADS * N_SELECT
    blk_refs = rest[:nb]
    o_ref, nwin_ref = rest[nb:]
    b = pl.program_id(0)
    qpos = p_len
    n_past_blocks = p_len // SLC_BLOCK
    per_page = PAGE_SIZE // SLC_BLOCK
    row = lax.broadcasted_iota(jnp.int32, (SUB, 1), 0)
    lane = lax.broadcasted_iota(jnp.int32, (SUB, LANES), 1)
    gates = jnp.broadcast_to(jax.nn.sigmoid(gl_ref[0]), (SUB, LANES))
    nw = win_ref.shape[2]
    win = win_ref[0]
    wnew = wnew_ref[0]
    snew = snew_ref[0]

    def merge_new(s, mask, kv_t, q8, new_row, slope, use_new):
        new_f = new_row.astype(BF16).astype(F32)
        s_n = jnp.sum(q8.astype(F32) * new_f, axis=1, keepdims=True)
        s = jnp.where(mask, s, NEG_BIG)
        m = jnp.maximum(_row_max(s), jnp.where(use_new, s_n, NEG_BIG))
        e = jnp.where(mask, jnp.exp(s - m), 0.0)
        e_n = jnp.where(use_new, jnp.exp(s_n - m), 0.0)
        denom = jnp.maximum(_row_sum(e) + e_n, 1e-30)
        p = e / denom
        p_n = e_n / denom
        o = _dot_nt(p.astype(BF16), kv_t) + p_n.astype(BF16).astype(F32) * new_f
        return pltpu.roll(o, LANES - NSA_DH, 1)

    for g in range(NSA_KV_HEADS):
        slope = jnp.zeros((SUB, 1), F32)
        for rr in range(NSA_GROUP):
            slope = jnp.where(row == rr, 2.0 ** (-(g * NSA_GROUP + rr + 1.0)), slope)
        q8 = qn_ref[0, g]
        kv_t = jnp.concatenate([r[...] for r in blk_refs[g * N_SELECT:(g + 1) * N_SELECT]], axis=1).astype(BF16)
        nk = N_SELECT * PAGE_SIZE
        key = lax.broadcasted_iota(jnp.int32, (1, nk), 1)
        slot = lax.shift_right_logical(key, int(np.log2(PAGE_SIZE)))
        blk_in_page = jnp.bitwise_and(lax.shift_right_logical(key, int(np.log2(SLC_BLOCK))), per_page - 1)
        off = jnp.bitwise_and(key, SLC_BLOCK - 1)
        kblk = jnp.zeros((1, nk), jnp.int32)
        use_new = False
        for t in range(N_SELECT):
            st = sel_ref[(b * NSA_KV_HEADS + g) * N_SELECT + t]
            kblk = jnp.where(slot == t, st, kblk)
            use_new = jnp.logical_or(use_new, st == n_past_blocks)
        kpos = kblk * SLC_BLOCK + off
        ok = jnp.logical_and(jnp.logical_and(kblk >= 0, kblk < n_past_blocks),
                             jnp.logical_and(blk_in_page == jnp.bitwise_and(kblk, per_page - 1), kpos <= qpos))
        dist = (qpos - kpos).astype(F32)
        o_s = merge_new(_dot(q8, kv_t) - slope * dist, ok, kv_t, q8, snew[:, g * LANES:(g + 1) * LANES], slope, use_new)
        kvw_t = win[g * LANES:(g + 1) * LANES, :].astype(BF16)
        wpos = p_len - nw + lax.broadcasted_iota(jnp.int32, (1, nw), 1)
        dist_w = (qpos - wpos).astype(F32)
        mask_w = jnp.logical_and(jnp.logical_and(dist_w >= 0, dist_w <= WINDOW), wpos >= 0)
        o_w = merge_new(_dot(q8, kvw_t) - slope * dist_w, mask_w, kvw_t, q8, wnew[:, g * LANES:(g + 1) * LANES], slope, True)
        hd = g * NSA_GROUP + row
        gate = lambda c: jnp.sum(jnp.where(lane == 3 * hd + c, gates, 0.0), axis=1, keepdims=True)
        mix = gate(0) * oc_ref[0, g] + gate(1) * o_s + gate(2) * o_w
        o_ref[0, g * NSA_GROUP:(g + 1) * NSA_GROUP, :] = mix[0:NSA_GROUP]
    req = lax.broadcasted_iota(jnp.int32, wnewt_ref.shape, 1)
    col = jnp.sum(jnp.where(req == b, wnewt_ref[...], 0.0), axis=1, keepdims=True)
    t_idx = lax.broadcasted_iota(jnp.int32, (1, nw), 1)
    nwin_ref[0] = jnp.where(t_idx == nw - 1, col, pltpu.roll(win, nw - 1, 1))


def _sel_decode(qn_dec, gl, o_c, sel_idx, slc_new, win_new, slc_pool_t, win_buf_t, page_table):
    db, n_pages = page_table.shape
    p_len = n_pages * PAGE_SIZE
    n_past_blocks = p_len // SLC_BLOCK
    per_page = PAGE_SIZE // SLC_BLOCK
    nw = win_buf_t.shape[2]
    win_new_t = jnp.transpose(win_new)

    def blk_spec(g, t):
        def imap(b, pt, sel):
            j = jnp.clip(sel[(b * NSA_KV_HEADS + g) * N_SELECT + t], 0, n_past_blocks - 1)
            return (pt[b * n_pages + j // per_page], g, 0)
        return pl.BlockSpec((None, GRP_W, PAGE_SIZE), imap)

    vec = lambda w: pl.BlockSpec((1, 1, w), lambda b, pt, sel: (b, 0, 0))
    grp = pl.BlockSpec((1, NSA_KV_HEADS, SUB, LANES), lambda b, pt, sel: (b, 0, 0, 0))
    grid_spec = pltpu.PrefetchScalarGridSpec(
        num_scalar_prefetch=2,
        grid=(db,),
        in_specs=[grp, vec(LANES), grp, vec(NSA_KV_W), vec(NSA_KV_W),
                  pl.BlockSpec(win_new_t.shape, lambda b, pt, sel: (0, 0)),
                  pl.BlockSpec((1, NSA_KV_W, nw), lambda b, pt, sel: (b, 0, 0))]
                 + [blk_spec(g, t) for g in range(NSA_KV_HEADS) for t in range(N_SELECT)],
        out_specs=(pl.BlockSpec((1, NSA_HEADS, LANES), lambda b, pt, sel: (b, 0, 0)),
                   pl.BlockSpec((1, NSA_KV_W, nw), lambda b, pt, sel: (b, 0, 0))),
    )
    r3 = lambda a: a.reshape(db, 1, a.shape[-1])
    return pl.pallas_call(
        functools.partial(_sel_decode_kernel, p_len=p_len),
        grid_spec=grid_spec,
        out_shape=(jax.ShapeDtypeStruct((db, NSA_HEADS, LANES), F32),
                   jax.ShapeDtypeStruct(win_buf_t.shape, win_buf_t.dtype)),
        compiler_params=_cparams(("parallel",)),
        name="nsa_sel_decode",
    )(page_table.reshape(-1), sel_idx, qn_dec, r3(gl), o_c, r3(slc_new), r3(win_new), win_new_t, win_buf_t,
      *([slc_pool_t] * (NSA_KV_HEADS * N_SELECT)))


TM_PROMPT = 256
TQ_MLA, TK_MLA = 256, 512
TQ_NSA, TK_NSA = 128, 512


def _mla_rows(rows):
    return jnp.concatenate([rows[:, :MLA_KV_RANK], rows[:, MLA_KV_RANK:MLA_KV_RANK + HALF_ROPE],
                            rows[:, MLA_KV_RANK + LANES:MLA_KV_RANK + LANES + HALF_ROPE]], axis=1)


def kernel(x_prompt, x_sample, cache_mla, cache_nsa_cmp, cache_nsa_slc, state_nsa_win, state_ret, page_table,
           norm_even, w_in_even, mla_gq, mla_gkv, mla_wuq, mla_wuk, mla_wuv, nsa_cmp_w, nsa_cmp_pe, w_out_even,
           norm_odd, w_in_odd, ret_gn, w_out_odd, final_norm):
    b, s, d = x_prompt.shape
    db, n_new, _ = x_sample.shape
    assert n_new == 1 and norm_even.shape[0] == 1 and norm_odd.shape[0] == 1
    n_pages = page_table.shape[1]
    p_len = n_pages * PAGE_SIZE
    kv_row = (NSA_KV_HEADS, 2, NSA_DH)
    tm = min(TM_PROMPT, b * s)

    w_ext, wuq_ext, wuk_ext, wuv_ext, wo_ext = _prep_even_weights(
        w_in_even[0], mla_wuq[0], mla_wuk[0], mla_wuv[0], w_out_even[0])
    w_big, pe_big = _prep_cmp_weights(nsa_cmp_w[0], nsa_cmp_pe[0])
    w_odd = w_in_odd[0].astype(BF16)
    wo_odd = w_out_odd[0].astype(BF16)
    gn_e, gq, gkv = norm_even[0][None], mla_gq[0][None], mla_gkv[0][None]
    gn_o, fn = norm_odd[0][None], final_norm[None]

    xp = x_prompt.reshape(b * s, d)
    pos_p = jnp.arange(s)
    cos_m, sin_m = _rope_tables_mla(pos_p)
    (qext, rows, kext, gmla, qn, cmp, slc, slcp, win, winp, gl, gnsa) = _even_project(
        xp, gn_e, w_ext, gq, gkv, wuq_ext, wuk_ext, cos_m, sin_m, tm)
    lat = _mla_prompt_attention(qext, kext, b, s, min(TQ_MLA, s), min(TK_MLA, s))
    kvc = _cmp_prompt(cmp, w_big, pe_big, b, s)
    onsa = _nsa_prompt(qn, gl, kvc, slcp, winp, b, s, TQ_NSA, min(TK_NSA, s))
    x1 = _even_output(xp, lat, gmla, onsa, gnsa, wuv_ext, wo_ext, tm)
    cos_r, sin_r = _rope_tables_ret(pos_p)
    q, k, v, g = _odd_project(x1, gn_o, w_odd, cos_r, sin_r, tm)
    y, ret_p = _retention_prompt(q, k, v, g, ret_gn[0], b, s)
    y_prompt = _odd_output(x1, y, wo_odd, fn, tm).reshape(b, s, d)
    nwin = min(WINDOW, s)
    mla_p = _mla_rows(rows).reshape(1, b, s, MLA_KV_RANK + MLA_ROPE)
    cmp_p = cmp.reshape((1, b, s) + kv_row)
    slc_p = slc.reshape((1, b, s) + kv_row)
    win_p = win.reshape((b, s) + kv_row)[:, s - nwin:][None]

    xs = x_sample.reshape(db, d)
    pos_s = p_len + jnp.arange(n_new)
    cos_s, sin_s = [jnp.broadcast_to(t, (db, LANES)) for t in _rope_tables_mla(pos_s)]
    (qext_s, rows_s, _, gmla_s, qn_s, cmp_s, slc_s, _, win_s, _, gl_s, gnsa_s) = _even_project(
        xs, gn_e, w_ext, gq, gkv, wuq_ext, wuk_ext, cos_s, sin_s, db)
    rows_new = _mla_rows(rows_s)
    rope = lambda a: a.astype(F32).reshape(MLA_HEADS, db, MLA_HEADS, HALF_ROPE).sum(2).astype(BF16)
    q_dec = jnp.concatenate([qext_s[:, :, :MLA_KV_RANK], rope(qext_s[:, :, MLA_KV_RANK:MLA_KV_RANK + LANES]),
                             rope(qext_s[:, :, MLA_KV_RANK + LANES:])], axis=-1)
    pool = cache_nsa_cmp.shape[1]
    feat_major = lambda a, n: jnp.transpose(a[0], (0, 2, 3, 4, 1)).reshape(n, NSA_KV_W, a.shape[2])
    mla_pool_t = jnp.transpose(cache_mla[0], (0, 2, 1))
    lat_s = _mla_decode(jnp.transpose(q_dec, (1, 0, 2)), rows_new, mla_pool_t, page_table)
    qn_dec = jnp.pad(qn_s.reshape(db, NSA_KV_HEADS, NSA_GROUP, LANES), ((0, 0), (0, 0), (0, SUB - NSA_GROUP), (0, 0)))
    w_pair, pe_pair = _prep_cmp_pair_weights(nsa_cmp_w[0], nsa_cmp_pe[0])
    o_c, s_slc = _cmp_decode(qn_dec, cmp_s, feat_major(cache_nsa_cmp, pool), page_table, w_pair,
                             _cmp_bias(w_pair, pe_pair))
    sel = _topk_decode(s_slc[:, :, 0, :].reshape(db * NSA_KV_HEADS, -1), p_len)
    sel_idx = sel[:, :N_SELECT].reshape(-1)
    onsa_s, win_new_t = _sel_decode(qn_dec, gl_s, o_c, sel_idx, slc_s, win_s, feat_major(cache_nsa_slc, pool),
                                    feat_major(state_nsa_win, db), page_table)
    nw = win_new_t.shape[2]
    win_so = jnp.transpose(win_new_t.reshape((db,) + kv_row + (nw,)), (0, 4, 1, 2, 3))[None]
    x1s = _even_output(xs, jnp.transpose(lat_s, (1, 0, 2)), gmla_s, onsa_s.reshape(db, NSA_HEADS * LANES), gnsa_s,
                       wuv_ext, wo_ext, db)
    cos_rs, sin_rs = [jnp.broadcast_to(t, (db, RET_DK)) for t in _rope_tables_ret(pos_s)]
    q, k, v, g = _odd_project(x1s, gn_o, w_odd, cos_rs, sin_rs, db)
    ys, ret_s = _retention_decode(q, k, v, g, ret_gn[0], state_ret.reshape(state_ret.shape[1:]))
    y_sample = _odd_output(x1s, ys, wo_odd, fn, db).reshape(db, n_new, d)
    mla_s = rows_new.reshape(1, db, n_new, MLA_KV_RANK + MLA_ROPE)
    cmp_so = cmp_s.reshape((1, db, n_new) + kv_row)
    slc_so = slc_s.reshape((1, db, n_new) + kv_row)
    return (y_prompt, y_sample, mla_p, cmp_p, slc_p, win_p, ret_p[None],
            mla_s, cmp_so, slc_so, win_so, ret_s[None])
```

```python
import functools

import numpy as np
import jax
import jax.numpy as jnp
from jax import lax
from jax.experimental import pallas as pl
from jax.experimental.pallas import tpu as pltpu

F32 = jnp.float32
BF16 = jnp.bfloat16

D_MODEL = 1024
PAGE_SIZE = 128
MLA_HEADS = 8
MLA_NOPE = 64
MLA_ROPE = 32
MLA_V = 64
MLA_Q_RANK = 768
MLA_KV_RANK = 256
MLA_WIDTH = MLA_HEADS * MLA_V
MLA_SCALE = (MLA_NOPE + MLA_ROPE) ** -0.5
NSA_HEADS = 8
NSA_KV_HEADS = 2
NSA_GROUP = NSA_HEADS // NSA_KV_HEADS
NSA_DH = 64
NSA_WIDTH = NSA_HEADS * NSA_DH
NSA_KV_W = NSA_KV_HEADS * 2 * NSA_DH
CMP_BLOCK = 32
CMP_STRIDE = 16
SLC_BLOCK = 64
N_SELECT = 16
WINDOW = 512
FORCE_SCORE = 1e4
RET_HEADS = 8
RET_DK = 128
RET_DV = 256
RET_WIDTH = RET_HEADS * RET_DV
RET_CHUNK = 128
ROPE_BASE = 10000.0
EPS = 1e-6
EVEN_SPLITS = (MLA_Q_RANK, MLA_KV_RANK, MLA_ROPE, MLA_WIDTH, NSA_WIDTH, NSA_KV_W, NSA_KV_W, NSA_KV_W,
               3 * NSA_HEADS, NSA_WIDTH)

LANES = 128
VMEM_LIMIT_BYTES = 56 * 1024 * 1024
NEG_BIG = -1e30
MASK_BIG = 16384.0

HALF_ROPE = MLA_ROPE // 2


def _cparams(sem):
    return pltpu.CompilerParams(dimension_semantics=sem, vmem_limit_bytes=VMEM_LIMIT_BYTES)


def _full_spec(a):
    nd = a.ndim
    return pl.BlockSpec(a.shape, lambda *_: (0,) * nd)


def _rms(x, g):
    y = x * lax.rsqrt(jnp.mean(x * x, axis=-1, keepdims=True) + EPS)
    return y * g


def _dot(a, b):
    return jnp.dot(a, b, preferred_element_type=F32)


def _dot_nt(a, b):
    return lax.dot_general(a, b, (((1,), (1,)), ((), ())), preferred_element_type=F32)


def _fold_lanes(x, op):
    n = x.shape[1]
    if n % LANES:
        return x
    parts = [x[:, i:i + LANES] for i in range(0, n, LANES)]
    while len(parts) > 1:
        parts = [op(parts[i], parts[i + 1]) if i + 1 < len(parts) else parts[i] for i in range(0, len(parts), 2)]
    return parts[0]


def _row_max(x):
    return jnp.max(_fold_lanes(x, jnp.maximum), axis=1, keepdims=True)


def _row_sum(x):
    return jnp.sum(_fold_lanes(x, jnp.add), axis=1, keepdims=True)


def _rep(col):
    return jnp.broadcast_to(col, (col.shape[0], LANES))


def _tile_lanes(x, n):
    if n % LANES:
        return jnp.broadcast_to(x[:, :1], (x.shape[0], n))
    return x if n == LANES else jnp.tile(x, (1, n // LANES))


_EVEN_GROUPS = (
    ("cq", MLA_Q_RANK), ("ckv", MLA_KV_RANK), ("kr1", LANES), ("kr2", LANES),
    ("gmla", MLA_HEADS * LANES), ("qn", NSA_HEADS * LANES), ("cmp", NSA_KV_W),
    ("slc", NSA_KV_W), ("slcp", 4 * LANES), ("win", NSA_KV_W), ("winp", 4 * LANES),
    ("gl", LANES), ("gnsa", NSA_HEADS * LANES),
)
_EVEN_OFF = {}
_o = 0
for _n, _w in _EVEN_GROUPS:
    _EVEN_OFF[_n] = (_o, _o + _w)
    _o += _w
EVEN_EXT = _o


def _pad_heads(w, nh, dh):
    k = w.shape[0]
    w = w.reshape(k, nh, dh)
    return jnp.pad(w, ((0, 0), (0, 0), (0, LANES - dh))).reshape(k, nh * LANES)


def _kv_pad(w):
    k = w.shape[0]
    w4 = w.reshape(k, NSA_KV_HEADS, 2, NSA_DH)
    w4 = jnp.transpose(w4, (0, 2, 1, 3))
    return jnp.pad(w4, ((0, 0), (0, 0), (0, 0), (0, LANES - NSA_DH))).reshape(k, 4 * LANES)


def _prep_even_weights(w_in, wuq, wuk, wuv, w_out):
    offs = np.cumsum((0,) + EVEN_SPLITS)
    cq, ckv, kr, g_mla, q_nsa, kv_cmp, kv_slc, kv_win, gl, g_nsa = [
        w_in[:, offs[i]:offs[i + 1]] for i in range(len(EVEN_SPLITS))]
    parts = {
        "cq": cq, "ckv": ckv,
        "kr1": jnp.tile(kr[:, :HALF_ROPE], (1, MLA_HEADS)),
        "kr2": jnp.tile(kr[:, HALF_ROPE:], (1, MLA_HEADS)),
        "gmla": _pad_heads(g_mla, MLA_HEADS, MLA_V),
        "qn": _pad_heads(q_nsa, NSA_HEADS, NSA_DH),
        "cmp": kv_cmp, "slc": kv_slc, "slcp": _kv_pad(kv_slc),
        "win": kv_win, "winp": _kv_pad(kv_win),
        "gl": jnp.pad(gl, ((0, 0), (0, LANES - gl.shape[1]))),
        "gnsa": _pad_heads(g_nsa, NSA_HEADS, NSA_DH),
    }
    w_ext = jnp.concatenate([parts[n] for n, _ in _EVEN_GROUPS], axis=1).astype(BF16)
    c = wuq.shape[0]
    nope = jnp.pad(wuq[:, :, :MLA_NOPE], ((0, 0), (0, 0), (0, LANES - MLA_NOPE))).reshape(c, MLA_HEADS * LANES)
    r1 = wuq[:, :, MLA_NOPE:MLA_NOPE + HALF_ROPE].reshape(c, MLA_HEADS * HALF_ROPE)
    r2 = wuq[:, :, MLA_NOPE + HALF_ROPE:].reshape(c, MLA_HEADS * HALF_ROPE)
    wuq_ext = jnp.concatenate([nope, r1, r2], axis=1).astype(BF16)
    wuk_ext = jnp.pad(jnp.transpose(wuk, (1, 2, 0)), ((0, 0), (0, LANES - MLA_NOPE), (0, 0))).astype(BF16)
    wuv_ext = jnp.pad(jnp.transpose(wuv, (1, 0, 2)), ((0, 0), (0, 0), (0, LANES - MLA_V))).astype(BF16)
    d = w_out.shape[1]
    wo = w_out.reshape(MLA_HEADS + NSA_HEADS, MLA_V, d)
    wo_ext = jnp.pad(wo, ((0, 0), (0, LANES - MLA_V), (0, 0))).reshape((MLA_HEADS + NSA_HEADS) * LANES, d).astype(BF16)
    return w_ext, wuq_ext, wuk_ext, wuv_ext, wo_ext


def _rope_tables_mla(pos):
    inv = ROPE_BASE ** (-jnp.arange(HALF_ROPE, dtype=F32) / HALF_ROPE)
    ang = pos.astype(F32)[:, None] * inv
    return jnp.tile(jnp.cos(ang), (1, LANES // HALF_ROPE)), jnp.tile(jnp.sin(ang), (1, LANES // HALF_ROPE))


def _even_proj_kernel(x_ref, gn_ref, w_ref, gq_ref, gkv_ref, wuq_ref, wuk_ref, cos_ref, sin_ref,
                      qext_ref, rows_ref, kext_ref, gmla_ref, qn_ref, cmp_ref, slc_ref, slcp_ref,
                      win_ref, winp_ref, gl_ref, gnsa_ref):
    h = _rms(x_ref[...], gn_ref[...]).astype(BF16)

    def proj(name):
        a, b = _EVEN_OFF[name]
        return _dot(h, w_ref[:, a:b])

    cos = cos_ref[...]
    sin = sin_ref[...]
    cqn = _rms(proj("cq"), gq_ref[...]).astype(BF16)
    nh = MLA_HEADS * LANES
    r1 = _dot(cqn, wuq_ref[:, nh:nh + LANES])
    r2 = _dot(cqn, wuq_ref[:, nh + LANES:nh + 2 * LANES])
    o1 = r1 * cos - r2 * sin
    o2 = r1 * sin + r2 * cos
    head_of_lane = lax.broadcasted_iota(jnp.int32, (1, LANES), 1) // HALF_ROPE
    for hd in range(MLA_HEADS):
        nope = _dot(cqn, wuq_ref[:, hd * LANES:(hd + 1) * LANES]).astype(BF16)
        qext_ref[hd, :, 0:MLA_KV_RANK] = _dot(nope, wuk_ref[hd]).astype(BF16)
        sel = head_of_lane == hd
        qext_ref[hd, :, MLA_KV_RANK:MLA_KV_RANK + LANES] = jnp.where(sel, o1, 0.0).astype(BF16)
        qext_ref[hd, :, MLA_KV_RANK + LANES:] = jnp.where(sel, o2, 0.0).astype(BF16)
    latn = _rms(proj("ckv"), gkv_ref[...])
    kr1 = proj("kr1")
    kr2 = proj("kr2")
    k1 = kr1 * cos - kr2 * sin
    k2 = kr1 * sin + kr2 * cos
    rows_ref[:, 0:MLA_KV_RANK] = latn
    rows_ref[:, MLA_KV_RANK:MLA_KV_RANK + LANES] = k1
    rows_ref[:, MLA_KV_RANK + LANES:] = k2
    kext_ref[:, 0:MLA_KV_RANK] = latn.astype(BF16)
    kext_ref[:, MLA_KV_RANK:MLA_KV_RANK + LANES] = k1.astype(BF16)
    kext_ref[:, MLA_KV_RANK + LANES:] = k2.astype(BF16)
    gmla_ref[...] = proj("gmla")
    qn_ref[...] = (proj("qn") * (NSA_DH ** -0.5)).astype(BF16)
    cmp_ref[...] = proj("cmp")
    slc_ref[...] = proj("slc")
    slcp_ref[...] = proj("slcp").astype(BF16)
    win_ref[...] = proj("win")
    winp_ref[...] = proj("winp").astype(BF16)
    gl_ref[...] = proj("gl")
    gnsa_ref[...] = proj("gnsa")


def _even_project(x, gn, w_ext, gq, gkv, wuq_ext, wuk_ext, cos, sin, tm):
    t = x.shape[0]
    nt = t // tm
    ntab = cos.shape[0] // tm
    row = lambda w: pl.BlockSpec((tm, w), lambda i: (i, 0))
    tab = pl.BlockSpec((tm, LANES), lambda i: (i % ntab, 0))
    kext_w = MLA_KV_RANK + 2 * LANES
    out_shapes = (
        jax.ShapeDtypeStruct((MLA_HEADS, t, kext_w), BF16),
        jax.ShapeDtypeStruct((t, kext_w), F32),
        jax.ShapeDtypeStruct((t, kext_w), BF16),
        jax.ShapeDtypeStruct((t, MLA_HEADS * LANES), F32),
        jax.ShapeDtypeStruct((t, NSA_HEADS * LANES), BF16),
        jax.ShapeDtypeStruct((t, NSA_KV_W), F32),
        jax.ShapeDtypeStruct((t, NSA_KV_W), F32),
        jax.ShapeDtypeStruct((t, 4 * LANES), BF16),
        jax.ShapeDtypeStruct((t, NSA_KV_W), F32),
        jax.ShapeDtypeStruct((t, 4 * LANES), BF16),
        jax.ShapeDtypeStruct((t, LANES), F32),
        jax.ShapeDtypeStruct((t, NSA_HEADS * LANES), F32),
    )
    out_specs = (
        pl.BlockSpec((MLA_HEADS, tm, kext_w), lambda i: (0, i, 0)),
        row(kext_w), row(kext_w), row(MLA_HEADS * LANES), row(NSA_HEADS * LANES), row(NSA_KV_W),
        row(NSA_KV_W), row(4 * LANES), row(NSA_KV_W), row(4 * LANES), row(LANES), row(NSA_HEADS * LANES),
    )
    return pl.pallas_call(
        _even_proj_kernel,
        grid=(nt,),
        in_specs=[row(D_MODEL), _full_spec(gn), _full_spec(w_ext), _full_spec(gq), _full_spec(gkv),
                  _full_spec(wuq_ext), _full_spec(wuk_ext), tab, tab],
        out_specs=out_specs,
        out_shape=out_shapes,
        compiler_params=_cparams(("parallel",)),
        name="even_proj",
    )(x, gn, w_ext, gq, gkv, wuq_ext, wuk_ext, cos, sin)


LOG2E = float(np.log2(np.e))


def _mla_attn_kernel(q_ref, k_ref, kn_ref, o_ref, m_sc, l_sc, acc_sc, sa_sc, sb_sc, *, tq, tk):
    qi = pl.program_id(1)
    ki = pl.program_id(2)
    n_needed = ((qi + 1) * tq + tk - 1) // tk
    m_rows = MLA_HEADS * tq
    c_exp = MLA_SCALE * LOG2E

    def scores(kref):
        return _dot_nt(q_ref[...].reshape(m_rows, q_ref.shape[-1]), kref[...])

    @pl.when(ki == 0)
    def _():
        m_sc[...] = jnp.full(m_sc.shape, NEG_BIG, F32)
        l_sc[...] = jnp.zeros(l_sc.shape, F32)
        acc_sc[...] = jnp.zeros(acc_sc.shape, F32)
        sa_sc[...] = scores(k_ref)

    def step(masked, cur_sc, nxt_sc):
        nxt_sc[...] = scores(kn_ref)
        s = cur_sc[...]
        if masked:
            qpos = qi * tq + jnp.bitwise_and(lax.broadcasted_iota(jnp.int32, (m_rows, 1), 0), tq - 1)
            s = jnp.where((ki * tk + lax.broadcasted_iota(jnp.int32, (1, tk), 1)) <= qpos, s, NEG_BIG)
        m_prev = m_sc[...]
        m_new = jnp.maximum(m_prev, _rep(_row_max(s)))
        alpha = jnp.exp2((m_prev - m_new) * c_exp)
        p = jnp.exp2((s - _tile_lanes(m_new, tk)) * c_exp)
        l_sc[...] = alpha * l_sc[...] + _fold_lanes(p, jnp.add)
        acc_sc[...] = _tile_lanes(alpha, MLA_KV_RANK) * acc_sc[...] + _dot(p.astype(BF16), k_ref[:, :MLA_KV_RANK])
        m_sc[...] = m_new

    crosses = (ki + 1) * tk - 1 > qi * tq
    even = lax.rem(ki, 2) == 0
    for par, (cur_sc, nxt_sc) in enumerate(((sa_sc, sb_sc), (sb_sc, sa_sc))):
        for masked in (True, False):
            cond = jnp.logical_and(ki < n_needed, jnp.logical_and(even == (par == 0), crosses == masked))
            pl.when(cond)(functools.partial(step, masked, cur_sc, nxt_sc))

    @pl.when(ki == n_needed - 1)
    def _():
        out = acc_sc[...] / jnp.maximum(jnp.sum(l_sc[...], axis=1, keepdims=True), 1e-30)
        o_ref[...] = out.reshape(MLA_HEADS, tq, MLA_KV_RANK).astype(o_ref.dtype)


def _mla_prompt_attention(q_ext, k_ext, batch, seq, tq, tk):
    nq, nk = seq // tq, seq // tk
    kw = q_ext.shape[-1]

    def k_map(off):
        def imap(b, qi, ki):
            last = ((qi + 1) * tq + tk - 1) // tk - 1
            return (b * nk + jnp.minimum(ki + off, last), 0)
        return imap

    return pl.pallas_call(
        functools.partial(_mla_attn_kernel, tq=tq, tk=tk),
        grid=(batch, nq, nk),
        in_specs=[pl.BlockSpec((MLA_HEADS, tq, kw), lambda b, qi, ki: (0, b * nq + qi, 0)),
                  pl.BlockSpec((tk, kw), k_map(0)), pl.BlockSpec((tk, kw), k_map(1))],
        out_specs=pl.BlockSpec((MLA_HEADS, tq, MLA_KV_RANK), lambda b, qi, ki: (0, b * nq + qi, 0)),
        out_shape=jax.ShapeDtypeStruct((MLA_HEADS, batch * seq, MLA_KV_RANK), BF16),
        scratch_shapes=[pltpu.VMEM((MLA_HEADS * tq, LANES), F32), pltpu.VMEM((MLA_HEADS * tq, LANES), F32),
                        pltpu.VMEM((MLA_HEADS * tq, MLA_KV_RANK), F32),
                        pltpu.VMEM((MLA_HEADS * tq, tk), F32), pltpu.VMEM((MLA_HEADS * tq, tk), F32)],
        compiler_params=_cparams(("parallel", "parallel", "arbitrary")),
        name="mla_prompt_attn",
    )(q_ext, k_ext, k_ext)


SEG_W = CMP_STRIDE * NSA_KV_W
KVP_W = 4 * LANES


def _prep_cmp_weights(w_cmp, pe_cmp):
    wh = w_cmp.reshape(2, 2, CMP_STRIDE, NSA_DH, NSA_DH)
    big = jnp.zeros((2, CMP_STRIDE, NSA_KV_HEADS, 2, NSA_DH, 2, NSA_KV_HEADS, LANES), F32)
    for g in range(NSA_KV_HEADS):
        for c in range(2):
            big = big.at[:, :, g, c, :, c, g, :NSA_DH].set(wh[c])
    w_big = big.reshape(2, SEG_W, KVP_W).astype(BF16)
    peh = pe_cmp.reshape(2, 2, CMP_STRIDE, NSA_DH)
    pe_big = jnp.broadcast_to(jnp.transpose(peh, (1, 2, 0, 3))[:, :, None], (2, CMP_STRIDE, NSA_KV_HEADS, 2, NSA_DH))
    return w_big, pe_big.reshape(2, 1, SEG_W)


def _slc_overlap_matrix(n_cmp_rows, n_lanes):
    ratio = SLC_BLOCK // CMP_STRIDE
    lead = CMP_BLOCK // CMP_STRIDE - 1
    a = np.zeros((n_cmp_rows, n_lanes), np.float32)
    for j in range(n_lanes):
        for o in range(-lead, ratio):
            n = ratio * j + o
            if 0 <= n < n_cmp_rows:
                a[n, j] = (min(CMP_STRIDE * o + CMP_BLOCK, SLC_BLOCK) - max(CMP_STRIDE * o, 0)) / CMP_BLOCK
    return a


def _key_position_features(n_keys, n_lanes):
    kc = np.zeros((n_keys, LANES + n_lanes), np.float32)
    pos = np.arange(n_keys)
    kc[:, NSA_DH] = (pos // SLC_BLOCK) * SLC_BLOCK
    kc[:, NSA_DH + 1] = pos % SLC_BLOCK
    kc[pos, LANES + pos // SLC_BLOCK] = 1.0
    return kc


def _slopes_col(g, rows, tq):
    r = lax.shift_right_logical(lax.broadcasted_iota(jnp.int32, (rows, 1), 0), int(np.log2(tq)))
    col = jnp.zeros((rows, 1), F32)
    for rr in range(NSA_GROUP):
        col = jnp.where(r == rr, 2.0 ** (-(g * NSA_GROUP + rr + 1.0)), col)
    return col


def _softmax_rows(s, mask):
    n = s.shape[1]
    s = jnp.where(mask, s, NEG_BIG)
    m = _rep(_row_max(s))
    e = jnp.where(mask, jnp.exp(s - _tile_lanes(m, n)), 0.0)
    inv = 1.0 / jnp.maximum(_rep(_row_sum(e)), 1e-30)
    return e * _tile_lanes(inv, n)


def _top_blocks(score, n_pick):
    nb = score.shape[0]
    blk = lax.broadcasted_iota(jnp.int32, score.shape, 0).astype(F32)
    sel = jnp.zeros(score.shape, F32)
    for _ in range(n_pick):
        m = jnp.max(score, axis=0, keepdims=True)
        idx = jnp.min(jnp.where(score == m, blk, float(nb)), axis=0, keepdims=True)
        hit = blk == idx
        sel = jnp.where(jnp.logical_and(hit, m > 0.5 * NEG_BIG), 1.0, sel)
        score = jnp.where(hit, 2.0 * NEG_BIG, score)
    return sel


def _split3(x):
    hi = x.astype(BF16)
    r1 = x - hi.astype(F32)
    mid = r1.astype(BF16)
    lo = (r1 - mid.astype(F32)).astype(BF16)
    return hi, mid, lo


def _cmp_prompt_kernel(x_ref, pe_ref, w_ref, kvc_ref):
    x = x_ref[...]
    h1 = _dot((x + pe_ref[0]).astype(BF16), w_ref[0])
    h2 = _dot((x + pe_ref[1]).astype(BF16), w_ref[1])
    n = x.shape[0]
    kvc_ref[...] = (h1 + pltpu.roll(h2, n - 1, 0)).astype(kvc_ref.dtype)


def _cmp_prompt(kv_cmp, w_big, pe_big, batch, seq):
    nseg = seq // CMP_STRIDE
    x = kv_cmp.reshape(batch * nseg, SEG_W)
    return pl.pallas_call(
        _cmp_prompt_kernel,
        grid=(batch,),
        in_specs=[pl.BlockSpec((nseg, SEG_W), lambda b: (b, 0)), _full_spec(pe_big), _full_spec(w_big)],
        out_specs=pl.BlockSpec((nseg, KVP_W), lambda b: (b, 0)),
        out_shape=jax.ShapeDtypeStruct((batch * nseg, KVP_W), BF16),
        compiler_params=_cparams(("parallel",)),
        name="nsa_cmp_prompt",
    )(x, pe_big, w_big)


N_WIN_BLOCKS = WINDOW // 128 + 1


def _nsa_prompt_kernel(qn_ref, gl_ref, kvc_ref, amat_ref, slcp_ref, kc_ref, *rest, tq, tk, nseg):
    win_refs = rest[:N_WIN_BLOCKS]
    o_ref = rest[N_WIN_BLOCKS]
    stat = rest[N_WIN_BLOCKS + 1:N_WIN_BLOCKS + 1 + 3 * NSA_KV_HEADS]
    flag_ref = rest[-1]
    nt_all = slcp_ref.shape[0] // tk
    qi = pl.program_id(1)
    s0 = qi * tq
    rows = NSA_GROUP * tq
    nl = amat_ref.shape[1]
    qpos = s0 + jnp.bitwise_and(lax.broadcasted_iota(jnp.int32, (rows, 1), 0), tq - 1)
    qpos_t = s0 + lax.broadcasted_iota(jnp.int32, (tq, 1), 0)
    gates = jax.nn.sigmoid(gl_ref[...])
    n_tiles = (s0 + tq + tk - 1) // tk
    lane = lax.broadcasted_iota(jnp.int32, (1, LANES), 1)
    is_pos_lane = jnp.logical_or(lane == NSA_DH, lane == NSA_DH + 1)
    blk = lax.broadcasted_iota(jnp.int32, (1, nl), 1)
    cur = lax.shift_right_logical(qpos_t, int(np.log2(SLC_BLOCK)))
    valid = blk * SLC_BLOCK <= qpos_t
    forced = jnp.logical_or(blk == 0, jnp.logical_or(blk == cur, blk == cur - 1))
    cpos = lax.broadcasted_iota(jnp.int32, (1, nseg), 1) * CMP_STRIDE + (CMP_BLOCK - 1)
    dist_c = (qpos - cpos).astype(F32)
    amat = amat_ref[...]

    slopes, qgs, o_cs, q_augs = [], [], [], []
    any_sel = jnp.zeros((1, nl), F32)
    for g in range(NSA_KV_HEADS):
        slope = _slopes_col(g, rows, tq)
        q_heads = [qn_ref[:, (g * NSA_GROUP + r) * LANES:(g * NSA_GROUP + r + 1) * LANES] for r in range(NSA_GROUP)]
        qg = jnp.concatenate(q_heads, axis=0)
        kc = kvc_ref[:, g * LANES:(g + 1) * LANES]
        vc = kvc_ref[:, (2 + g) * LANES:(3 + g) * LANES]
        p_c = _softmax_rows(_dot_nt(qg, kc) - slope * dist_c, dist_c >= 0)
        o_cs.append(_dot(p_c.astype(BF16), vc))
        p_grp = p_c[0:tq]
        for r in range(1, NSA_GROUP):
            p_grp = p_grp + p_c[r * tq:(r + 1) * tq]
        s_slc = sum(_dot(t, amat) for t in _split3(p_grp))
        score = jnp.where(forced, FORCE_SCORE, jnp.where(valid, s_slc, NEG_BIG))
        sel = _top_blocks(score.T, N_SELECT).T
        selneg = ((sel - 1.0) * MASK_BIG).astype(BF16)
        any_sel = jnp.maximum(any_sel, jnp.max(sel, axis=0, keepdims=True))
        q_augs.append(jnp.concatenate(
            [jnp.concatenate([jnp.where(is_pos_lane, (2.0 ** (-(g * NSA_GROUP + r + 1.0))), q_heads[r].astype(F32)).astype(BF16),
                              selneg], axis=1) for r in range(NSA_GROUP)], axis=0))
        slopes.append(slope)
        qgs.append(qg)
        m_sc, l_sc, acc_sc = stat[3 * g:3 * g + 3]
        m_sc[...] = jnp.full(m_sc.shape, NEG_BIG, F32)
        l_sc[...] = jnp.zeros(l_sc.shape, F32)
        acc_sc[...] = jnp.zeros(acc_sc.shape, F32)
    tile_of_blk = lax.shift_right_logical(blk, int(np.log2(tk // SLC_BLOCK)))
    for t in range(nt_all):
        flag_ref[t] = jnp.max(jnp.where(tile_of_blk == t, any_sel, 0.0)).astype(jnp.int32)

    def slc_tile(kt, causal):
        k0 = pl.multiple_of(kt * tk, tk)
        if causal:
            ok = (k0 + lax.broadcasted_iota(jnp.int32, (1, tk), 1)) <= qpos
        for g in range(NSA_KV_HEADS):
            m_sc, l_sc, acc_sc = stat[3 * g:3 * g + 3]
            kk = slcp_ref[pl.ds(k0, tk), g * LANES:(g + 1) * LANES] + kc_ref[pl.ds(k0, tk), 0:LANES]
            k_aug = jnp.concatenate([kk, kc_ref[pl.ds(k0, tk), LANES:]], axis=1)
            vv = slcp_ref[pl.ds(k0, tk), (2 + g) * LANES:(3 + g) * LANES]
            s = _dot_nt(q_augs[g], k_aug)
            if causal:
                s = jnp.where(ok, s, NEG_BIG)
            m_prev = m_sc[...]
            m_new = jnp.maximum(m_prev, _rep(_row_max(s)))
            alpha = jnp.exp(m_prev - m_new)
            p = jnp.exp(s - _tile_lanes(m_new, tk))
            if causal:
                p = jnp.where(ok, p, 0.0)
            l_sc[...] = alpha * l_sc[...] + _fold_lanes(p, jnp.add)
            acc_sc[...] = alpha * acc_sc[...] + _dot(p.astype(BF16), vv)
            m_sc[...] = m_new

    def body(kt, carry):
        @pl.when(flag_ref[kt] > 0)
        def _():
            slc_tile(kt, False)
        return carry

    lax.fori_loop(0, n_tiles - 1, body, 0)
    slc_tile(n_tiles - 1, True)
    nw = N_WIN_BLOCKS * 128
    wpos = s0 - WINDOW + lax.broadcasted_iota(jnp.int32, (1, nw), 1)
    dist_w = (qpos - wpos).astype(F32)
    mask_w = jnp.logical_and(jnp.logical_and(dist_w >= 0, dist_w <= WINDOW), wpos >= 0)
    for g in range(NSA_KV_HEADS):
        m_sc, l_sc, acc_sc = stat[3 * g:3 * g + 3]
        o_s = acc_sc[...] / jnp.maximum(jnp.sum(l_sc[...], axis=1, keepdims=True), 1e-30)
        kw = jnp.concatenate([w[:, g * LANES:(g + 1) * LANES] for w in win_refs], axis=0)
        vw = jnp.concatenate([w[:, (2 + g) * LANES:(3 + g) * LANES] for w in win_refs], axis=0)
        p_w = _softmax_rows(_dot_nt(qgs[g], kw) - slopes[g] * dist_w, mask_w)
        o_w = _dot(p_w.astype(BF16), vw)
        for r in range(NSA_GROUP):
            hd = g * NSA_GROUP + r
            sl = slice(r * tq, (r + 1) * tq)
            o_ref[:, hd * LANES:(hd + 1) * LANES] = (
                gates[:, 3 * hd:3 * hd + 1] * o_cs[g][sl] + gates[:, 3 * hd + 1:3 * hd + 2] * o_s[sl]
                + gates[:, 3 * hd + 2:3 * hd + 3] * o_w[sl])


def _nsa_prompt(qn, gl, kvc, slcp, winp, batch, seq, tq, tk):
    assert tq == 128 and seq % tk == 0 and tk % tq == 0
    nseg = seq // CMP_STRIDE
    n_slc = seq // SLC_BLOCK
    nl = LANES * ((n_slc + LANES - 1) // LANES)
    assert nl == LANES, "selection blocks must fit one lane group"
    nq = seq // tq
    amat = jnp.asarray(_slc_overlap_matrix(nseg, nl), BF16)
    kc = jnp.asarray(_key_position_features(seq, nl), BF16)

    def win_spec(j):
        return pl.BlockSpec((128, KVP_W), lambda b, qi: (b * nq + jnp.maximum(qi - (N_WIN_BLOCKS - 1) + j, 0), 0))

    rows = NSA_GROUP * tq
    return pl.pallas_call(
        functools.partial(_nsa_prompt_kernel, tq=tq, tk=tk, nseg=nseg),
        grid=(batch, nq),
        in_specs=[pl.BlockSpec((tq, NSA_HEADS * LANES), lambda b, qi: (b * nq + qi, 0)),
                  pl.BlockSpec((tq, LANES), lambda b, qi: (b * nq + qi, 0)),
                  pl.BlockSpec((nseg, KVP_W), lambda b, qi: (b, 0)),
                  _full_spec(amat),
                  pl.BlockSpec((seq, KVP_W), lambda b, qi: (b, 0)),
                  _full_spec(kc)] + [win_spec(j) for j in range(N_WIN_BLOCKS)],
        out_specs=pl.BlockSpec((tq, NSA_HEADS * LANES), lambda b, qi: (b * nq + qi, 0)),
        out_shape=jax.ShapeDtypeStruct((batch * seq, NSA_HEADS * LANES), F32),
        scratch_shapes=[pltpu.VMEM((rows, LANES), F32), pltpu.VMEM((rows, LANES), F32), pltpu.VMEM((rows, LANES), F32)]
        * NSA_KV_HEADS + [pltpu.SMEM((seq // tk,), jnp.int32)],
        compiler_params=_cparams(("parallel", "arbitrary")),
        name="nsa_prompt",
    )(qn, gl, kvc, amat, slcp, kc, *([winp] * N_WIN_BLOCKS))


def _silu(x):
    return x * jax.nn.sigmoid(x)


def _even_out_kernel(x_ref, lat_ref, gmla_ref, onsa_ref, gnsa_ref, wuv_ref, wo_ref, o_ref):
    o_mla = jnp.concatenate([_dot(lat_ref[hd], wuv_ref[hd]) for hd in range(MLA_HEADS)], axis=1)
    a = (o_mla * _silu(gmla_ref[...])).astype(BF16)
    b = (onsa_ref[...] * _silu(gnsa_ref[...])).astype(BF16)
    nm = MLA_HEADS * LANES
    o_ref[...] = x_ref[...] + _dot(a, wo_ref[0:nm]) + _dot(b, wo_ref[nm:])


def _even_output(x, lat, gmla, onsa, gnsa, wuv_ext, wo_ext, tm):
    t = x.shape[0]
    row = lambda w: pl.BlockSpec((tm, w), lambda i: (i, 0))
    return pl.pallas_call(
        _even_out_kernel,
        grid=(t // tm,),
        in_specs=[row(D_MODEL), pl.BlockSpec((MLA_HEADS, tm, MLA_KV_RANK), lambda i: (0, i, 0)),
                  row(MLA_HEADS * LANES), row(NSA_HEADS * LANES), row(NSA_HEADS * LANES),
                  _full_spec(wuv_ext), _full_spec(wo_ext)],
        out_specs=row(D_MODEL),
        out_shape=jax.ShapeDtypeStruct((t, D_MODEL), F32),
        compiler_params=_cparams(("parallel",)),
        name="even_out",
    )(x, lat, gmla, onsa, gnsa, wuv_ext, wo_ext)


_RET_QK = RET_HEADS * RET_DK
_RET_LOG_G = [float(np.log1p(-(2.0 ** (-5.0 - h)))) for h in range(RET_HEADS)]


def _rope_tables_ret(pos):
    half = RET_DK // 2
    inv = ROPE_BASE ** (-jnp.arange(half, dtype=F32) / half)
    ang = pos.astype(F32)[:, None] * inv
    cos, sin = jnp.cos(ang), jnp.sin(ang)
    return jnp.concatenate([cos, cos], 1), jnp.concatenate([-sin, sin], 1)


def _odd_proj_kernel(x_ref, gn_ref, w_ref, c_ref, s_ref, q_ref, k_ref, v_ref, g_ref):
    h = _rms(x_ref[...], gn_ref[...]).astype(BF16)
    c = c_ref[...]
    s = s_ref[...]
    half = RET_DK // 2

    def rot(z):
        return z * c + pltpu.roll(z, half, 1) * s

    for hd in range(RET_HEADS):
        sl = slice(hd * RET_DK, (hd + 1) * RET_DK)
        q_ref[:, sl] = rot(_dot(h, w_ref[:, sl])).astype(BF16)
        ks = slice(_RET_QK + hd * RET_DK, _RET_QK + (hd + 1) * RET_DK)
        k_ref[:, sl] = (rot(_dot(h, w_ref[:, ks])) * (RET_DK ** -0.5)).astype(BF16)
    v_ref[...] = _dot(h, w_ref[:, 2 * _RET_QK:2 * _RET_QK + RET_WIDTH])
    g_ref[...] = _dot(h, w_ref[:, 2 * _RET_QK + RET_WIDTH:])


def _odd_project(x, gn, w_bf, cos, sin, tm):
    t = x.shape[0]
    ntab = cos.shape[0] // tm
    row = lambda w: pl.BlockSpec((tm, w), lambda i: (i, 0))
    tab = pl.BlockSpec((tm, RET_DK), lambda i: (i % ntab, 0))
    return pl.pallas_call(
        _odd_proj_kernel,
        grid=(t // tm,),
        in_specs=[row(D_MODEL), _full_spec(gn), _full_spec(w_bf), tab, tab],
        out_specs=(row(_RET_QK), row(_RET_QK), row(RET_WIDTH), row(RET_WIDTH)),
        out_shape=(jax.ShapeDtypeStruct((t, _RET_QK), BF16), jax.ShapeDtypeStruct((t, _RET_QK), BF16),
                   jax.ShapeDtypeStruct((t, RET_WIDTH), F32), jax.ShapeDtypeStruct((t, RET_WIDTH), F32)),
        compiler_params=_cparams(("parallel",)),
        name="odd_proj",
    )(x, gn, w_bf, cos, sin)


def _group_norm_gate(o, gn_row, g):
    mu = jnp.mean(o, axis=-1, keepdims=True)
    var = jnp.mean(jnp.square(o - mu), axis=-1, keepdims=True)
    return _silu(g) * ((o - mu) * lax.rsqrt(var + EPS) * gn_row)


def _ret_prompt_kernel(q_ref, k_ref, v_ref, g_ref, gn_ref, y_ref, sfin_ref, s_sc, *, chunk):
    ci = pl.program_id(1)

    @pl.when(ci == 0)
    def _():
        s_sc[...] = jnp.zeros(s_sc.shape, F32)

    n_col = lax.broadcasted_iota(jnp.int32, (chunk, 1), 0).astype(F32)
    diff = n_col - lax.broadcasted_iota(jnp.int32, (1, chunk), 1).astype(F32)
    for hd in range(RET_HEADS):
        lg = _RET_LOG_G[hd]
        dmat = jnp.where(diff >= 0, jnp.exp(lg * jnp.maximum(diff, 0.0)), 0.0)
        xi = jnp.exp(lg * (n_col + 1.0))
        zeta = jnp.exp(lg * (chunk - 1.0 - n_col))
        qh = q_ref[:, hd * RET_DK:(hd + 1) * RET_DK]
        kh = k_ref[:, hd * RET_DK:(hd + 1) * RET_DK]
        vh = v_ref[:, hd * RET_DV:(hd + 1) * RET_DV]
        s_prev = s_sc[hd]
        inner = _dot_nt(qh, kh) * dmat
        o = _dot(inner.astype(BF16), vh.astype(BF16)) + _dot(qh, s_prev.astype(BF16)) * xi
        kv = lax.dot_general(kh, (vh * zeta).astype(BF16), (((0,), (0,)), ((), ())), preferred_element_type=F32)
        s_sc[hd] = float(np.exp(lg * chunk)) * s_prev + kv
        y_ref[:, hd * RET_DV:(hd + 1) * RET_DV] = _group_norm_gate(
            o, gn_ref[hd:hd + 1, :], g_ref[:, hd * RET_DV:(hd + 1) * RET_DV]).astype(y_ref.dtype)

    @pl.when(ci == pl.num_programs(1) - 1)
    def _():
        sfin_ref[0] = s_sc[...]


def _retention_prompt(q, k, v, g, gn, batch, seq):
    chunk = RET_CHUNK
    nc = seq // chunk
    row = lambda w: pl.BlockSpec((chunk, w), lambda b, c: (b * nc + c, 0))
    return pl.pallas_call(
        functools.partial(_ret_prompt_kernel, chunk=chunk),
        grid=(batch, nc),
        in_specs=[row(_RET_QK), row(_RET_QK), row(RET_WIDTH), row(RET_WIDTH), _full_spec(gn)],
        out_specs=(row(RET_WIDTH),
                   pl.BlockSpec((1, RET_HEADS, RET_DK, RET_DV), lambda b, c: (b, 0, 0, 0))),
        out_shape=(jax.ShapeDtypeStruct((batch * seq, RET_WIDTH), BF16),
                   jax.ShapeDtypeStruct((batch, RET_HEADS, RET_DK, RET_DV), F32)),
        scratch_shapes=[pltpu.VMEM((RET_HEADS, RET_DK, RET_DV), F32)],
        compiler_params=_cparams(("parallel", "arbitrary")),
        name="retention_prompt",
    )(q, k, v, g, gn)


def _odd_out_kernel(x_ref, y_ref, w_ref, fn_ref, o_ref):
    x2 = x_ref[...] + _dot(y_ref[...], w_ref[...])
    o_ref[...] = _rms(x2, fn_ref[...])


def _odd_output(x, y, w_bf, fn, tm):
    t = x.shape[0]
    row = lambda w: pl.BlockSpec((tm, w), lambda i: (i, 0))
    return pl.pallas_call(
        _odd_out_kernel,
        grid=(t // tm,),
        in_specs=[row(D_MODEL), row(RET_WIDTH), _full_spec(w_bf), _full_spec(fn)],
        out_specs=row(D_MODEL),
        out_shape=jax.ShapeDtypeStruct((t, D_MODEL), F32),
        compiler_params=_cparams(("parallel",)),
        name="odd_out",
    )(x, y, w_bf, fn)


def _ret_decode_kernel(q_ref, k_ref, v_ref, g_ref, gn_ref, s_ref, y_ref, snew_ref):
    eye = (lax.broadcasted_iota(jnp.int32, (RET_DK, RET_DK), 0)
           == lax.broadcasted_iota(jnp.int32, (RET_DK, RET_DK), 1))
    for hd in range(RET_HEADS):
        gam = float(np.exp(_RET_LOG_G[hd]))
        qh = q_ref[0, :, hd * RET_DK:(hd + 1) * RET_DK]
        kh = k_ref[0, :, hd * RET_DK:(hd + 1) * RET_DK]
        vh = v_ref[0, :, hd * RET_DV:(hd + 1) * RET_DV].astype(BF16).astype(F32)
        s_prev = s_ref[0, hd]
        inner = jnp.sum(qh.astype(F32) * kh.astype(F32), axis=1, keepdims=True).astype(BF16).astype(F32)
        qs = _dot(jnp.broadcast_to(qh, (8, RET_DK)), s_prev.astype(BF16))[0:1]
        o = inner * vh + qs * gam
        k_col = jnp.sum(jnp.where(eye, jnp.broadcast_to(kh.astype(F32), (RET_DK, RET_DK)), 0.0), axis=1, keepdims=True)
        snew_ref[0, hd] = gam * s_prev + k_col * vh
        y_ref[0, :, hd * RET_DV:(hd + 1) * RET_DV] = _group_norm_gate(
            o, gn_ref[hd:hd + 1, :], g_ref[0, :, hd * RET_DV:(hd + 1) * RET_DV]).astype(y_ref.dtype)


def _retention_decode(q, k, v, g, gn, state):
    db = q.shape[0]
    r3 = lambda a: a.reshape(db, 1, a.shape[-1])
    vec = lambda w: pl.BlockSpec((1, 1, w), lambda b: (b, 0, 0))
    st = pl.BlockSpec((1, RET_HEADS, RET_DK, RET_DV), lambda b: (b, 0, 0, 0))
    y, s_new = pl.pallas_call(
        _ret_decode_kernel,
        grid=(db,),
        in_specs=[vec(_RET_QK), vec(_RET_QK), vec(RET_WIDTH), vec(RET_WIDTH), _full_spec(gn), st],
        out_specs=(vec(RET_WIDTH), st),
        out_shape=(jax.ShapeDtypeStruct((db, 1, RET_WIDTH), BF16),
                   jax.ShapeDtypeStruct(state.shape, state.dtype)),
        compiler_params=_cparams(("parallel",)),
        name="retention_decode",
    )(r3(q), r3(k), r3(v), r3(g), gn, state)
    return y.reshape(db, RET_WIDTH), s_new


PAGES_PER_STEP = 16
SUB = 8


def _mla_decode_kernel(pt_ref, q_ref, new_ref, *rest, steps):
    del pt_ref
    page_refs = rest[:PAGES_PER_STEP]
    o_ref, m_sc, l_sc, acc_sc, r0_sc, r1_sc, s0_sc, s1_sc = rest[PAGES_PER_STEP:]
    bufs = ((r0_sc, s0_sc), (r1_sc, s1_sc))
    i = pl.program_id(1)
    q = q_ref[0]

    def load(par):
        r_sc, s_sc = bufs[par]
        rows_t = jnp.concatenate([p[...] for p in page_refs], axis=1).astype(BF16)
        r_sc[...] = rows_t
        s_sc[...] = _dot(q, rows_t) * MLA_SCALE

    def proc(par):
        r_sc, s_sc = bufs[par]
        s = s_sc[...]
        m_prev = m_sc[...]
        m_new = jnp.maximum(m_prev, _row_max(s))
        alpha = jnp.exp(m_prev - m_new)
        p = jnp.exp(s - m_new)
        l_sc[...] = alpha * l_sc[...] + _row_sum(p)
        acc_sc[...] = alpha * acc_sc[...] + _dot_nt(p.astype(BF16), r_sc[0:MLA_KV_RANK, :])
        m_sc[...] = m_new

    @pl.when(i == 0)
    def _():
        m_sc[...] = jnp.full(m_sc.shape, NEG_BIG, F32)
        l_sc[...] = jnp.zeros(l_sc.shape, F32)
        acc_sc[...] = jnp.zeros(acc_sc.shape, F32)
        load(0)

    even = lax.rem(i, 2) == 0
    steady = jnp.logical_and(i > 0, i < steps)
    for par in range(2):
        @pl.when(jnp.logical_and(steady, even == (par == 0)))
        def _(par=par):
            load(par)
            proc(1 - par)

    @pl.when(i == steps)
    def _():
        proc((steps - 1) % 2)
        new = new_ref[0].astype(BF16).astype(F32)
        s_n = jnp.sum(q.astype(F32) * new, axis=1, keepdims=True) * MLA_SCALE
        m_prev = m_sc[...]
        m_new = jnp.maximum(m_prev, s_n)
        alpha = jnp.exp(m_prev - m_new)
        p_n = jnp.exp(s_n - m_new)
        l = alpha * l_sc[...] + p_n
        acc = alpha * acc_sc[...] + p_n.astype(BF16).astype(F32) * new[:, :MLA_KV_RANK]
        o_ref[0] = (acc / jnp.maximum(l, 1e-30)).astype(o_ref.dtype)


def _mla_decode(q_dec, rows_new, pool, page_table):
    db, n_pages = page_table.shape
    assert n_pages % PAGES_PER_STEP == 0
    steps = n_pages // PAGES_PER_STEP
    w = pool.shape[1]
    n_tok = PAGES_PER_STEP * PAGE_SIZE

    def page_spec(j):
        return pl.BlockSpec((None, w, PAGE_SIZE),
                            lambda b, i, pt: (pt[b * n_pages + jnp.minimum(i, steps - 1) * PAGES_PER_STEP + j], 0, 0))

    grid_spec = pltpu.PrefetchScalarGridSpec(
        num_scalar_prefetch=1,
        grid=(db, steps + 1),
        in_specs=[pl.BlockSpec((1, MLA_HEADS, w), lambda b, i, pt: (b, 0, 0)),
                  pl.BlockSpec((1, 1, w), lambda b, i, pt: (b, 0, 0))]
                 + [page_spec(j) for j in range(PAGES_PER_STEP)],
        out_specs=pl.BlockSpec((1, MLA_HEADS, MLA_KV_RANK), lambda b, i, pt: (b, 0, 0)),
        scratch_shapes=[pltpu.VMEM((MLA_HEADS, 1), F32), pltpu.VMEM((MLA_HEADS, 1), F32),
                        pltpu.VMEM((MLA_HEADS, MLA_KV_RANK), F32),
                        pltpu.VMEM((w, n_tok), BF16), pltpu.VMEM((w, n_tok), BF16),
                        pltpu.VMEM((MLA_HEADS, n_tok), F32), pltpu.VMEM((MLA_HEADS, n_tok), F32)],
    )
    return pl.pallas_call(
        functools.partial(_mla_decode_kernel, steps=steps),
        grid_spec=grid_spec,
        out_shape=jax.ShapeDtypeStruct((db, MLA_HEADS, MLA_KV_RANK), BF16),
        compiler_params=_cparams(("parallel", "arbitrary")),
        name="mla_decode",
    )(page_table.reshape(-1), q_dec, rows_new.reshape(db, 1, w), *([pool] * PAGES_PER_STEP))


def _decode_blocks(p_len, n_new):
    n_slc = -(-(p_len + n_new) // SLC_BLOCK)
    return n_slc, LANES * ((n_slc + LANES - 1) // LANES)


CMP_PAIRS = CMP_STRIDE // 2
GRP_W = 2 * NSA_DH


def _prep_cmp_pair_weights(w_cmp, pe_cmp):
    wh = w_cmp.reshape(2, 2, CMP_PAIRS, 2, NSA_DH, NSA_DH)
    big = jnp.zeros((CMP_PAIRS, 2, 2, NSA_DH, 2, 2, NSA_DH), F32)
    for c in range(2):
        big = big.at[:, :, c, :, :, c, :].set(jnp.transpose(wh[c], (1, 2, 3, 0, 4)))
    w_pair = big.reshape(CMP_PAIRS, 2 * GRP_W, 2 * GRP_W).astype(BF16)
    peh = pe_cmp.reshape(2, 2, CMP_PAIRS, 2, NSA_DH)
    pe_pair = jnp.transpose(peh, (1, 2, 3, 0, 4)).reshape(2, CMP_PAIRS, 1, 2 * GRP_W)
    return w_pair, pe_pair


def _cmp_bias_kernel(pe_ref, w_ref, b_ref):
    for half in range(2):
        acc = jnp.zeros((SUB, GRP_W), F32)
        for jp in range(CMP_PAIRS):
            w = w_ref[jp][:, half * GRP_W:(half + 1) * GRP_W]
            for t in _split3(jnp.broadcast_to(pe_ref[half, jp], (SUB, 2 * GRP_W))):
                acc = acc + _dot(t, w)
        b_ref[half] = acc


def _cmp_bias(w_pair, pe_pair):
    return pl.pallas_call(
        _cmp_bias_kernel,
        grid=(1,),
        in_specs=[_full_spec(pe_pair), _full_spec(w_pair)],
        out_specs=pl.BlockSpec((2, SUB, GRP_W), lambda i: (0, 0, 0)),
        out_shape=jax.ShapeDtypeStruct((2, SUB, GRP_W), F32),
        compiler_params=_cparams(("arbitrary",)),
        name="nsa_cmp_bias",
    )(pe_pair, w_pair)


def _cmp_decode_kernel(pt_ref, qn_ref, cnew_ref, bias_ref, w_ref, amat_ref, *rest, p_len, steps):
    del pt_ref
    page_refs = rest[:PAGES_PER_STEP]
    oc_ref, sslc_ref, xa_sc, xb_sc, h_sc = rest[PAGES_PER_STEP:]
    x_bufs = (xa_sc, xb_sc)
    i = pl.program_id(1)
    n_tok = PAGES_PER_STEP * PAGE_SIZE
    n_step = n_tok // CMP_STRIDE
    nseg = h_sc.shape[1]

    def load(par):
        x_sc = x_bufs[par]
        for j, p in enumerate(page_refs):
            for g in range(NSA_KV_HEADS):
                x_sc[g, j * PAGE_SIZE:(j + 1) * PAGE_SIZE, :] = p[g * GRP_W:(g + 1) * GRP_W, :].T

    def compress(par, step):
        x_sc = x_bufs[par]
        r0 = pl.multiple_of(step * n_step, n_step)
        for g in range(NSA_KV_HEADS):
            acc = jnp.zeros((n_step, 2 * GRP_W), F32)
            for jp in range(CMP_PAIRS):
                rows = [x_sc[g, pl.ds(2 * jp + jj, n_step, stride=CMP_STRIDE), :] for jj in range(2)]
                acc = acc + _dot(jnp.concatenate(rows, axis=1).astype(BF16), w_ref[jp])
            h_sc[g, pl.ds(r0, n_step), :] = acc

    @pl.when(i == 0)
    def _():
        load(0)

    even = lax.rem(i, 2) == 0
    steady = jnp.logical_and(i > 0, i < steps)
    for par in range(2):
        @pl.when(jnp.logical_and(steady, even == (par == 0)))
        def _(par=par):
            load(par)
            compress(1 - par, i - 1)

    @pl.when(i == steps)
    def _():
        compress((steps - 1) % 2, steps - 1)

    @pl.when(i == steps)
    def _():
        qpos = p_len
        cpos = lax.broadcasted_iota(jnp.int32, (1, nseg), 1) * CMP_STRIDE + (CMP_BLOCK - 1)
        dist = (qpos - cpos).astype(F32)
        row = lax.broadcasted_iota(jnp.int32, (SUB, 1), 0)
        seg = lax.broadcasted_iota(jnp.int32, (nseg, 1), 0)
        amat = amat_ref[...]
        b1 = bias_ref[0, 0:1, :]
        b2 = bias_ref[1, 0:1, :]
        for g in range(NSA_KV_HEADS):
            h = h_sc[g]
            new_seg = jnp.concatenate([cnew_ref[0, :, g * GRP_W:(g + 1) * GRP_W], jnp.zeros((1, GRP_W), F32)], axis=1)
            h2n = _dot(jnp.broadcast_to(new_seg, (SUB, 2 * GRP_W)).astype(BF16), w_ref[0])[0:1, GRP_W:] + b2
            h2 = jnp.where(seg == nseg - 1, h2n, pltpu.roll(h[:, GRP_W:], nseg - 1, 0) + b2)
            kvc = (h[:, :GRP_W] + b1 + h2).astype(BF16)
            slope = jnp.zeros((SUB, 1), F32)
            for rr in range(NSA_GROUP):
                slope = jnp.where(row == rr, 2.0 ** (-(g * NSA_GROUP + rr + 1.0)), slope)
            qg = qn_ref[0, g]
            p_c = _softmax_rows(_dot_nt(qg, kvc) - slope * dist, dist >= 0)
            p_c = jnp.where(row < NSA_GROUP, p_c, 0.0)
            oc_ref[0, g] = pltpu.roll(_dot(p_c.astype(BF16), kvc), LANES - NSA_DH, 1)
            s_rows = sum(_dot(t, amat) for t in _split3(p_c))
            sslc_ref[0, g] = jnp.broadcast_to(jnp.sum(s_rows, axis=0, keepdims=True), s_rows.shape)


def _cmp_decode(qn_dec, cmp_new, pool_t, page_table, w_pair, bias):
    db, n_pages = page_table.shape
    steps = n_pages // PAGES_PER_STEP
    p_len = n_pages * PAGE_SIZE
    nseg = p_len // CMP_STRIDE
    n_slc, nl = _decode_blocks(p_len, 1)
    amat = jnp.asarray(_slc_overlap_matrix(nseg, nl), BF16)

    def page_spec(j):
        return pl.BlockSpec((None, NSA_KV_W, PAGE_SIZE),
                            lambda b, i, pt: (pt[b * n_pages + jnp.minimum(i, steps - 1) * PAGES_PER_STEP + j], 0, 0))

    grid_spec = pltpu.PrefetchScalarGridSpec(
        num_scalar_prefetch=1,
        grid=(db, steps + 1),
        in_specs=[pl.BlockSpec((1, NSA_KV_HEADS, SUB, LANES), lambda b, i, pt: (b, 0, 0, 0)),
                  pl.BlockSpec((1, 1, NSA_KV_W), lambda b, i, pt: (b, 0, 0)),
                  pl.BlockSpec(bias.shape, lambda b, i, pt: (0, 0, 0)),
                  pl.BlockSpec(w_pair.shape, lambda b, i, pt: (0, 0, 0)),
                  pl.BlockSpec(amat.shape, lambda b, i, pt: (0, 0))]
                 + [page_spec(j) for j in range(PAGES_PER_STEP)],
        out_specs=(pl.BlockSpec((1, NSA_KV_HEADS, SUB, LANES), lambda b, i, pt: (b, 0, 0, 0)),
                   pl.BlockSpec((1, NSA_KV_HEADS, SUB, nl), lambda b, i, pt: (b, 0, 0, 0))),
        scratch_shapes=[pltpu.VMEM((NSA_KV_HEADS, PAGES_PER_STEP * PAGE_SIZE, GRP_W), F32),
                        pltpu.VMEM((NSA_KV_HEADS, PAGES_PER_STEP * PAGE_SIZE, GRP_W), F32),
                        pltpu.VMEM((NSA_KV_HEADS, nseg, 2 * GRP_W), F32)],
    )
    return pl.pallas_call(
        functools.partial(_cmp_decode_kernel, p_len=p_len, steps=steps),
        grid_spec=grid_spec,
        out_shape=(jax.ShapeDtypeStruct((db, NSA_KV_HEADS, SUB, LANES), F32),
                   jax.ShapeDtypeStruct((db, NSA_KV_HEADS, SUB, nl), F32)),
        compiler_params=_cparams(("parallel", "arbitrary")),
        name="nsa_cmp_decode",
    )(page_table.reshape(-1), qn_dec, cmp_new.reshape(db, 1, NSA_KV_W), bias, w_pair, amat,
      *([pool_t] * PAGES_PER_STEP))


def _topk_decode_kernel(s_ref, idx_ref, *, qpos):
    s_slc = s_ref[...]
    blk = lax.broadcasted_iota(jnp.int32, s_slc.shape, 1)
    cur = qpos // SLC_BLOCK
    valid = blk * SLC_BLOCK <= qpos
    forced = jnp.logical_or(blk == 0, jnp.logical_or(blk == cur, blk == cur - 1))
    score = jnp.where(forced, FORCE_SCORE, jnp.where(valid, s_slc, NEG_BIG))
    lane = lax.broadcasted_iota(jnp.int32, idx_ref.shape, 1)
    out = jnp.full(idx_ref.shape, -1, jnp.int32)
    for t in range(N_SELECT):
        m = jnp.max(score, axis=1, keepdims=True)
        idx = jnp.min(jnp.where(score == m, blk, score.shape[1]), axis=1, keepdims=True)
        out = jnp.where(lane == t, jnp.where(m > 0.5 * NEG_BIG, idx, -1), out)
        score = jnp.where(blk == idx, 2.0 * NEG_BIG, score)
    idx_ref[...] = out


def _topk_decode(s_slc, qpos):
    rows = s_slc.shape[0]
    return pl.pallas_call(
        functools.partial(_topk_decode_kernel, qpos=qpos),
        grid=(1,),
        in_specs=[_full_spec(s_slc)],
        out_specs=pl.BlockSpec((rows, LANES), lambda i: (0, 0)),
        out_shape=jax.ShapeDtypeStruct((rows, LANES), jnp.int32),
        compiler_params=_cparams(("arbitrary",)),
        name="nsa_topk_decode",
    )(s_slc)


def _sel_decode_kernel(pt_ref, sel_ref, qn_ref, gl_ref, oc_ref, snew_ref, wnew_ref, wnewt_ref, win_ref, *rest, p_len):
    del pt_ref
    nb = NSA_KV_HEADS * N_SELECT
    blk_refs = rest[:nb]
    o_ref, nwin_ref = rest[nb:]
    b = pl.program_id(0)
    qpos = p_len
    n_past_blocks = p_len // SLC_BLOCK
    per_page = PAGE_SIZE // SLC_BLOCK
    row = lax.broadcasted_iota(jnp.int32, (SUB, 1), 0)
    lane = lax.broadcasted_iota(jnp.int32, (SUB, LANES), 1)
    gates = jnp.broadcast_to(jax.nn.sigmoid(gl_ref[0]), (SUB, LANES))
    nw = win_ref.shape[2]
    win = win_ref[0]
    wnew = wnew_ref[0]
    snew = snew_ref[0]

    def merge_new(s, mask, kv_t, q8, new_row, slope, use_new):
        new_f = new_row.astype(BF16).astype(F32)
        s_n = jnp.sum(q8.astype(F32) * new_f, axis=1, keepdims=True)
        s = jnp.where(mask, s, NEG_BIG)
        m = jnp.maximum(_row_max(s), jnp.where(use_new, s_n, NEG_BIG))
        e = jnp.where(mask, jnp.exp(s - m), 0.0)
        e_n = jnp.where(use_new, jnp.exp(s_n - m), 0.0)
        denom = jnp.maximum(_row_sum(e) + e_n, 1e-30)
        p = e / denom
        p_n = e_n / denom
        o = _dot_nt(p.astype(BF16), kv_t) + p_n.astype(BF16).astype(F32) * new_f
        return pltpu.roll(o, LANES - NSA_DH, 1)

    for g in range(NSA_KV_HEADS):
        slope = jnp.zeros((SUB, 1), F32)
        for rr in range(NSA_GROUP):
            slope = jnp.where(row == rr, 2.0 ** (-(g * NSA_GROUP + rr + 1.0)), slope)
        q8 = qn_ref[0, g]
        kv_t = jnp.concatenate([r[...] for r in blk_refs[g * N_SELECT:(g + 1) * N_SELECT]], axis=1).astype(BF16)
        nk = N_SELECT * PAGE_SIZE
        key = lax.broadcasted_iota(jnp.int32, (1, nk), 1)
        slot = lax.shift_right_logical(key, int(np.log2(PAGE_SIZE)))
        blk_in_page = jnp.bitwise_and(lax.shift_right_logical(key, int(np.log2(SLC_BLOCK))), per_page - 1)
        off = jnp.bitwise_and(key, SLC_BLOCK - 1)
        kblk = jnp.zeros((1, nk), jnp.int32)
        use_new = False
        for t in range(N_SELECT):
            st = sel_ref[(b * NSA_KV_HEADS + g) * N_SELECT + t]
            kblk = jnp.where(slot == t, st, kblk)
            use_new = jnp.logical_or(use_new, st == n_past_blocks)
        kpos = kblk * SLC_BLOCK + off
        ok = jnp.logical_and(jnp.logical_and(kblk >= 0, kblk < n_past_blocks),
                             jnp.logical_and(blk_in_page == jnp.bitwise_and(kblk, per_page - 1), kpos <= qpos))
        dist = (qpos - kpos).astype(F32)
        o_s = merge_new(_dot(q8, kv_t) - slope * dist, ok, kv_t, q8, snew[:, g * LANES:(g + 1) * LANES], slope, use_new)
        kvw_t = win[g * LANES:(g + 1) * LANES, :].astype(BF16)
        wpos = p_len - nw + lax.broadcasted_iota(jnp.int32, (1, nw), 1)
        dist_w = (qpos - wpos).astype(F32)
        mask_w = jnp.logical_and(jnp.logical_and(dist_w >= 0, dist_w <= WINDOW), wpos >= 0)
        o_w = merge_new(_dot(q8, kvw_t) - slope * dist_w, mask_w, kvw_t, q8, wnew[:, g * LANES:(g + 1) * LANES], slope, True)
        hd = g * NSA_GROUP + row
        gate = lambda c: jnp.sum(jnp.where(lane == 3 * hd + c, gates, 0.0), axis=1, keepdims=True)
        mix = gate(0) * oc_ref[0, g] + gate(1) * o_s + gate(2) * o_w
        o_ref[0, g * NSA_GROUP:(g + 1) * NSA_GROUP, :] = mix[0:NSA_GROUP]
    req = lax.broadcasted_iota(jnp.int32, wnewt_ref.shape, 1)
    col = jnp.sum(jnp.where(req == b, wnewt_ref[...], 0.0), axis=1, keepdims=True)
    t_idx = lax.broadcasted_iota(jnp.int32, (1, nw), 1)
    nwin_ref[0] = jnp.where(t_idx == nw - 1, col, pltpu.roll(win, nw - 1, 1))


def _sel_decode(qn_dec, gl, o_c, sel_idx, slc_new, win_new, slc_pool_t, win_buf_t, page_table):
    db, n_pages = page_table.shape
    p_len = n_pages * PAGE_SIZE
    n_past_blocks = p_len // SLC_BLOCK
    per_page = PAGE_SIZE // SLC_BLOCK
    nw = win_buf_t.shape[2]
    win_new_t = jnp.transpose(win_new)

    def blk_spec(g, t):
        def imap(b, pt, sel):
            j = jnp.clip(sel[(b * NSA_KV_HEADS + g) * N_SELECT + t], 0, n_past_blocks - 1)
            return (pt[b * n_pages + j // per_page], g, 0)
        return pl.BlockSpec((None, GRP_W, PAGE_SIZE), imap)

    vec = lambda w: pl.BlockSpec((1, 1, w), lambda b, pt, sel: (b, 0, 0))
    grp = pl.BlockSpec((1, NSA_KV_HEADS, SUB, LANES), lambda b, pt, sel: (b, 0, 0, 0))
    grid_spec = pltpu.PrefetchScalarGridSpec(
        num_scalar_prefetch=2,
        grid=(db,),
        in_specs=[grp, vec(LANES), grp, vec(NSA_KV_W), vec(NSA_KV_W),
                  pl.BlockSpec(win_new_t.shape, lambda b, pt, sel: (0, 0)),
                  pl.BlockSpec((1, NSA_KV_W, nw), lambda b, pt, sel: (b, 0, 0))]
                 + [blk_spec(g, t) for g in range(NSA_KV_HEADS) for t in range(N_SELECT)],
        out_specs=(pl.BlockSpec((1, NSA_HEADS, LANES), lambda b, pt, sel: (b, 0, 0)),
                   pl.BlockSpec((1, NSA_KV_W, nw), lambda b, pt, sel: (b, 0, 0))),
    )
    r3 = lambda a: a.reshape(db, 1, a.shape[-1])
    return pl.pallas_call(
        functools.partial(_sel_decode_kernel, p_len=p_len),
        grid_spec=grid_spec,
        out_shape=(jax.ShapeDtypeStruct((db, NSA_HEADS, LANES), F32),
                   jax.ShapeDtypeStruct(win_buf_t.shape, win_buf_t.dtype)),
        compiler_params=_cparams(("parallel",)),
        name="nsa_sel_decode",
    )(page_table.reshape(-1), sel_idx, qn_dec, r3(gl), o_c, r3(slc_new), r3(win_new), win_new_t, win_buf_t,
      *([slc_pool_t] * (NSA_KV_HEADS * N_SELECT)))


TM_PROMPT = 256
TQ_MLA, TK_MLA = 256, 512
TQ_NSA, TK_NSA = 128, 512


def _mla_rows(rows):
    return jnp.concatenate([rows[:, :MLA_KV_RANK], rows[:, MLA_KV_RANK:MLA_KV_RANK + HALF_ROPE],
                            rows[:, MLA_KV_RANK + LANES:MLA_KV_RANK + LANES + HALF_ROPE]], axis=1)


def kernel(x_prompt, x_sample, cache_mla, cache_nsa_cmp, cache_nsa_slc, state_nsa_win, state_ret, page_table,
           norm_even, w_in_even, mla_gq, mla_gkv, mla_wuq, mla_wuk, mla_wuv, nsa_cmp_w, nsa_cmp_pe, w_out_even,
           norm_odd, w_in_odd, ret_gn, w_out_odd, final_norm):
    b, s, d = x_prompt.shape
    db, n_new, _ = x_sample.shape
    assert n_new == 1 and norm_even.shape[0] == 1 and norm_odd.shape[0] == 1
    n_pages = page_table.shape[1]
    p_len = n_pages * PAGE_SIZE
    kv_row = (NSA_KV_HEADS, 2, NSA_DH)
    tm = min(TM_PROMPT, b * s)

    w_ext, wuq_ext, wuk_ext, wuv_ext, wo_ext = _prep_even_weights(
        w_in_even[0], mla_wuq[0], mla_wuk[0], mla_wuv[0], w_out_even[0])
    w_big, pe_big = _prep_cmp_weights(nsa_cmp_w[0], nsa_cmp_pe[0])
    w_odd = w_in_odd[0].astype(BF16)
    wo_odd = w_out_odd[0].astype(BF16)
    gn_e, gq, gkv = norm_even[0][None], mla_gq[0][None], mla_gkv[0][None]
    gn_o, fn = norm_odd[0][None], final_norm[None]

    xp = x_prompt.reshape(b * s, d)
    pos_p = jnp.arange(s)
    cos_m, sin_m = _rope_tables_mla(pos_p)
    (qext, rows, kext, gmla, qn, cmp, slc, slcp, win, winp, gl, gnsa) = _even_project(
        xp, gn_e, w_ext, gq, gkv, wuq_ext, wuk_ext, cos_m, sin_m, tm)
    lat = _mla_prompt_attention(qext, kext, b, s, min(TQ_MLA, s), min(TK_MLA, s))
    kvc = _cmp_prompt(cmp, w_big, pe_big, b, s)
    onsa = _nsa_prompt(qn, gl, kvc, slcp, winp, b, s, TQ_NSA, min(TK_NSA, s))
    x1 = _even_output(xp, lat, gmla, onsa, gnsa, wuv_ext, wo_ext, tm)
    cos_r, sin_r = _rope_tables_ret(pos_p)
    q, k, v, g = _odd_project(x1, gn_o, w_odd, cos_r, sin_r, tm)
    y, ret_p = _retention_prompt(q, k, v, g, ret_gn[0], b, s)
    y_prompt = _odd_output(x1, y, wo_odd, fn, tm).reshape(b, s, d)
    nwin = min(WINDOW, s)
    mla_p = _mla_rows(rows).reshape(1, b, s, MLA_KV_RANK + MLA_ROPE)
    cmp_p = cmp.reshape((1, b, s) + kv_row)
    slc_p = slc.reshape((1, b, s) + kv_row)
    win_p = win.reshape((b, s) + kv_row)[:, s - nwin:][None]

    xs = x_sample.reshape(db, d)
    pos_s = p_len + jnp.arange(n_new)
    cos_s, sin_s = [jnp.broadcast_to(t, (db, LANES)) for t in _rope_tables_mla(pos_s)]
    (qext_s, rows_s, _, gmla_s, qn_s, cmp_s, slc_s, _, win_s, _, gl_s, gnsa_s) = _even_project(
        xs, gn_e, w_ext, gq, gkv, wuq_ext, wuk_ext, cos_s, sin_s, db)
    rows_new = _mla_rows(rows_s)
    rope = lambda a: a.astype(F32).reshape(MLA_HEADS, db, MLA_HEADS, HALF_ROPE).sum(2).astype(BF16)
    q_dec = jnp.concatenate([qext_s[:, :, :MLA_KV_RANK], rope(qext_s[:, :, MLA_KV_RANK:MLA_KV_RANK + LANES]),
                             rope(qext_s[:, :, MLA_KV_RANK + LANES:])], axis=-1)
    pool = cache_nsa_cmp.shape[1]
    feat_major = lambda a, n: jnp.transpose(a[0], (0, 2, 3, 4, 1)).reshape(n, NSA_KV_W, a.shape[2])
    mla_pool_t = jnp.transpose(cache_mla[0], (0, 2, 1))
    lat_s = _mla_decode(jnp.transpose(q_dec, (1, 0, 2)), rows_new, mla_pool_t, page_table)
    qn_dec = jnp.pad(qn_s.reshape(db, NSA_KV_HEADS, NSA_GROUP, LANES), ((0, 0), (0, 0), (0, SUB - NSA_GROUP), (0, 0)))
    w_pair, pe_pair = _prep_cmp_pair_weights(nsa_cmp_w[0], nsa_cmp_pe[0])
    o_c, s_slc = _cmp_decode(qn_dec, cmp_s, feat_major(cache_nsa_cmp, pool), page_table, w_pair,
                             _cmp_bias(w_pair, pe_pair))
    sel = _topk_decode(s_slc[:, :, 0, :].reshape(db * NSA_KV_HEADS, -1), p_len)
    sel_idx = sel[:, :N_SELECT].reshape(-1)
    onsa_s, win_new_t = _sel_decode(qn_dec, gl_s, o_c, sel_idx, slc_s, win_s, feat_major(cache_nsa_slc, pool),
                                    feat_major(state_nsa_win, db), page_table)
    nw = win_new_t.shape[2]
    win_so = jnp.transpose(win_new_t.reshape((db,) + kv_row + (nw,)), (0, 4, 1, 2, 3))[None]
    x1s = _even_output(xs, jnp.transpose(lat_s, (1, 0, 2)), gmla_s, onsa_s.reshape(db, NSA_HEADS * LANES), gnsa_s,
                       wuv_ext, wo_ext, db)
    cos_rs, sin_rs = [jnp.broadcast_to(t, (db, RET_DK)) for t in _rope_tables_ret(pos_s)]
    q, k, v, g = _odd_project(x1s, gn_o, w_odd, cos_rs, sin_rs, db)
    ys, ret_s = _retention_decode(q, k, v, g, ret_gn[0], state_ret.reshape(state_ret.shape[1:]))
    y_sample = _odd_output(x1s, ys, wo_odd, fn, db).reshape(db, n_new, d)
    mla_s = rows_new.reshape(1, db, n_new, MLA_KV_RANK + MLA_ROPE)
    cmp_so = cmp_s.reshape((1, db, n_new) + kv_row)
    slc_so = slc_s.reshape((1, db, n_new) + kv_row)
    return (y_prompt, y_sample, mla_p, cmp_p, slc_p, win_p, ret_p[None],
            mla_s, cmp_so, slc_so, win_so, ret_s[None])
```

```python
import functools

import numpy as np
import jax
import jax.numpy as jnp
from jax import lax
from jax.experimental import pallas as pl
from jax.experimental.pallas import tpu as pltpu

F32 = jnp.float32
BF16 = jnp.bfloat16

D_MODEL = 1024
PAGE_SIZE = 128
MLA_HEADS = 8
MLA_NOPE = 64
MLA_ROPE = 32
MLA_V = 64
MLA_Q_RANK = 768
MLA_KV_RANK = 256
MLA_WIDTH = MLA_HEADS * MLA_V
MLA_SCALE = (MLA_NOPE + MLA_ROPE) ** -0.5
NSA_HEADS = 8
NSA_KV_HEADS = 2
NSA_GROUP = NSA_HEADS // NSA_KV_HEADS
NSA_DH = 64
NSA_WIDTH = NSA_HEADS * NSA_DH
NSA_KV_W = NSA_KV_HEADS * 2 * NSA_DH
CMP_BLOCK = 32
CMP_STRIDE = 16
SLC_BLOCK = 64
N_SELECT = 16
WINDOW = 512
FORCE_SCORE = 1e4
RET_HEADS = 8
RET_DK = 128
RET_DV = 256
RET_WIDTH = RET_HEADS * RET_DV
RET_CHUNK = 128
ROPE_BASE = 10000.0
EPS = 1e-6
EVEN_SPLITS = (MLA_Q_RANK, MLA_KV_RANK, MLA_ROPE, MLA_WIDTH, NSA_WIDTH, NSA_KV_W, NSA_KV_W, NSA_KV_W,
               3 * NSA_HEADS, NSA_WIDTH)

LANES = 128
VMEM_LIMIT_BYTES = 56 * 1024 * 1024
NEG_BIG = -1e30
MASK_BIG = 16384.0

HALF_ROPE = MLA_ROPE // 2


def _cparams(sem):
    return pltpu.CompilerParams(dimension_semantics=sem, vmem_limit_bytes=VMEM_LIMIT_BYTES)


def _full_spec(a):
    nd = a.ndim
    return pl.BlockSpec(a.shape, lambda *_: (0,) * nd)


def _rms(x, g):
    y = x * lax.rsqrt(jnp.mean(x * x, axis=-1, keepdims=True) + EPS)
    return y * g


def _dot(a, b):
    return jnp.dot(a, b, preferred_element_type=F32)


def _dot_nt(a, b):
    return lax.dot_general(a, b, (((1,), (1,)), ((), ())), preferred_element_type=F32)


def _fold_lanes(x, op):
    n = x.shape[1]
    if n % LANES:
        return x
    parts = [x[:, i:i + LANES] for i in range(0, n, LANES)]
    while len(parts) > 1:
        parts = [op(parts[i], parts[i + 1]) if i + 1 < len(parts) else parts[i] for i in range(0, len(parts), 2)]
    return parts[0]


def _row_max(x):
    return jnp.max(_fold_lanes(x, jnp.maximum), axis=1, keepdims=True)


def _row_sum(x):
    return jnp.sum(_fold_lanes(x, jnp.add), axis=1, keepdims=True)


def _rep(col):
    return jnp.broadcast_to(col, (col.shape[0], LANES))


def _tile_lanes(x, n):
    if n % LANES:
        return jnp.broadcast_to(x[:, :1], (x.shape[0], n))
    return x if n == LANES else jnp.tile(x, (1, n // LANES))


_EVEN_GROUPS = (
    ("cq", MLA_Q_RANK), ("ckv", MLA_KV_RANK), ("kr1", LANES), ("kr2", LANES),
    ("gmla", MLA_HEADS * LANES), ("qn", NSA_HEADS * LANES), ("cmp", NSA_KV_W),
    ("slc", NSA_KV_W), ("slcp", 4 * LANES), ("win", NSA_KV_W), ("winp", 4 * LANES),
    ("gl", LANES), ("gnsa", NSA_HEADS * LANES),
)
_EVEN_OFF = {}
_o = 0
for _n, _w in _EVEN_GROUPS:
    _EVEN_OFF[_n] = (_o, _o + _w)
    _o += _w
EVEN_EXT = _o


def _pad_heads(w, nh, dh):
    k = w.shape[0]
    w = w.reshape(k, nh, dh)
    return jnp.pad(w, ((0, 0), (0, 0), (0, LANES - dh))).reshape(k, nh * LANES)


def _kv_pad(w):
    k = w.shape[0]
    w4 = w.reshape(k, NSA_KV_HEADS, 2, NSA_DH)
    w4 = jnp.transpose(w4, (0, 2, 1, 3))
    return jnp.pad(w4, ((0, 0), (0, 0), (0, 0), (0, LANES - NSA_DH))).reshape(k, 4 * LANES)


def _prep_even_weights(w_in, wuq, wuk, wuv, w_out):
    offs = np.cumsum((0,) + EVEN_SPLITS)
    cq, ckv, kr, g_mla, q_nsa, kv_cmp, kv_slc, kv_win, gl, g_nsa = [
        w_in[:, offs[i]:offs[i + 1]] for i in range(len(EVEN_SPLITS))]
    parts = {
        "cq": cq, "ckv": ckv,
        "kr1": jnp.tile(kr[:, :HALF_ROPE], (1, MLA_HEADS)),
        "kr2": jnp.tile(kr[:, HALF_ROPE:], (1, MLA_HEADS)),
        "gmla": _pad_heads(g_mla, MLA_HEADS, MLA_V),
        "qn": _pad_heads(q_nsa, NSA_HEADS, NSA_DH),
        "cmp": kv_cmp, "slc": kv_slc, "slcp": _kv_pad(kv_slc),
        "win": kv_win, "winp": _kv_pad(kv_win),
        "gl": jnp.pad(gl, ((0, 0), (0, LANES - gl.shape[1]))),
        "gnsa": _pad_heads(g_nsa, NSA_HEADS, NSA_DH),
    }
    w_ext = jnp.concatenate([parts[n] for n, _ in _EVEN_GROUPS], axis=1).astype(BF16)
    c = wuq.shape[0]
    nope = jnp.pad(wuq[:, :, :MLA_NOPE], ((0, 0), (0, 0), (0, LANES - MLA_NOPE))).reshape(c, MLA_HEADS * LANES)
    r1 = wuq[:, :, MLA_NOPE:MLA_NOPE + HALF_ROPE].reshape(c, MLA_HEADS * HALF_ROPE)
    r2 = wuq[:, :, MLA_NOPE + HALF_ROPE:].reshape(c, MLA_HEADS * HALF_ROPE)
    wuq_ext = jnp.concatenate([nope, r1, r2], axis=1).astype(BF16)
    wuk_ext = jnp.pad(jnp.transpose(wuk, (1, 2, 0)), ((0, 0), (0, LANES - MLA_NOPE), (0, 0))).astype(BF16)
    wuv_ext = jnp.pad(jnp.transpose(wuv, (1, 0, 2)), ((0, 0), (0, 0), (0, LANES - MLA_V))).astype(BF16)
    d = w_out.shape[1]
    wo = w_out.reshape(MLA_HEADS + NSA_HEADS, MLA_V, d)
    wo_ext = jnp.pad(wo, ((0, 0), (0, LANES - MLA_V), (0, 0))).reshape((MLA_HEADS + NSA_HEADS) * LANES, d).astype(BF16)
    return w_ext, wuq_ext, wuk_ext, wuv_ext, wo_ext


def _rope_tables_mla(pos):
    inv = ROPE_BASE ** (-jnp.arange(HALF_ROPE, dtype=F32) / HALF_ROPE)
    ang = pos.astype(F32)[:, None] * inv
    return jnp.tile(jnp.cos(ang), (1, LANES // HALF_ROPE)), jnp.tile(jnp.sin(ang), (1, LANES // HALF_ROPE))


def _even_proj_kernel(x_ref, gn_ref, w_ref, gq_ref, gkv_ref, wuq_ref, wuk_ref, cos_ref, sin_ref,
                      qext_ref, rows_ref, kext_ref, gmla_ref, qn_ref, cmp_ref, slc_ref, slcp_ref,
                      win_ref, winp_ref, gl_ref, gnsa_ref):
    h = _rms(x_ref[...], gn_ref[...]).astype(BF16)

    def proj(name):
        a, b = _EVEN_OFF[name]
        return _dot(h, w_ref[:, a:b])

    cos = cos_ref[...]
    sin = sin_ref[...]
    cqn = _rms(proj("cq"), gq_ref[...]).astype(BF16)
    nh = MLA_HEADS * LANES
    r1 = _dot(cqn, wuq_ref[:, nh:nh + LANES])
    r2 = _dot(cqn, wuq_ref[:, nh + LANES:nh + 2 * LANES])
    o1 = r1 * cos - r2 * sin
    o2 = r1 * sin + r2 * cos
    head_of_lane = lax.broadcasted_iota(jnp.int32, (1, LANES), 1) // HALF_ROPE
    for hd in range(MLA_HEADS):
        nope = _dot(cqn, wuq_ref[:, hd * LANES:(hd + 1) * LANES]).astype(BF16)
        qext_ref[hd, :, 0:MLA_KV_RANK] = _dot(nope, wuk_ref[hd]).astype(BF16)
        sel = head_of_lane == hd
        qext_ref[hd, :, MLA_KV_RANK:MLA_KV_RANK + LANES] = jnp.where(sel, o1, 0.0).astype(BF16)
        qext_ref[hd, :, MLA_KV_RANK + LANES:] = jnp.where(sel, o2, 0.0).astype(BF16)
    latn = _rms(proj("ckv"), gkv_ref[...])
    kr1 = proj("kr1")
    kr2 = proj("kr2")
    k1 = kr1 * cos - kr2 * sin
    k2 = kr1 * sin + kr2 * cos
    rows_ref[:, 0:MLA_KV_RANK] = latn
    rows_ref[:, MLA_KV_RANK:MLA_KV_RANK + LANES] = k1
    rows_ref[:, MLA_KV_RANK + LANES:] = k2
    kext_ref[:, 0:MLA_KV_RANK] = latn.astype(BF16)
    kext_ref[:, MLA_KV_RANK:MLA_KV_RANK + LANES] = k1.astype(BF16)
    kext_ref[:, MLA_KV_RANK + LANES:] = k2.astype(BF16)
    gmla_ref[...] = proj("gmla")
    qn_ref[...] = (proj("qn") * (NSA_DH ** -0.5)).astype(BF16)
    cmp_ref[...] = proj("cmp")
    slc_ref[...] = proj("slc")
    slcp_ref[...] = proj("slcp").astype(BF16)
    win_ref[...] = proj("win")
    winp_ref[...] = proj("winp").astype(BF16)
    gl_ref[...] = proj("gl")
    gnsa_ref[...] = proj("gnsa")


def _even_project(x, gn, w_ext, gq, gkv, wuq_ext, wuk_ext, cos, sin, tm):
    t = x.shape[0]
    nt = t // tm
    ntab = cos.shape[0] // tm
    row = lambda w: pl.BlockSpec((tm, w), lambda i: (i, 0))
    tab = pl.BlockSpec((tm, LANES), lambda i: (i % ntab, 0))
    kext_w = MLA_KV_RANK + 2 * LANES
    out_shapes = (
        jax.ShapeDtypeStruct((MLA_HEADS, t, kext_w), BF16),
        jax.ShapeDtypeStruct((t, kext_w), F32),
        jax.ShapeDtypeStruct((t, kext_w), BF16),
        jax.ShapeDtypeStruct((t, MLA_HEADS * LANES), F32),
        jax.ShapeDtypeStruct((t, NSA_HEADS * LANES), BF16),
        jax.ShapeDtypeStruct((t, NSA_KV_W), F32),
        jax.ShapeDtypeStruct((t, NSA_KV_W), F32),
        jax.ShapeDtypeStruct((t, 4 * LANES), BF16),
        jax.ShapeDtypeStruct((t, NSA_KV_W), F32),
        jax.ShapeDtypeStruct((t, 4 * LANES), BF16),
        jax.ShapeDtypeStruct((t, LANES), F32),
        jax.ShapeDtypeStruct((t, NSA_HEADS * LANES), F32),
    )
    out_specs = (
        pl.BlockSpec((MLA_HEADS, tm, kext_w), lambda i: (0, i, 0)),
        row(kext_w), row(kext_w), row(MLA_HEADS * LANES), row(NSA_HEADS * LANES), row(NSA_KV_W),
        row(NSA_KV_W), row(4 * LANES), row(NSA_KV_W), row(4 * LANES), row(LANES), row(NSA_HEADS * LANES),
    )
    return pl.pallas_call(
        _even_proj_kernel,
        grid=(nt,),
        in_specs=[row(D_MODEL), _full_spec(gn), _full_spec(w_ext), _full_spec(gq), _full_spec(gkv),
                  _full_spec(wuq_ext), _full_spec(wuk_ext), tab, tab],
        out_specs=out_specs,
        out_shape=out_shapes,
        compiler_params=_cparams(("parallel",)),
        name="even_proj",
    )(x, gn, w_ext, gq, gkv, wuq_ext, wuk_ext, cos, sin)


LOG2E = float(np.log2(np.e))


def _mla_attn_kernel(q_ref, k_ref, kn_ref, o_ref, m_sc, l_sc, acc_sc, sa_sc, sb_sc, *, tq, tk):
    qi = pl.program_id(1)
    ki = pl.program_id(2)
    n_needed = ((qi + 1) * tq + tk - 1) // tk
    m_rows = MLA_HEADS * tq
    c_exp = MLA_SCALE * LOG2E

    def scores(kref):
        return _dot_nt(q_ref[...].reshape(m_rows, q_ref.shape[-1]), kref[...])

    @pl.when(ki == 0)
    def _():
        m_sc[...] = jnp.full(m_sc.shape, NEG_BIG, F32)
        l_sc[...] = jnp.zeros(l_sc.shape, F32)
        acc_sc[...] = jnp.zeros(acc_sc.shape, F32)
        sa_sc[...] = scores(k_ref)

    def step(masked, cur_sc, nxt_sc):
        nxt_sc[...] = scores(kn_ref)
        s = cur_sc[...]
        if masked:
            qpos = qi * tq + jnp.bitwise_and(lax.broadcasted_iota(jnp.int32, (m_rows, 1), 0), tq - 1)
            s = jnp.where((ki * tk + lax.broadcasted_iota(jnp.int32, (1, tk), 1)) <= qpos, s, NEG_BIG)
        m_prev = m_sc[...]
        m_new = jnp.maximum(m_prev, _rep(_row_max(s)))
        alpha = jnp.exp2((m_prev - m_new) * c_exp)
        p = jnp.exp2((s - _tile_lanes(m_new, tk)) * c_exp)
        l_sc[...] = alpha * l_sc[...] + _fold_lanes(p, jnp.add)
        acc_sc[...] = _tile_lanes(alpha, MLA_KV_RANK) * acc_sc[...] + _dot(p.astype(BF16), k_ref[:, :MLA_KV_RANK])
        m_sc[...] = m_new

    crosses = (ki + 1) * tk - 1 > qi * tq
    even = lax.rem(ki, 2) == 0
    for par, (cur_sc, nxt_sc) in enumerate(((sa_sc, sb_sc), (sb_sc, sa_sc))):
        for masked in (True, False):
            cond = jnp.logical_and(ki < n_needed, jnp.logical_and(even == (par == 0), crosses == masked))
            pl.when(cond)(functools.partial(step, masked, cur_sc, nxt_sc))

    @pl.when(ki == n_needed - 1)
    def _():
        out = acc_sc[...] / jnp.maximum(jnp.sum(l_sc[...], axis=1, keepdims=True), 1e-30)
        o_ref[...] = out.reshape(MLA_HEADS, tq, MLA_KV_RANK).astype(o_ref.dtype)


def _mla_prompt_attention(q_ext, k_ext, batch, seq, tq, tk):
    nq, nk = seq // tq, seq // tk
    kw = q_ext.shape[-1]

    def k_map(off):
        def imap(b, qi, ki):
            last = ((qi + 1) * tq + tk - 1) // tk - 1
            return (b * nk + jnp.minimum(ki + off, last), 0)
        return imap

    return pl.pallas_call(
        functools.partial(_mla_attn_kernel, tq=tq, tk=tk),
        grid=(batch, nq, nk),
        in_specs=[pl.BlockSpec((MLA_HEADS, tq, kw), lambda b, qi, ki: (0, b * nq + qi, 0)),
                  pl.BlockSpec((tk, kw), k_map(0)), pl.BlockSpec((tk, kw), k_map(1))],
        out_specs=pl.BlockSpec((MLA_HEADS, tq, MLA_KV_RANK), lambda b, qi, ki: (0, b * nq + qi, 0)),
        out_shape=jax.ShapeDtypeStruct((MLA_HEADS, batch * seq, MLA_KV_RANK), BF16),
        scratch_shapes=[pltpu.VMEM((MLA_HEADS * tq, LANES), F32), pltpu.VMEM((MLA_HEADS * tq, LANES), F32),
                        pltpu.VMEM((MLA_HEADS * tq, MLA_KV_RANK), F32),
                        pltpu.VMEM((MLA_HEADS * tq, tk), F32), pltpu.VMEM((MLA_HEADS * tq, tk), F32)],
        compiler_params=_cparams(("parallel", "parallel", "arbitrary")),
        name="mla_prompt_attn",
    )(q_ext, k_ext, k_ext)


SEG_W = CMP_STRIDE * NSA_KV_W
KVP_W = 4 * LANES


def _prep_cmp_weights(w_cmp, pe_cmp):
    wh = w_cmp.reshape(2, 2, CMP_STRIDE, NSA_DH, NSA_DH)
    t = jnp.transpose(jnp.pad(wh, ((0, 0),) * 4 + ((0, LANES - NSA_DH),)), (1, 2, 0, 3, 4)).astype(BF16)
    eye_c = np.eye(2, dtype=np.float32).reshape(1, 1, 1, 2, 1, 2, 1, 1)
    eye_g = np.eye(NSA_KV_HEADS, dtype=np.float32).reshape(1, 1, NSA_KV_HEADS, 1, 1, 1, NSA_KV_HEADS, 1)
    mask = jnp.asarray(eye_c * eye_g, BF16)
    big = t[:, :, None, :, :, None, None, :] * mask
    w_big = big.reshape(2, SEG_W, KVP_W)
    peh = pe_cmp.reshape(2, 2, CMP_STRIDE, NSA_DH)
    pe_big = jnp.broadcast_to(jnp.transpose(peh, (1, 2, 0, 3))[:, :, None], (2, CMP_STRIDE, NSA_KV_HEADS, 2, NSA_DH))
    return w_big, pe_big.reshape(2, 1, SEG_W)


def _slc_overlap_matrix(n_cmp_rows, n_lanes):
    ratio = SLC_BLOCK // CMP_STRIDE
    lead = CMP_BLOCK // CMP_STRIDE - 1
    a = np.zeros((n_cmp_rows, n_lanes), np.float32)
    for j in range(n_lanes):
        for o in range(-lead, ratio):
            n = ratio * j + o
            if 0 <= n < n_cmp_rows:
                a[n, j] = (min(CMP_STRIDE * o + CMP_BLOCK, SLC_BLOCK) - max(CMP_STRIDE * o, 0)) / CMP_BLOCK
    return a


def _key_position_features(n_keys, n_lanes):
    kc = np.zeros((n_keys, LANES + n_lanes), np.float32)
    pos = np.arange(n_keys)
    kc[:, NSA_DH] = (pos // SLC_BLOCK) * SLC_BLOCK
    kc[:, NSA_DH + 1] = pos % SLC_BLOCK
    kc[pos, LANES + pos // SLC_BLOCK] = 1.0
    return kc


def _slopes_col(g, rows, tq):
    r = lax.shift_right_logical(lax.broadcasted_iota(jnp.int32, (rows, 1), 0), int(np.log2(tq)))
    col = jnp.zeros((rows, 1), F32)
    for rr in range(NSA_GROUP):
        col = jnp.where(r == rr, 2.0 ** (-(g * NSA_GROUP + rr + 1.0)), col)
    return col


def _softmax_rows(s, mask):
    n = s.shape[1]
    s = jnp.where(mask, s, NEG_BIG)
    m = _rep(_row_max(s))
    e = jnp.where(mask, jnp.exp(s - _tile_lanes(m, n)), 0.0)
    inv = 1.0 / jnp.maximum(_rep(_row_sum(e)), 1e-30)
    return e * _tile_lanes(inv, n)


def _top_blocks(score, n_pick):
    nb = score.shape[0]
    blk = lax.broadcasted_iota(jnp.int32, score.shape, 0).astype(F32)
    sel = jnp.zeros(score.shape, F32)
    for _ in range(n_pick):
        m = jnp.max(score, axis=0, keepdims=True)
        idx = jnp.min(jnp.where(score == m, blk, float(nb)), axis=0, keepdims=True)
        hit = blk == idx
        sel = jnp.where(jnp.logical_and(hit, m > 0.5 * NEG_BIG), 1.0, sel)
        score = jnp.where(hit, 2.0 * NEG_BIG, score)
    return sel


def _split3(x):
    hi = x.astype(BF16)
    r1 = x - hi.astype(F32)
    mid = r1.astype(BF16)
    lo = (r1 - mid.astype(F32)).astype(BF16)
    return hi, mid, lo


def _cmp_prompt_kernel(x_ref, pe_ref, w_ref, kvc_ref):
    x = x_ref[...]
    h1 = _dot((x + pe_ref[0]).astype(BF16), w_ref[0])
    h2 = _dot((x + pe_ref[1]).astype(BF16), w_ref[1])
    n = x.shape[0]
    kvc_ref[...] = (h1 + pltpu.roll(h2, n - 1, 0)).astype(kvc_ref.dtype)


def _cmp_prompt(kv_cmp, w_big, pe_big, batch, seq):
    nseg = seq // CMP_STRIDE
    x = kv_cmp.reshape(batch * nseg, SEG_W)
    return pl.pallas_call(
        _cmp_prompt_kernel,
        grid=(batch,),
        in_specs=[pl.BlockSpec((nseg, SEG_W), lambda b: (b, 0)), _full_spec(pe_big), _full_spec(w_big)],
        out_specs=pl.BlockSpec((nseg, KVP_W), lambda b: (b, 0)),
        out_shape=jax.ShapeDtypeStruct((batch * nseg, KVP_W), BF16),
        compiler_params=_cparams(("parallel",)),
        name="nsa_cmp_prompt",
    )(x, pe_big, w_big)


N_WIN_BLOCKS = WINDOW // 128 + 1


def _nsa_prompt_kernel(qn_ref, gl_ref, kvc_ref, amat_ref, slcp_ref, kc_ref, *rest, tq, tk, nseg):
    win_refs = rest[:N_WIN_BLOCKS]
    o_ref = rest[N_WIN_BLOCKS]
    stat = rest[N_WIN_BLOCKS + 1:N_WIN_BLOCKS + 1 + 3 * NSA_KV_HEADS]
    flag_ref = rest[-1]
    nt_all = slcp_ref.shape[0] // tk
    qi = pl.program_id(1)
    s0 = qi * tq
    rows = NSA_GROUP * tq
    nl = amat_ref.shape[1]
    qpos = s0 + jnp.bitwise_and(lax.broadcasted_iota(jnp.int32, (rows, 1), 0), tq - 1)
    qpos_t = s0 + lax.broadcasted_iota(jnp.int32, (tq, 1), 0)
    gates = jax.nn.sigmoid(gl_ref[...])
    n_tiles = (s0 + tq + tk - 1) // tk
    lane = lax.broadcasted_iota(jnp.int32, (1, LANES), 1)
    is_pos_lane = jnp.logical_or(lane == NSA_DH, lane == NSA_DH + 1)
    blk = lax.broadcasted_iota(jnp.int32, (1, nl), 1)
    cur = lax.shift_right_logical(qpos_t, int(np.log2(SLC_BLOCK)))
    valid = blk * SLC_BLOCK <= qpos_t
    forced = jnp.logical_or(blk == 0, jnp.logical_or(blk == cur, blk == cur - 1))
    cpos = lax.broadcasted_iota(jnp.int32, (1, nseg), 1) * CMP_STRIDE + (CMP_BLOCK - 1)
    dist_c = (qpos - cpos).astype(F32)
    amat = amat_ref[...]

    slopes, qgs, o_cs, q_augs = [], [], [], []
    any_sel = jnp.zeros((1, nl), F32)
    for g in range(NSA_KV_HEADS):
        slope = _slopes_col(g, rows, tq)
        q_heads = [qn_ref[:, (g * NSA_GROUP + r) * LANES:(g * NSA_GROUP + r + 1) * LANES] for r in range(NSA_GROUP)]
        qg = jnp.concatenate(q_heads, axis=0)
        kc = kvc_ref[:, g * LANES:(g + 1) * LANES]
        vc = kvc_ref[:, (2 + g) * LANES:(3 + g) * LANES]
        p_c = _softmax_rows(_dot_nt(qg, kc) - slope * dist_c, dist_c >= 0)
        o_cs.append(_dot(p_c.astype(BF16), vc))
        p_grp = p_c[0:tq]
        for r in range(1, NSA_GROUP):
            p_grp = p_grp + p_c[r * tq:(r + 1) * tq]
        s_slc = sum(_dot(t, amat) for t in _split3(p_grp))
        score = jnp.where(forced, FORCE_SCORE, jnp.where(valid, s_slc, NEG_BIG))
        sel = _top_blocks(score.T, N_SELECT).T
        selneg = ((sel - 1.0) * MASK_BIG).astype(BF16)
        any_sel = jnp.maximum(any_sel, jnp.max(sel, axis=0, keepdims=True))
        q_augs.append(jnp.concatenate(
            [jnp.concatenate([jnp.where(is_pos_lane, (2.0 ** (-(g * NSA_GROUP + r + 1.0))), q_heads[r].astype(F32)).astype(BF16),
                              selneg], axis=1) for r in range(NSA_GROUP)], axis=0))
        slopes.append(slope)
        qgs.append(qg)
        m_sc, l_sc, acc_sc = stat[3 * g:3 * g + 3]
        m_sc[...] = jnp.full(m_sc.shape, NEG_BIG, F32)
        l_sc[...] = jnp.zeros(l_sc.shape, F32)
        acc_sc[...] = jnp.zeros(acc_sc.shape, F32)
    tile_of_blk = lax.shift_right_logical(blk, int(np.log2(tk // SLC_BLOCK)))
    for t in range(nt_all):
        flag_ref[t] = jnp.max(jnp.where(tile_of_blk == t, any_sel, 0.0)).astype(jnp.int32)

    def slc_tile(kt, causal):
        k0 = pl.multiple_of(kt * tk, tk)
        if causal:
            ok = (k0 + lax.broadcasted_iota(jnp.int32, (1, tk), 1)) <= qpos
        for g in range(NSA_KV_HEADS):
            m_sc, l_sc, acc_sc = stat[3 * g:3 * g + 3]
            kk = slcp_ref[pl.ds(k0, tk), g * LANES:(g + 1) * LANES] + kc_ref[pl.ds(k0, tk), 0:LANES]
            k_aug = jnp.concatenate([kk, kc_ref[pl.ds(k0, tk), LANES:]], axis=1)
            vv = slcp_ref[pl.ds(k0, tk), (2 + g) * LANES:(3 + g) * LANES]
            s = _dot_nt(q_augs[g], k_aug)
            if causal:
                s = jnp.where(ok, s, NEG_BIG)
            m_prev = m_sc[...]
            m_new = jnp.maximum(m_prev, _rep(_row_max(s)))
            alpha = jnp.exp(m_prev - m_new)
            p = jnp.exp(s - _tile_lanes(m_new, tk))
            if causal:
                p = jnp.where(ok, p, 0.0)
            l_sc[...] = alpha * l_sc[...] + _fold_lanes(p, jnp.add)
            acc_sc[...] = alpha * acc_sc[...] + _dot(p.astype(BF16), vv)
            m_sc[...] = m_new

    def body(kt, carry):
        @pl.when(flag_ref[kt] > 0)
        def _():
            slc_tile(kt, False)
        return carry

    lax.fori_loop(0, n_tiles - 1, body, 0)
    slc_tile(n_tiles - 1, True)
    nw = N_WIN_BLOCKS * 128
    wpos = s0 - WINDOW + lax.broadcasted_iota(jnp.int32, (1, nw), 1)
    dist_w = (qpos - wpos).astype(F32)
    mask_w = jnp.logical_and(jnp.logical_and(dist_w >= 0, dist_w <= WINDOW), wpos >= 0)
    for g in range(NSA_KV_HEADS):
        m_sc, l_sc, acc_sc = stat[3 * g:3 * g + 3]
        o_s = acc_sc[...] / jnp.maximum(jnp.sum(l_sc[...], axis=1, keepdims=True), 1e-30)
        kw = jnp.concatenate([w[:, g * LANES:(g + 1) * LANES] for w in win_refs], axis=0)
        vw = jnp.concatenate([w[:, (2 + g) * LANES:(3 + g) * LANES] for w in win_refs], axis=0)
        p_w = _softmax_rows(_dot_nt(qgs[g], kw) - slopes[g] * dist_w, mask_w)
        o_w = _dot(p_w.astype(BF16), vw)
        for r in range(NSA_GROUP):
            hd = g * NSA_GROUP + r
            sl = slice(r * tq, (r + 1) * tq)
            o_ref[:, hd * LANES:(hd + 1) * LANES] = (
                gates[:, 3 * hd:3 * hd + 1] * o_cs[g][sl] + gates[:, 3 * hd + 1:3 * hd + 2] * o_s[sl]
                + gates[:, 3 * hd + 2:3 * hd + 3] * o_w[sl])


def _nsa_prompt(qn, gl, kvc, slcp, winp, batch, seq, tq, tk):
    assert tq == 128 and seq % tk == 0 and tk % tq == 0
    nseg = seq // CMP_STRIDE
    n_slc = seq // SLC_BLOCK
    nl = LANES * ((n_slc + LANES - 1) // LANES)
    assert nl == LANES, "selection blocks must fit one lane group"
    nq = seq // tq
    amat = jnp.asarray(_slc_overlap_matrix(nseg, nl), BF16)
    kc = jnp.asarray(_key_position_features(seq, nl), BF16)

    def win_spec(j):
        return pl.BlockSpec((128, KVP_W), lambda b, qi: (b * nq + jnp.maximum(qi - (N_WIN_BLOCKS - 1) + j, 0), 0))

    rows = NSA_GROUP * tq
    return pl.pallas_call(
        functools.partial(_nsa_prompt_kernel, tq=tq, tk=tk, nseg=nseg),
        grid=(batch, nq),
        in_specs=[pl.BlockSpec((tq, NSA_HEADS * LANES), lambda b, qi: (b * nq + qi, 0)),
                  pl.BlockSpec((tq, LANES), lambda b, qi: (b * nq + qi, 0)),
                  pl.BlockSpec((nseg, KVP_W), lambda b, qi: (b, 0)),
                  _full_spec(amat),
                  pl.BlockSpec((seq, KVP_W), lambda b, qi: (b, 0)),
                  _full_spec(kc)] + [win_spec(j) for j in range(N_WIN_BLOCKS)],
        out_specs=pl.BlockSpec((tq, NSA_HEADS * LANES), lambda b, qi: (b * nq + qi, 0)),
        out_shape=jax.ShapeDtypeStruct((batch * seq, NSA_HEADS * LANES), F32),
        scratch_shapes=[pltpu.VMEM((rows, LANES), F32), pltpu.VMEM((rows, LANES), F32), pltpu.VMEM((rows, LANES), F32)]
        * NSA_KV_HEADS + [pltpu.SMEM((seq // tk,), jnp.int32)],
        compiler_params=_cparams(("parallel", "arbitrary")),
        name="nsa_prompt",
    )(qn, gl, kvc, amat, slcp, kc, *([winp] * N_WIN_BLOCKS))


def _silu(x):
    return x * jax.nn.sigmoid(x)


def _even_out_kernel(x_ref, lat_ref, gmla_ref, onsa_ref, gnsa_ref, wuv_ref, wo_ref, o_ref):
    o_mla = jnp.concatenate([_dot(lat_ref[hd], wuv_ref[hd]) for hd in range(MLA_HEADS)], axis=1)
    a = (o_mla * _silu(gmla_ref[...])).astype(BF16)
    b = (onsa_ref[...] * _silu(gnsa_ref[...])).astype(BF16)
    nm = MLA_HEADS * LANES
    o_ref[...] = x_ref[...] + _dot(a, wo_ref[0:nm]) + _dot(b, wo_ref[nm:])


def _even_output(x, lat, gmla, onsa, gnsa, wuv_ext, wo_ext, tm):
    t = x.shape[0]
    row = lambda w: pl.BlockSpec((tm, w), lambda i: (i, 0))
    return pl.pallas_call(
        _even_out_kernel,
        grid=(t // tm,),
        in_specs=[row(D_MODEL), pl.BlockSpec((MLA_HEADS, tm, MLA_KV_RANK), lambda i: (0, i, 0)),
                  row(MLA_HEADS * LANES), row(NSA_HEADS * LANES), row(NSA_HEADS * LANES),
                  _full_spec(wuv_ext), _full_spec(wo_ext)],
        out_specs=row(D_MODEL),
        out_shape=jax.ShapeDtypeStruct((t, D_MODEL), F32),
        compiler_params=_cparams(("parallel",)),
        name="even_out",
    )(x, lat, gmla, onsa, gnsa, wuv_ext, wo_ext)


_RET_QK = RET_HEADS * RET_DK
_RET_LOG_G = [float(np.log1p(-(2.0 ** (-5.0 - h)))) for h in range(RET_HEADS)]


def _rope_tables_ret(pos):
    half = RET_DK // 2
    inv = ROPE_BASE ** (-jnp.arange(half, dtype=F32) / half)
    ang = pos.astype(F32)[:, None] * inv
    cos, sin = jnp.cos(ang), jnp.sin(ang)
    return jnp.concatenate([cos, cos], 1), jnp.concatenate([-sin, sin], 1)


def _odd_proj_kernel(x_ref, gn_ref, w_ref, c_ref, s_ref, q_ref, k_ref, v_ref, g_ref):
    h = _rms(x_ref[...], gn_ref[...]).astype(BF16)
    c = c_ref[...]
    s = s_ref[...]
    half = RET_DK // 2

    def rot(z):
        return z * c + pltpu.roll(z, half, 1) * s

    for hd in range(RET_HEADS):
        sl = slice(hd * RET_DK, (hd + 1) * RET_DK)
        q_ref[:, sl] = rot(_dot(h, w_ref[:, sl])).astype(BF16)
        ks = slice(_RET_QK + hd * RET_DK, _RET_QK + (hd + 1) * RET_DK)
        k_ref[:, sl] = (rot(_dot(h, w_ref[:, ks])) * (RET_DK ** -0.5)).astype(BF16)
    v_ref[...] = _dot(h, w_ref[:, 2 * _RET_QK:2 * _RET_QK + RET_WIDTH])
    g_ref[...] = _dot(h, w_ref[:, 2 * _RET_QK + RET_WIDTH:])


def _odd_project(x, gn, w_bf, cos, sin, tm):
    t = x.shape[0]
    ntab = cos.shape[0] // tm
    row = lambda w: pl.BlockSpec((tm, w), lambda i: (i, 0))
    tab = pl.BlockSpec((tm, RET_DK), lambda i: (i % ntab, 0))
    return pl.pallas_call(
        _odd_proj_kernel,
        grid=(t // tm,),
        in_specs=[row(D_MODEL), _full_spec(gn), _full_spec(w_bf), tab, tab],
        out_specs=(row(_RET_QK), row(_RET_QK), row(RET_WIDTH), row(RET_WIDTH)),
        out_shape=(jax.ShapeDtypeStruct((t, _RET_QK), BF16), jax.ShapeDtypeStruct((t, _RET_QK), BF16),
                   jax.ShapeDtypeStruct((t, RET_WIDTH), F32), jax.ShapeDtypeStruct((t, RET_WIDTH), F32)),
        compiler_params=_cparams(("parallel",)),
        name="odd_proj",
    )(x, gn, w_bf, cos, sin)


def _group_norm_gate(o, gn_row, g):
    mu = jnp.mean(o, axis=-1, keepdims=True)
    var = jnp.mean(jnp.square(o - mu), axis=-1, keepdims=True)
    return _silu(g) * ((o - mu) * lax.rsqrt(var + EPS) * gn_row)


def _ret_prompt_kernel(q_ref, k_ref, v_ref, g_ref, gn_ref, y_ref, sfin_ref, s_sc, *, chunk):
    ci = pl.program_id(1)

    @pl.when(ci == 0)
    def _():
        s_sc[...] = jnp.zeros(s_sc.shape, F32)

    n_col = lax.broadcasted_iota(jnp.int32, (chunk, 1), 0).astype(F32)
    diff = n_col - lax.broadcasted_iota(jnp.int32, (1, chunk), 1).astype(F32)
    for hd in range(RET_HEADS):
        lg = _RET_LOG_G[hd]
        dmat = jnp.where(diff >= 0, jnp.exp(lg * jnp.maximum(diff, 0.0)), 0.0)
        xi = jnp.exp(lg * (n_col + 1.0))
        zeta = jnp.exp(lg * (chunk - 1.0 - n_col))
        qh = q_ref[:, hd * RET_DK:(hd + 1) * RET_DK]
        kh = k_ref[:, hd * RET_DK:(hd + 1) * RET_DK]
        vh = v_ref[:, hd * RET_DV:(hd + 1) * RET_DV]
        s_prev = s_sc[hd]
        inner = _dot_nt(qh, kh) * dmat
        o = _dot(inner.astype(BF16), vh.astype(BF16)) + _dot(qh, s_prev.astype(BF16)) * xi
        kv = lax.dot_general(kh, (vh * zeta).astype(BF16), (((0,), (0,)), ((), ())), preferred_element_type=F32)
        s_sc[hd] = float(np.exp(lg * chunk)) * s_prev + kv
        y_ref[:, hd * RET_DV:(hd + 1) * RET_DV] = _group_norm_gate(
            o, gn_ref[hd:hd + 1, :], g_ref[:, hd * RET_DV:(hd + 1) * RET_DV]).astype(y_ref.dtype)

    @pl.when(ci == pl.num_programs(1) - 1)
    def _():
        sfin_ref[0] = s_sc[...]


def _retention_prompt(q, k, v, g, gn, batch, seq):
    chunk = RET_CHUNK
    nc = seq // chunk
    row = lambda w: pl.BlockSpec((chunk, w), lambda b, c: (b * nc + c, 0))
    return pl.pallas_call(
        functools.partial(_ret_prompt_kernel, chunk=chunk),
        grid=(batch, nc),
        in_specs=[row(_RET_QK), row(_RET_QK), row(RET_WIDTH), row(RET_WIDTH), _full_spec(gn)],
        out_specs=(row(RET_WIDTH),
                   pl.BlockSpec((1, RET_HEADS, RET_DK, RET_DV), lambda b, c: (b, 0, 0, 0))),
        out_shape=(jax.ShapeDtypeStruct((batch * seq, RET_WIDTH), BF16),
                   jax.ShapeDtypeStruct((batch, RET_HEADS, RET_DK, RET_DV), F32)),
        scratch_shapes=[pltpu.VMEM((RET_HEADS, RET_DK, RET_DV), F32)],
        compiler_params=_cparams(("parallel", "arbitrary")),
        name="retention_prompt",
    )(q, k, v, g, gn)


def _odd_out_kernel(x_ref, y_ref, w_ref, fn_ref, o_ref):
    x2 = x_ref[...] + _dot(y_ref[...], w_ref[...])
    o_ref[...] = _rms(x2, fn_ref[...])


def _odd_output(x, y, w_bf, fn, tm):
    t = x.shape[0]
    row = lambda w: pl.BlockSpec((tm, w), lambda i: (i, 0))
    return pl.pallas_call(
        _odd_out_kernel,
        grid=(t // tm,),
        in_specs=[row(D_MODEL), row(RET_WIDTH), _full_spec(w_bf), _full_spec(fn)],
        out_specs=row(D_MODEL),
        out_shape=jax.ShapeDtypeStruct((t, D_MODEL), F32),
        compiler_params=_cparams(("parallel",)),
        name="odd_out",
    )(x, y, w_bf, fn)


def _ret_decode_kernel(q_ref, k_ref, v_ref, g_ref, gn_ref, s_ref, y_ref, snew_ref):
    eye = (lax.broadcasted_iota(jnp.int32, (RET_DK, RET_DK), 0)
           == lax.broadcasted_iota(jnp.int32, (RET_DK, RET_DK), 1))
    for hd in range(RET_HEADS):
        gam = float(np.exp(_RET_LOG_G[hd]))
        qh = q_ref[0, :, hd * RET_DK:(hd + 1) * RET_DK]
        kh = k_ref[0, :, hd * RET_DK:(hd + 1) * RET_DK]
        vh = v_ref[0, :, hd * RET_DV:(hd + 1) * RET_DV].astype(BF16).astype(F32)
        s_prev = s_ref[0, hd]
        inner = jnp.sum(qh.astype(F32) * kh.astype(F32), axis=1, keepdims=True).astype(BF16).astype(F32)
        qs = _dot(jnp.broadcast_to(qh, (8, RET_DK)), s_prev.astype(BF16))[0:1]
        o = inner * vh + qs * gam
        k_col = jnp.sum(jnp.where(eye, jnp.broadcast_to(kh.astype(F32), (RET_DK, RET_DK)), 0.0), axis=1, keepdims=True)
        snew_ref[0, hd] = gam * s_prev + k_col * vh
        y_ref[0, :, hd * RET_DV:(hd + 1) * RET_DV] = _group_norm_gate(
            o, gn_ref[hd:hd + 1, :], g_ref[0, :, hd * RET_DV:(hd + 1) * RET_DV]).astype(y_ref.dtype)


def _retention_decode(q, k, v, g, gn, state):
    db = q.shape[0]
    r3 = lambda a: a.reshape(db, 1, a.shape[-1])
    vec = lambda w: pl.BlockSpec((1, 1, w), lambda b: (b, 0, 0))
    st = pl.BlockSpec((1, RET_HEADS, RET_DK, RET_DV), lambda b: (b, 0, 0, 0))
    y, s_new = pl.pallas_call(
        _ret_decode_kernel,
        grid=(db,),
        in_specs=[vec(_RET_QK), vec(_RET_QK), vec(RET_WIDTH), vec(RET_WIDTH), _full_spec(gn), st],
        out_specs=(vec(RET_WIDTH), st),
        out_shape=(jax.ShapeDtypeStruct((db, 1, RET_WIDTH), BF16),
                   jax.ShapeDtypeStruct(state.shape, state.dtype)),
        compiler_params=_cparams(("parallel",)),
        name="retention_decode",
    )(r3(q), r3(k), r3(v), r3(g), gn, state)
    return y.reshape(db, RET_WIDTH), s_new


PAGES_PER_STEP = 32
SUB = 8


def _mla_decode_kernel(pt_ref, q_ref, new_ref, *rest, steps):
    del pt_ref
    page_refs = rest[:PAGES_PER_STEP]
    o_ref, m_sc, l_sc, acc_sc, r0_sc, r1_sc, s0_sc, s1_sc = rest[PAGES_PER_STEP:]
    bufs = ((r0_sc, s0_sc), (r1_sc, s1_sc))
    i = pl.program_id(1)
    q = q_ref[0]

    def load(par):
        r_sc, s_sc = bufs[par]
        rows_t = jnp.concatenate([p[...] for p in page_refs], axis=1).astype(BF16)
        r_sc[...] = rows_t
        s_sc[...] = _dot(q, rows_t) * MLA_SCALE

    def proc(par):
        r_sc, s_sc = bufs[par]
        s = s_sc[...]
        m_prev = m_sc[...]
        m_new = jnp.maximum(m_prev, _row_max(s))
        alpha = jnp.exp(m_prev - m_new)
        p = jnp.exp(s - m_new)
        l_sc[...] = alpha * l_sc[...] + _row_sum(p)
        acc_sc[...] = alpha * acc_sc[...] + _dot_nt(p.astype(BF16), r_sc[0:MLA_KV_RANK, :])
        m_sc[...] = m_new

    @pl.when(i == 0)
    def _():
        m_sc[...] = jnp.full(m_sc.shape, NEG_BIG, F32)
        l_sc[...] = jnp.zeros(l_sc.shape, F32)
        acc_sc[...] = jnp.zeros(acc_sc.shape, F32)
        load(0)

    even = lax.rem(i, 2) == 0
    steady = jnp.logical_and(i > 0, i < steps)
    for par in range(2):
        @pl.when(jnp.logical_and(steady, even == (par == 0)))
        def _(par=par):
            load(par)
            proc(1 - par)

    @pl.when(i == steps)
    def _():
        proc((steps - 1) % 2)
        new = new_ref[0].astype(BF16).astype(F32)
        s_n = jnp.sum(q.astype(F32) * new, axis=1, keepdims=True) * MLA_SCALE
        m_prev = m_sc[...]
        m_new = jnp.maximum(m_prev, s_n)
        alpha = jnp.exp(m_prev - m_new)
        p_n = jnp.exp(s_n - m_new)
        l = alpha * l_sc[...] + p_n
        acc = alpha * acc_sc[...] + p_n.astype(BF16).astype(F32) * new[:, :MLA_KV_RANK]
        o_ref[0] = (acc / jnp.maximum(l, 1e-30)).astype(o_ref.dtype)


def _mla_decode(q_dec, rows_new, pool, page_table):
    db, n_pages = page_table.shape
    assert n_pages % PAGES_PER_STEP == 0
    steps = n_pages // PAGES_PER_STEP
    w = pool.shape[1]
    n_tok = PAGES_PER_STEP * PAGE_SIZE

    def page_spec(j):
        return pl.BlockSpec((None, w, PAGE_SIZE),
                            lambda b, i, pt: (pt[b * n_pages + jnp.minimum(i, steps - 1) * PAGES_PER_STEP + j], 0, 0))

    grid_spec = pltpu.PrefetchScalarGridSpec(
        num_scalar_prefetch=1,
        grid=(db, steps + 1),
        in_specs=[pl.BlockSpec((1, MLA_HEADS, w), lambda b, i, pt: (b, 0, 0)),
                  pl.BlockSpec((1, 1, w), lambda b, i, pt: (b, 0, 0))]
                 + [page_spec(j) for j in range(PAGES_PER_STEP)],
        out_specs=pl.BlockSpec((1, MLA_HEADS, MLA_KV_RANK), lambda b, i, pt: (b, 0, 0)),
        scratch_shapes=[pltpu.VMEM((MLA_HEADS, 1), F32), pltpu.VMEM((MLA_HEADS, 1), F32),
                        pltpu.VMEM((MLA_HEADS, MLA_KV_RANK), F32),
                        pltpu.VMEM((w, n_tok), BF16), pltpu.VMEM((w, n_tok), BF16),
                        pltpu.VMEM((MLA_HEADS, n_tok), F32), pltpu.VMEM((MLA_HEADS, n_tok), F32)],
    )
    return pl.pallas_call(
        functools.partial(_mla_decode_kernel, steps=steps),
        grid_spec=grid_spec,
        out_shape=jax.ShapeDtypeStruct((db, MLA_HEADS, MLA_KV_RANK), BF16),
        compiler_params=_cparams(("parallel", "arbitrary")),
        name="mla_decode",
    )(page_table.reshape(-1), q_dec, rows_new.reshape(db, 1, w), *([pool] * PAGES_PER_STEP))


def _decode_blocks(p_len, n_new):
    n_slc = -(-(p_len + n_new) // SLC_BLOCK)
    return n_slc, LANES * ((n_slc + LANES - 1) // LANES)


CMP_PAIRS = CMP_STRIDE // 2
GRP_W = 2 * NSA_DH


def _prep_cmp_pair_weights(w_cmp, pe_cmp):
    wh = w_cmp.reshape(2, 2, CMP_PAIRS, 2, NSA_DH, NSA_DH)
    t = jnp.transpose(wh, (2, 3, 0, 4, 1, 5)).astype(BF16)
    eye_c = jnp.asarray(np.eye(2, dtype=np.float32).reshape(1, 1, 2, 1, 1, 2, 1), BF16)
    big = t[:, :, :, :, :, None, :] * eye_c
    w_pair = big.reshape(CMP_PAIRS, 2 * GRP_W, 2 * GRP_W)
    peh = pe_cmp.reshape(2, 2, CMP_PAIRS, 2, NSA_DH)
    pe_pair = jnp.transpose(peh, (1, 2, 3, 0, 4)).reshape(2, CMP_PAIRS, 1, 2 * GRP_W)
    return w_pair, pe_pair


def _cmp_bias_kernel(pe_ref, w_ref, b_ref):
    for half in range(2):
        acc = jnp.zeros((SUB, GRP_W), F32)
        for jp in range(CMP_PAIRS):
            w = w_ref[jp][:, half * GRP_W:(half + 1) * GRP_W]
            for t in _split3(jnp.broadcast_to(pe_ref[half, jp], (SUB, 2 * GRP_W))):
                acc = acc + _dot(t, w)
        b_ref[half] = acc


def _cmp_bias(w_pair, pe_pair):
    return pl.pallas_call(
        _cmp_bias_kernel,
        grid=(1,),
        in_specs=[_full_spec(pe_pair), _full_spec(w_pair)],
        out_specs=pl.BlockSpec((2, SUB, GRP_W), lambda i: (0, 0, 0)),
        out_shape=jax.ShapeDtypeStruct((2, SUB, GRP_W), F32),
        compiler_params=_cparams(("arbitrary",)),
        name="nsa_cmp_bias",
    )(pe_pair, w_pair)


def _cmp_decode_kernel(pt_ref, qn_ref, cnew_ref, bias_ref, w_ref, amat_ref, *rest, p_len, steps):
    del pt_ref
    page_refs = rest[:PAGES_PER_STEP]
    oc_ref, sslc_ref, xa_sc, xb_sc, h_sc = rest[PAGES_PER_STEP:]
    x_bufs = (xa_sc, xb_sc)
    i = pl.program_id(1)
    n_tok = PAGES_PER_STEP * PAGE_SIZE
    n_step = n_tok // CMP_STRIDE
    nseg = h_sc.shape[1]

    def load(par):
        x_sc = x_bufs[par]
        for j, p in enumerate(page_refs):
            for g in range(NSA_KV_HEADS):
                x_sc[g, j * PAGE_SIZE:(j + 1) * PAGE_SIZE, :] = p[g * GRP_W:(g + 1) * GRP_W, :].T

    def compress(par, step):
        x_sc = x_bufs[par]
        r0 = pl.multiple_of(step * n_step, n_step)
        for g in range(NSA_KV_HEADS):
            acc = jnp.zeros((n_step, 2 * GRP_W), F32)
            for jp in range(CMP_PAIRS):
                rows = [x_sc[g, pl.ds(2 * jp + jj, n_step, stride=CMP_STRIDE), :] for jj in range(2)]
                acc = acc + _dot(jnp.concatenate(rows, axis=1).astype(BF16), w_ref[jp])
            h_sc[g, pl.ds(r0, n_step), :] = acc

    @pl.when(i == 0)
    def _():
        load(0)

    even = lax.rem(i, 2) == 0
    steady = jnp.logical_and(i > 0, i < steps)
    for par in range(2):
        @pl.when(jnp.logical_and(steady, even == (par == 0)))
        def _(par=par):
            load(par)
            compress(1 - par, i - 1)

    @pl.when(i == steps)
    def _():
        compress((steps - 1) % 2, steps - 1)

    @pl.when(i == steps)
    def _():
        qpos = p_len
        cpos = lax.broadcasted_iota(jnp.int32, (1, nseg), 1) * CMP_STRIDE + (CMP_BLOCK - 1)
        dist = (qpos - cpos).astype(F32)
        row = lax.broadcasted_iota(jnp.int32, (SUB, 1), 0)
        seg = lax.broadcasted_iota(jnp.int32, (nseg, 1), 0)
        amat = amat_ref[...]
        b1 = bias_ref[0, 0:1, :]
        b2 = bias_ref[1, 0:1, :]
        for g in range(NSA_KV_HEADS):
            h = h_sc[g]
            new_seg = jnp.concatenate([cnew_ref[0, :, g * GRP_W:(g + 1) * GRP_W], jnp.zeros((1, GRP_W), F32)], axis=1)
            h2n = _dot(jnp.broadcast_to(new_seg, (SUB, 2 * GRP_W)).astype(BF16), w_ref[0])[0:1, GRP_W:] + b2
            h2 = jnp.where(seg == nseg - 1, h2n, pltpu.roll(h[:, GRP_W:], nseg - 1, 0) + b2)
            kvc = (h[:, :GRP_W] + b1 + h2).astype(BF16)
            slope = jnp.zeros((SUB, 1), F32)
            for rr in range(NSA_GROUP):
                slope = jnp.where(row == rr, 2.0 ** (-(g * NSA_GROUP + rr + 1.0)), slope)
            qg = qn_ref[0, g]
            p_c = _softmax_rows(_dot_nt(qg, kvc) - slope * dist, dist >= 0)
            p_c = jnp.where(row < NSA_GROUP, p_c, 0.0)
            oc_ref[0, g] = pltpu.roll(_dot(p_c.astype(BF16), kvc), LANES - NSA_DH, 1)
            s_rows = sum(_dot(t, amat) for t in _split3(p_c))
            sslc_ref[0, g] = jnp.broadcast_to(jnp.sum(s_rows, axis=0, keepdims=True), s_rows.shape)


def _cmp_decode(qn_dec, cmp_new, pool_t, page_table, w_pair, bias):
    db, n_pages = page_table.shape
    steps = n_pages // PAGES_PER_STEP
    p_len = n_pages * PAGE_SIZE
    nseg = p_len // CMP_STRIDE
    n_slc, nl = _decode_blocks(p_len, 1)
    amat = jnp.asarray(_slc_overlap_matrix(nseg, nl), BF16)

    def page_spec(j):
        return pl.BlockSpec((None, NSA_KV_W, PAGE_SIZE),
                            lambda b, i, pt: (pt[b * n_pages + jnp.minimum(i, steps - 1) * PAGES_PER_STEP + j], 0, 0))

    grid_spec = pltpu.PrefetchScalarGridSpec(
        num_scalar_prefetch=1,
        grid=(db, steps + 1),
        in_specs=[pl.BlockSpec((1, NSA_KV_HEADS, SUB, LANES), lambda b, i, pt: (b, 0, 0, 0)),
                  pl.BlockSpec((1, 1, NSA_KV_W), lambda b, i, pt: (b, 0, 0)),
                  pl.BlockSpec(bias.shape, lambda b, i, pt: (0, 0, 0)),
                  pl.BlockSpec(w_pair.shape, lambda b, i, pt: (0, 0, 0)),
                  pl.BlockSpec(amat.shape, lambda b, i, pt: (0, 0))]
                 + [page_spec(j) for j in range(PAGES_PER_STEP)],
        out_specs=(pl.BlockSpec((1, NSA_KV_HEADS, SUB, LANES), lambda b, i, pt: (b, 0, 0, 0)),
                   pl.BlockSpec((1, NSA_KV_HEADS, SUB, nl), lambda b, i, pt: (b, 0, 0, 0))),
        scratch_shapes=[pltpu.VMEM((NSA_KV_HEADS, PAGES_PER_STEP * PAGE_SIZE, GRP_W), F32),
                        pltpu.VMEM((NSA_KV_HEADS, PAGES_PER_STEP * PAGE_SIZE, GRP_W), F32),
                        pltpu.VMEM((NSA_KV_HEADS, nseg, 2 * GRP_W), F32)],
    )
    return pl.pallas_call(
        functools.partial(_cmp_decode_kernel, p_len=p_len, steps=steps),
        grid_spec=grid_spec,
        out_shape=(jax.ShapeDtypeStruct((db, NSA_KV_HEADS, SUB, LANES), F32),
                   jax.ShapeDtypeStruct((db, NSA_KV_HEADS, SUB, nl), F32)),
        compiler_params=_cparams(("parallel", "arbitrary")),
        name="nsa_cmp_decode",
    )(page_table.reshape(-1), qn_dec, cmp_new.reshape(db, 1, NSA_KV_W), bias, w_pair, amat,
      *([pool_t] * PAGES_PER_STEP))


def _topk_decode_kernel(s_ref, idx_ref, *, qpos):
    s_slc = s_ref[...]
    blk = lax.broadcasted_iota(jnp.int32, s_slc.shape, 1)
    cur = qpos // SLC_BLOCK
    valid = blk * SLC_BLOCK <= qpos
    forced = jnp.logical_or(blk == 0, jnp.logical_or(blk == cur, blk == cur - 1))
    score = jnp.where(forced, FORCE_SCORE, jnp.where(valid, s_slc, NEG_BIG))
    lane = lax.broadcasted_iota(jnp.int32, idx_ref.shape, 1)
    out = jnp.full(idx_ref.shape, -1, jnp.int32)
    for t in range(N_SELECT):
        m = jnp.max(score, axis=1, keepdims=True)
        idx = jnp.min(jnp.where(score == m, blk, score.shape[1]), axis=1, keepdims=True)
        out = jnp.where(lane == t, jnp.where(m > 0.5 * NEG_BIG, idx, -1), out)
        score = jnp.where(blk == idx, 2.0 * NEG_BIG, score)
    idx_ref[...] = out


def _topk_decode(s_slc, qpos):
    rows = s_slc.shape[0]
    return pl.pallas_call(
        functools.partial(_topk_decode_kernel, qpos=qpos),
        grid=(1,),
        in_specs=[_full_spec(s_slc)],
        out_specs=pl.BlockSpec((rows, LANES), lambda i: (0, 0)),
        out_shape=jax.ShapeDtypeStruct((rows, LANES), jnp.int32),
        compiler_params=_cparams(("arbitrary",)),
        name="nsa_topk_decode",
    )(s_slc)


def _sel_decode_kernel(pt_ref, sel_ref, qn_ref, gl_ref, oc_ref, snew_ref, wnew_ref, wnewt_ref, win_ref, *rest, p_len):
    del pt_ref
    nb = NSA_KV_HEADS * N_SELECT
    blk_refs = rest[:nb]
    o_ref, nwin_ref = rest[nb:]
    b = pl.program_id(0)
    qpos = p_len
    n_past_blocks = p_len // SLC_BLOCK
    per_page = PAGE_SIZE // SLC_BLOCK
    row = lax.broadcasted_iota(jnp.int32, (SUB, 1), 0)
    lane = lax.broadcasted_iota(jnp.int32, (SUB, LANES), 1)
    gates = jnp.broadcast_to(jax.nn.sigmoid(gl_ref[0]), (SUB, LANES))
    nw = win_ref.shape[2]
    win = win_ref[0]
    wnew = wnew_ref[0]
    snew = snew_ref[0]

    def merge_new(s, mask, kv_t, q8, new_row, slope, use_new):
        new_f = new_row.astype(BF16).astype(F32)
        s_n = jnp.sum(q8.astype(F32) * new_f, axis=1, keepdims=True)
        s = jnp.where(mask, s, NEG_BIG)
        m = jnp.maximum(_row_max(s), jnp.where(use_new, s_n, NEG_BIG))
        e = jnp.where(mask, jnp.exp(s - m), 0.0)
        e_n = jnp.where(use_new, jnp.exp(s_n - m), 0.0)
        denom = jnp.maximum(_row_sum(e) + e_n, 1e-30)
        p = e / denom
        p_n = e_n / denom
        o = _dot_nt(p.astype(BF16), kv_t) + p_n.astype(BF16).astype(F32) * new_f
        return pltpu.roll(o, LANES - NSA_DH, 1)

    for g in range(NSA_KV_HEADS):
        slope = jnp.zeros((SUB, 1), F32)
        for rr in range(NSA_GROUP):
            slope = jnp.where(row == rr, 2.0 ** (-(g * NSA_GROUP + rr + 1.0)), slope)
        q8 = qn_ref[0, g]
        kv_t = jnp.concatenate([r[...] for r in blk_refs[g * N_SELECT:(g + 1) * N_SELECT]], axis=1).astype(BF16)
        nk = N_SELECT * PAGE_SIZE
        key = lax.broadcasted_iota(jnp.int32, (1, nk), 1)
        slot = lax.shift_right_logical(key, int(np.log2(PAGE_SIZE)))
        blk_in_page = jnp.bitwise_and(lax.shift_right_logical(key, int(np.log2(SLC_BLOCK))), per_page - 1)
        off = jnp.bitwise_and(key, SLC_BLOCK - 1)
        kblk = jnp.zeros((1, nk), jnp.int32)
        use_new = False
        for t in range(N_SELECT):
            st = sel_ref[(b * NSA_KV_HEADS + g) * N_SELECT + t]
            kblk = jnp.where(slot == t, st, kblk)
            use_new = jnp.logical_or(use_new, st == n_past_blocks)
        kpos = kblk * SLC_BLOCK + off
        ok = jnp.logical_and(jnp.logical_and(kblk >= 0, kblk < n_past_blocks),
                             jnp.logical_and(blk_in_page == jnp.bitwise_and(kblk, per_page - 1), kpos <= qpos))
        dist = (qpos - kpos).astype(F32)
        o_s = merge_new(_dot(q8, kv_t) - slope * dist, ok, kv_t, q8, snew[:, g * LANES:(g + 1) * LANES], slope, use_new)
        kvw_t = win[g * LANES:(g + 1) * LANES, :].astype(BF16)
        wpos = p_len - nw + lax.broadcasted_iota(jnp.int32, (1, nw), 1)
        dist_w = (qpos - wpos).astype(F32)
        mask_w = jnp.logical_and(jnp.logical_and(dist_w >= 0, dist_w <= WINDOW), wpos >= 0)
        o_w = merge_new(_dot(q8, kvw_t) - slope * dist_w, mask_w, kvw_t, q8, wnew[:, g * LANES:(g + 1) * LANES], slope, True)
        hd = g * NSA_GROUP + row
        gate = lambda c: jnp.sum(jnp.where(lane == 3 * hd + c, gates, 0.0), axis=1, keepdims=True)
        mix = gate(0) * oc_ref[0, g] + gate(1) * o_s + gate(2) * o_w
        o_ref[0, g * NSA_GROUP:(g + 1) * NSA_GROUP, :] = mix[0:NSA_GROUP]
    req = lax.broadcasted_iota(jnp.int32, wnewt_ref.shape, 1)
    col = jnp.sum(jnp.where(req == b, wnewt_ref[...], 0.0), axis=1, keepdims=True)
    t_idx = lax.broadcasted_iota(jnp.int32, (1, nw), 1)
    nwin_ref[0] = jnp.where(t_idx == nw - 1, col, pltpu.roll(win, nw - 1, 1))


def _sel_decode(qn_dec, gl, o_c, sel_idx, slc_new, win_new, slc_pool_t, win_buf_t, page_table):
    db, n_pages = page_table.shape
    p_len = n_pages * PAGE_SIZE
    n_past_blocks = p_len // SLC_BLOCK
    per_page = PAGE_SIZE // SLC_BLOCK
    nw = win_buf_t.shape[2]
    win_new_t = jnp.transpose(win_new)

    def blk_spec(g, t):
        def imap(b, pt, sel):
            j = jnp.clip(sel[(b * NSA_KV_HEADS + g) * N_SELECT + t], 0, n_past_blocks - 1)
            return (pt[b * n_pages + j // per_page], g, 0)
        return pl.BlockSpec((None, GRP_W, PAGE_SIZE), imap)

    vec = lambda w: pl.BlockSpec((1, 1, w), lambda b, pt, sel: (b, 0, 0))
    grp = pl.BlockSpec((1, NSA_KV_HEADS, SUB, LANES), lambda b, pt, sel: (b, 0, 0, 0))
    grid_spec = pltpu.PrefetchScalarGridSpec(
        num_scalar_prefetch=2,
        grid=(db,),
        in_specs=[grp, vec(LANES), grp, vec(NSA_KV_W), vec(NSA_KV_W),
                  pl.BlockSpec(win_new_t.shape, lambda b, pt, sel: (0, 0)),
                  pl.BlockSpec((1, NSA_KV_W, nw), lambda b, pt, sel: (b, 0, 0))]
                 + [blk_spec(g, t) for g in range(NSA_KV_HEADS) for t in range(N_SELECT)],
        out_specs=(pl.BlockSpec((1, NSA_HEADS, LANES), lambda b, pt, sel: (b, 0, 0)),
                   pl.BlockSpec((1, NSA_KV_W, nw), lambda b, pt, sel: (b, 0, 0))),
    )
    r3 = lambda a: a.reshape(db, 1, a.shape[-1])
    return pl.pallas_call(
        functools.partial(_sel_decode_kernel, p_len=p_len),
        grid_spec=grid_spec,
        out_shape=(jax.ShapeDtypeStruct((db, NSA_HEADS, LANES), F32),
                   jax.ShapeDtypeStruct(win_buf_t.shape, win_buf_t.dtype)),
        compiler_params=_cparams(("parallel",)),
        name="nsa_sel_decode",
    )(page_table.reshape(-1), sel_idx, qn_dec, r3(gl), o_c, r3(slc_new), r3(win_new), win_new_t, win_buf_t,
      *([slc_pool_t] * (NSA_KV_HEADS * N_SELECT)))


TM_PROMPT = 256
TQ_MLA, TK_MLA = 256, 512
TQ_NSA, TK_NSA = 128, 512


def _mla_rows(rows):
    return jnp.concatenate([rows[:, :MLA_KV_RANK], rows[:, MLA_KV_RANK:MLA_KV_RANK + HALF_ROPE],
                            rows[:, MLA_KV_RANK + LANES:MLA_KV_RANK + LANES + HALF_ROPE]], axis=1)


def kernel(x_prompt, x_sample, cache_mla, cache_nsa_cmp, cache_nsa_slc, state_nsa_win, state_ret, page_table,
           norm_even, w_in_even, mla_gq, mla_gkv, mla_wuq, mla_wuk, mla_wuv, nsa_cmp_w, nsa_cmp_pe, w_out_even,
           norm_odd, w_in_odd, ret_gn, w_out_odd, final_norm):
    b, s, d = x_prompt.shape
    db, n_new, _ = x_sample.shape
    assert n_new == 1 and norm_even.shape[0] == 1 and norm_odd.shape[0] == 1
    n_pages = page_table.shape[1]
    p_len = n_pages * PAGE_SIZE
    kv_row = (NSA_KV_HEADS, 2, NSA_DH)
    tm = min(TM_PROMPT, b * s)

    w_ext, wuq_ext, wuk_ext, wuv_ext, wo_ext = _prep_even_weights(
        w_in_even[0], mla_wuq[0], mla_wuk[0], mla_wuv[0], w_out_even[0])
    w_big, pe_big = _prep_cmp_weights(nsa_cmp_w[0], nsa_cmp_pe[0])
    w_odd = w_in_odd[0].astype(BF16)
    wo_odd = w_out_odd[0].astype(BF16)
    gn_e, gq, gkv = norm_even[0][None], mla_gq[0][None], mla_gkv[0][None]
    gn_o, fn = norm_odd[0][None], final_norm[None]

    xp = x_prompt.reshape(b * s, d)
    pos_p = jnp.arange(s)
    cos_m, sin_m = _rope_tables_mla(pos_p)
    (qext, rows, kext, gmla, qn, cmp, slc, slcp, win, winp, gl, gnsa) = _even_project(
        xp, gn_e, w_ext, gq, gkv, wuq_ext, wuk_ext, cos_m, sin_m, tm)
    lat = _mla_prompt_attention(qext, kext, b, s, min(TQ_MLA, s), min(TK_MLA, s))
    kvc = _cmp_prompt(cmp, w_big, pe_big, b, s)
    onsa = _nsa_prompt(qn, gl, kvc, slcp, winp, b, s, TQ_NSA, min(TK_NSA, s))
    x1 = _even_output(xp, lat, gmla, onsa, gnsa, wuv_ext, wo_ext, tm)
    cos_r, sin_r = _rope_tables_ret(pos_p)
    q, k, v, g = _odd_project(x1, gn_o, w_odd, cos_r, sin_r, tm)
    y, ret_p = _retention_prompt(q, k, v, g, ret_gn[0], b, s)
    y_prompt = _odd_output(x1, y, wo_odd, fn, tm).reshape(b, s, d)
    nwin = min(WINDOW, s)
    mla_p = _mla_rows(rows).reshape(1, b, s, MLA_KV_RANK + MLA_ROPE)
    cmp_p = cmp.reshape((1, b, s) + kv_row)
    slc_p = slc.reshape((1, b, s) + kv_row)
    win_p = win.reshape((b, s) + kv_row)[:, s - nwin:][None]

    xs = x_sample.reshape(db, d)
    pos_s = p_len + jnp.arange(n_new)
    cos_s, sin_s = [jnp.broadcast_to(t, (db, LANES)) for t in _rope_tables_mla(pos_s)]
    (qext_s, rows_s, _, gmla_s, qn_s, cmp_s, slc_s, _, win_s, _, gl_s, gnsa_s) = _even_project(
        xs, gn_e, w_ext, gq, gkv, wuq_ext, wuk_ext, cos_s, sin_s, db)
    rows_new = _mla_rows(rows_s)
    rope = lambda a: a.astype(F32).reshape(MLA_HEADS, db, MLA_HEADS, HALF_ROPE).sum(2).astype(BF16)
    q_dec = jnp.concatenate([qext_s[:, :, :MLA_KV_RANK], rope(qext_s[:, :, MLA_KV_RANK:MLA_KV_RANK + LANES]),
                             rope(qext_s[:, :, MLA_KV_RANK + LANES:])], axis=-1)
    pool = cache_nsa_cmp.shape[1]
    feat_major = lambda a, n: jnp.transpose(a[0], (0, 2, 3, 4, 1)).reshape(n, NSA_KV_W, a.shape[2])
    mla_pool_t = jnp.transpose(cache_mla[0], (0, 2, 1))
    lat_s = _mla_decode(jnp.transpose(q_dec, (1, 0, 2)), rows_new, mla_pool_t, page_table)
    qn_dec = jnp.pad(qn_s.reshape(db, NSA_KV_HEADS, NSA_GROUP, LANES), ((0, 0), (0, 0), (0, SUB - NSA_GROUP), (0, 0)))
    w_pair, pe_pair = _prep_cmp_pair_weights(nsa_cmp_w[0], nsa_cmp_pe[0])
    o_c, s_slc = _cmp_decode(qn_dec, cmp_s, feat_major(cache_nsa_cmp, pool), page_table, w_pair,
                             _cmp_bias(w_pair, pe_pair))
    sel = _topk_decode(s_slc[:, :, 0, :].reshape(db * NSA_KV_HEADS, -1), p_len)
    sel_idx = sel[:, :N_SELECT].reshape(-1)
    onsa_s, win_new_t = _sel_decode(qn_dec, gl_s, o_c, sel_idx, slc_s, win_s, feat_major(cache_nsa_slc, pool),
                                    feat_major(state_nsa_win, db), page_table)
    nw = win_new_t.shape[2]
    win_so = jnp.transpose(win_new_t.reshape((db,) + kv_row + (nw,)), (0, 4, 1, 2, 3))[None]
    x1s = _even_output(xs, jnp.transpose(lat_s, (1, 0, 2)), gmla_s, onsa_s.reshape(db, NSA_HEADS * LANES), gnsa_s,
                       wuv_ext, wo_ext, db)
    cos_rs, sin_rs = [jnp.broadcast_to(t, (db, RET_DK)) for t in _rope_tables_ret(pos_s)]
    q, k, v, g = _odd_project(x1s, gn_o, w_odd, cos_rs, sin_rs, db)
    ys, ret_s = _retention_decode(q, k, v, g, ret_gn[0], state_ret.reshape(state_ret.shape[1:]))
    y_sample = _odd_output(x1s, ys, wo_odd, fn, db).reshape(db, n_new, d)
    mla_s = rows_new.reshape(1, db, n_new, MLA_KV_RANK + MLA_ROPE)
    cmp_so = cmp_s.reshape((1, db, n_new) + kv_row)
    slc_so = slc_s.reshape((1, db, n_new) + kv_row)
    return (y_prompt, y_sample, mla_p, cmp_p, slc_p, win_p, ret_p[None],
            mla_s, cmp_so, slc_so, win_so, ret_s[None])
```

```python
import functools

import numpy as np
import jax
import jax.numpy as jnp
from jax import lax
from jax.experimental import pallas as pl
from jax.experimental.pallas import tpu as pltpu

F32 = jnp.float32
BF16 = jnp.bfloat16

D_MODEL = 1024
PAGE_SIZE = 128
MLA_HEADS = 8
MLA_NOPE = 64
MLA_ROPE = 32
MLA_V = 64
MLA_Q_RANK = 768
MLA_KV_RANK = 256
MLA_WIDTH = MLA_HEADS * MLA_V
MLA_SCALE = (MLA_NOPE + MLA_ROPE) ** -0.5
NSA_HEADS = 8
NSA_KV_HEADS = 2
NSA_GROUP = NSA_HEADS // NSA_KV_HEADS
NSA_DH = 64
NSA_WIDTH = NSA_HEADS * NSA_DH
NSA_KV_W = NSA_KV_HEADS * 2 * NSA_DH
CMP_BLOCK = 32
CMP_STRIDE = 16
SLC_BLOCK = 64
N_SELECT = 16
WINDOW = 512
FORCE_SCORE = 1e4
RET_HEADS = 8
RET_DK = 128
RET_DV = 256
RET_WIDTH = RET_HEADS * RET_DV
RET_CHUNK = 128
ROPE_BASE = 10000.0
EPS = 1e-6
EVEN_SPLITS = (MLA_Q_RANK, MLA_KV_RANK, MLA_ROPE, MLA_WIDTH, NSA_WIDTH, NSA_KV_W, NSA_KV_W, NSA_KV_W,
               3 * NSA_HEADS, NSA_WIDTH)

LANES = 128
VMEM_LIMIT_BYTES = 56 * 1024 * 1024
NEG_BIG = -1e30
MASK_BIG = 16384.0

HALF_ROPE = MLA_ROPE // 2


def _cparams(sem):
    return pltpu.CompilerParams(dimension_semantics=sem, vmem_limit_bytes=VMEM_LIMIT_BYTES)


def _full_spec(a):
    nd = a.ndim
    return pl.BlockSpec(a.shape, lambda *_: (0,) * nd)


def _rms(x, g):
    y = x * lax.rsqrt(jnp.mean(x * x, axis=-1, keepdims=True) + EPS)
    return y * g


def _dot(a, b):
    return jnp.dot(a, b, preferred_element_type=F32)


def _dot_nt(a, b):
    return lax.dot_general(a, b, (((1,), (1,)), ((), ())), preferred_element_type=F32)


def _fold_lanes(x, op):
    n = x.shape[1]
    if n % LANES:
        return x
    parts = [x[:, i:i + LANES] for i in range(0, n, LANES)]
    while len(parts) > 1:
        parts = [op(parts[i], parts[i + 1]) if i + 1 < len(parts) else parts[i] for i in range(0, len(parts), 2)]
    return parts[0]


def _row_max(x):
    return jnp.max(_fold_lanes(x, jnp.maximum), axis=1, keepdims=True)


def _row_sum(x):
    return jnp.sum(_fold_lanes(x, jnp.add), axis=1, keepdims=True)


def _rep(col):
    return jnp.broadcast_to(col, (col.shape[0], LANES))


def _tile_lanes(x, n):
    if n % LANES:
        return jnp.broadcast_to(x[:, :1], (x.shape[0], n))
    return x if n == LANES else jnp.tile(x, (1, n // LANES))


_EVEN_GROUPS = (
    ("cq", MLA_Q_RANK), ("ckv", MLA_KV_RANK), ("kr1", LANES), ("kr2", LANES),
    ("gmla", MLA_HEADS * LANES), ("qn", NSA_HEADS * LANES), ("cmp", NSA_KV_W),
    ("slc", NSA_KV_W), ("slcp", 4 * LANES), ("win", NSA_KV_W), ("winp", 4 * LANES),
    ("gl", LANES), ("gnsa", NSA_HEADS * LANES),
)
_EVEN_OFF = {}
_o = 0
for _n, _w in _EVEN_GROUPS:
    _EVEN_OFF[_n] = (_o, _o + _w)
    _o += _w
EVEN_EXT = _o


def _pad_heads(w, nh, dh):
    k = w.shape[0]
    w = w.reshape(k, nh, dh)
    return jnp.pad(w, ((0, 0), (0, 0), (0, LANES - dh))).reshape(k, nh * LANES)


def _kv_pad(w):
    k = w.shape[0]
    w4 = w.reshape(k, NSA_KV_HEADS, 2, NSA_DH)
    w4 = jnp.transpose(w4, (0, 2, 1, 3))
    return jnp.pad(w4, ((0, 0), (0, 0), (0, 0), (0, LANES - NSA_DH))).reshape(k, 4 * LANES)


def _prep_even_weights(w_in, wuq, wuk, wuv, w_out):
    offs = np.cumsum((0,) + EVEN_SPLITS)
    cq, ckv, kr, g_mla, q_nsa, kv_cmp, kv_slc, kv_win, gl, g_nsa = [
        w_in[:, offs[i]:offs[i + 1]] for i in range(len(EVEN_SPLITS))]
    parts = {
        "cq": cq, "ckv": ckv,
        "kr1": jnp.tile(kr[:, :HALF_ROPE], (1, MLA_HEADS)),
        "kr2": jnp.tile(kr[:, HALF_ROPE:], (1, MLA_HEADS)),
        "gmla": _pad_heads(g_mla, MLA_HEADS, MLA_V),
        "qn": _pad_heads(q_nsa, NSA_HEADS, NSA_DH),
        "cmp": kv_cmp, "slc": kv_slc, "slcp": _kv_pad(kv_slc),
        "win": kv_win, "winp": _kv_pad(kv_win),
        "gl": jnp.pad(gl, ((0, 0), (0, LANES - gl.shape[1]))),
        "gnsa": _pad_heads(g_nsa, NSA_HEADS, NSA_DH),
    }
    w_ext = jnp.concatenate([parts[n] for n, _ in _EVEN_GROUPS], axis=1).astype(BF16)
    c = wuq.shape[0]
    nope = jnp.pad(wuq[:, :, :MLA_NOPE], ((0, 0), (0, 0), (0, LANES - MLA_NOPE))).reshape(c, MLA_HEADS * LANES)
    r1 = wuq[:, :, MLA_NOPE:MLA_NOPE + HALF_ROPE].reshape(c, MLA_HEADS * HALF_ROPE)
    r2 = wuq[:, :, MLA_NOPE + HALF_ROPE:].reshape(c, MLA_HEADS * HALF_ROPE)
    wuq_ext = jnp.concatenate([nope, r1, r2], axis=1).astype(BF16)
    wuk_ext = jnp.pad(jnp.transpose(wuk, (1, 2, 0)), ((0, 0), (0, LANES - MLA_NOPE), (0, 0))).astype(BF16)
    wuv_ext = jnp.pad(jnp.transpose(wuv, (1, 0, 2)), ((0, 0), (0, 0), (0, LANES - MLA_V))).astype(BF16)
    d = w_out.shape[1]
    wo = w_out.reshape(MLA_HEADS + NSA_HEADS, MLA_V, d)
    wo_ext = jnp.pad(wo, ((0, 0), (0, LANES - MLA_V), (0, 0))).reshape((MLA_HEADS + NSA_HEADS) * LANES, d).astype(BF16)
    return w_ext, wuq_ext, wuk_ext, wuv_ext, wo_ext


def _rope_tables_mla(pos):
    inv = ROPE_BASE ** (-jnp.arange(HALF_ROPE, dtype=F32) / HALF_ROPE)
    ang = pos.astype(F32)[:, None] * inv
    return jnp.tile(jnp.cos(ang), (1, LANES // HALF_ROPE)), jnp.tile(jnp.sin(ang), (1, LANES // HALF_ROPE))


def _even_proj_kernel(x_ref, gn_ref, w_ref, gq_ref, gkv_ref, wuq_ref, wuk_ref, cos_ref, sin_ref,
                      qext_ref, rows_ref, kext_ref, gmla_ref, qn_ref, cmp_ref, slc_ref, slcp_ref,
                      win_ref, winp_ref, gl_ref, gnsa_ref):
    h = _rms(x_ref[...], gn_ref[...]).astype(BF16)

    def proj(name):
        a, b = _EVEN_OFF[name]
        return _dot(h, w_ref[:, a:b])

    cos = cos_ref[...]
    sin = sin_ref[...]
    cqn = _rms(proj("cq"), gq_ref[...]).astype(BF16)
    nh = MLA_HEADS * LANES
    r1 = _dot(cqn, wuq_ref[:, nh:nh + LANES])
    r2 = _dot(cqn, wuq_ref[:, nh + LANES:nh + 2 * LANES])
    o1 = r1 * cos - r2 * sin
    o2 = r1 * sin + r2 * cos
    head_of_lane = lax.broadcasted_iota(jnp.int32, (1, LANES), 1) // HALF_ROPE
    for hd in range(MLA_HEADS):
        nope = _dot(cqn, wuq_ref[:, hd * LANES:(hd + 1) * LANES]).astype(BF16)
        qext_ref[hd, :, 0:MLA_KV_RANK] = _dot(nope, wuk_ref[hd]).astype(BF16)
        sel = head_of_lane == hd
        qext_ref[hd, :, MLA_KV_RANK:MLA_KV_RANK + LANES] = jnp.where(sel, o1, 0.0).astype(BF16)
        qext_ref[hd, :, MLA_KV_RANK + LANES:] = jnp.where(sel, o2, 0.0).astype(BF16)
    latn = _rms(proj("ckv"), gkv_ref[...])
    kr1 = proj("kr1")
    kr2 = proj("kr2")
    k1 = kr1 * cos - kr2 * sin
    k2 = kr1 * sin + kr2 * cos
    rows_ref[:, 0:MLA_KV_RANK] = latn
    rows_ref[:, MLA_KV_RANK:MLA_KV_RANK + LANES] = k1
    rows_ref[:, MLA_KV_RANK + LANES:] = k2
    kext_ref[:, 0:MLA_KV_RANK] = latn.astype(BF16)
    kext_ref[:, MLA_KV_RANK:MLA_KV_RANK + LANES] = k1.astype(BF16)
    kext_ref[:, MLA_KV_RANK + LANES:] = k2.astype(BF16)
    gmla_ref[...] = proj("gmla")
    qn_ref[...] = (proj("qn") * (NSA_DH ** -0.5)).astype(BF16)
    cmp_ref[...] = proj("cmp")
    slc_ref[...] = proj("slc")
    slcp_ref[...] = proj("slcp").astype(BF16)
    win_ref[...] = proj("win")
    winp_ref[...] = proj("winp").astype(BF16)
    gl_ref[...] = proj("gl")
    gnsa_ref[...] = proj("gnsa")


def _even_project(x, gn, w_ext, gq, gkv, wuq_ext, wuk_ext, cos, sin, tm):
    t = x.shape[0]
    nt = t // tm
    ntab = cos.shape[0] // tm
    row = lambda w: pl.BlockSpec((tm, w), lambda i: (i, 0))
    tab = pl.BlockSpec((tm, LANES), lambda i: (i % ntab, 0))
    kext_w = MLA_KV_RANK + 2 * LANES
    out_shapes = (
        jax.ShapeDtypeStruct((MLA_HEADS, t, kext_w), BF16),
        jax.ShapeDtypeStruct((t, kext_w), F32),
        jax.ShapeDtypeStruct((t, kext_w), BF16),
        jax.ShapeDtypeStruct((t, MLA_HEADS * LANES), F32),
        jax.ShapeDtypeStruct((t, NSA_HEADS * LANES), BF16),
        jax.ShapeDtypeStruct((t, NSA_KV_W), F32),
        jax.ShapeDtypeStruct((t, NSA_KV_W), F32),
        jax.ShapeDtypeStruct((t, 4 * LANES), BF16),
        jax.ShapeDtypeStruct((t, NSA_KV_W), F32),
        jax.ShapeDtypeStruct((t, 4 * LANES), BF16),
        jax.ShapeDtypeStruct((t, LANES), F32),
        jax.ShapeDtypeStruct((t, NSA_HEADS * LANES), F32),
    )
    out_specs = (
        pl.BlockSpec((MLA_HEADS, tm, kext_w), lambda i: (0, i, 0)),
        row(kext_w), row(kext_w), row(MLA_HEADS * LANES), row(NSA_HEADS * LANES), row(NSA_KV_W),
        row(NSA_KV_W), row(4 * LANES), row(NSA_KV_W), row(4 * LANES), row(LANES), row(NSA_HEADS * LANES),
    )
    return pl.pallas_call(
        _even_proj_kernel,
        grid=(nt,),
        in_specs=[row(D_MODEL), _full_spec(gn), _full_spec(w_ext), _full_spec(gq), _full_spec(gkv),
                  _full_spec(wuq_ext), _full_spec(wuk_ext), tab, tab],
        out_specs=out_specs,
        out_shape=out_shapes,
        compiler_params=_cparams(("parallel",)),
        name="even_proj",
    )(x, gn, w_ext, gq, gkv, wuq_ext, wuk_ext, cos, sin)


LOG2E = float(np.log2(np.e))


def _mla_attn_kernel(q_ref, k_ref, kn_ref, o_ref, m_sc, l_sc, acc_sc, sa_sc, sb_sc, *, tq, tk):
    qi = pl.program_id(1)
    ki = pl.program_id(2)
    n_needed = ((qi + 1) * tq + tk - 1) // tk
    m_rows = MLA_HEADS * tq
    c_exp = MLA_SCALE * LOG2E

    def scores(kref):
        return _dot_nt(q_ref[...].reshape(m_rows, q_ref.shape[-1]), kref[...])

    @pl.when(ki == 0)
    def _():
        m_sc[...] = jnp.full(m_sc.shape, NEG_BIG, F32)
        l_sc[...] = jnp.zeros(l_sc.shape, F32)
        acc_sc[...] = jnp.zeros(acc_sc.shape, F32)
        sa_sc[...] = scores(k_ref)

    def step(masked, cur_sc, nxt_sc):
        nxt_sc[...] = scores(kn_ref)
        s = cur_sc[...]
        if masked:
            qpos = qi * tq + jnp.bitwise_and(lax.broadcasted_iota(jnp.int32, (m_rows, 1), 0), tq - 1)
            s = jnp.where((ki * tk + lax.broadcasted_iota(jnp.int32, (1, tk), 1)) <= qpos, s, NEG_BIG)
        m_prev = m_sc[...]
        m_new = jnp.maximum(m_prev, _rep(_row_max(s)))
        alpha = jnp.exp2((m_prev - m_new) * c_exp)
        p = jnp.exp2((s - _tile_lanes(m_new, tk)) * c_exp)
        l_sc[...] = alpha * l_sc[...] + _fold_lanes(p, jnp.add)
        acc_sc[...] = _tile_lanes(alpha, MLA_KV_RANK) * acc_sc[...] + _dot(p.astype(BF16), k_ref[:, :MLA_KV_RANK])
        m_sc[...] = m_new

    crosses = (ki + 1) * tk - 1 > qi * tq
    even = lax.rem(ki, 2) == 0
    for par, (cur_sc, nxt_sc) in enumerate(((sa_sc, sb_sc), (sb_sc, sa_sc))):
        for masked in (True, False):
            cond = jnp.logical_and(ki < n_needed, jnp.logical_and(even == (par == 0), crosses == masked))
            pl.when(cond)(functools.partial(step, masked, cur_sc, nxt_sc))

    @pl.when(ki == n_needed - 1)
    def _():
        out = acc_sc[...] / jnp.maximum(jnp.sum(l_sc[...], axis=1, keepdims=True), 1e-30)
        o_ref[...] = out.reshape(MLA_HEADS, tq, MLA_KV_RANK).astype(o_ref.dtype)


def _mla_prompt_attention(q_ext, k_ext, batch, seq, tq, tk):
    nq, nk = seq // tq, seq // tk
    kw = q_ext.shape[-1]

    def k_map(off):
        def imap(b, qi, ki):
            last = ((qi + 1) * tq + tk - 1) // tk - 1
            return (b * nk + jnp.minimum(ki + off, last), 0)
        return imap

    return pl.pallas_call(
        functools.partial(_mla_attn_kernel, tq=tq, tk=tk),
        grid=(batch, nq, nk),
        in_specs=[pl.BlockSpec((MLA_HEADS, tq, kw), lambda b, qi, ki: (0, b * nq + qi, 0)),
                  pl.BlockSpec((tk, kw), k_map(0)), pl.BlockSpec((tk, kw), k_map(1))],
        out_specs=pl.BlockSpec((MLA_HEADS, tq, MLA_KV_RANK), lambda b, qi, ki: (0, b * nq + qi, 0)),
        out_shape=jax.ShapeDtypeStruct((MLA_HEADS, batch * seq, MLA_KV_RANK), BF16),
        scratch_shapes=[pltpu.VMEM((MLA_HEADS * tq, LANES), F32), pltpu.VMEM((MLA_HEADS * tq, LANES), F32),
                        pltpu.VMEM((MLA_HEADS * tq, MLA_KV_RANK), F32),
                        pltpu.VMEM((MLA_HEADS * tq, tk), F32), pltpu.VMEM((MLA_HEADS * tq, tk), F32)],
        compiler_params=_cparams(("parallel", "parallel", "arbitrary")),
        name="mla_prompt_attn",
    )(q_ext, k_ext, k_ext)


SEG_W = CMP_STRIDE * NSA_KV_W
KVP_W = 4 * LANES


def _prep_cmp_weights(w_cmp, pe_cmp):
    wh = w_cmp.reshape(2, 2, CMP_STRIDE, NSA_DH, NSA_DH)
    t = jnp.transpose(jnp.pad(wh, ((0, 0),) * 4 + ((0, LANES - NSA_DH),)), (1, 2, 0, 3, 4)).astype(BF16)
    eye_c = np.eye(2, dtype=np.float32).reshape(1, 1, 1, 2, 1, 2, 1, 1)
    eye_g = np.eye(NSA_KV_HEADS, dtype=np.float32).reshape(1, 1, NSA_KV_HEADS, 1, 1, 1, NSA_KV_HEADS, 1)
    mask = jnp.asarray(eye_c * eye_g, BF16)
    big = t[:, :, None, :, :, None, None, :] * mask
    w_big = big.reshape(2, SEG_W, KVP_W)
    peh = pe_cmp.reshape(2, 2, CMP_STRIDE, NSA_DH)
    pe_big = jnp.broadcast_to(jnp.transpose(peh, (1, 2, 0, 3))[:, :, None], (2, CMP_STRIDE, NSA_KV_HEADS, 2, NSA_DH))
    return w_big, pe_big.reshape(2, 1, SEG_W)


def _slc_overlap_matrix(n_cmp_rows, n_lanes):
    ratio = SLC_BLOCK // CMP_STRIDE
    lead = CMP_BLOCK // CMP_STRIDE - 1
    a = np.zeros((n_cmp_rows, n_lanes), np.float32)
    for j in range(n_lanes):
        for o in range(-lead, ratio):
            n = ratio * j + o
            if 0 <= n < n_cmp_rows:
                a[n, j] = (min(CMP_STRIDE * o + CMP_BLOCK, SLC_BLOCK) - max(CMP_STRIDE * o, 0)) / CMP_BLOCK
    return a


def _key_position_features(n_keys, n_lanes):
    kc = np.zeros((n_keys, LANES + n_lanes), np.float32)
    pos = np.arange(n_keys)
    kc[:, NSA_DH] = (pos // SLC_BLOCK) * SLC_BLOCK
    kc[:, NSA_DH + 1] = pos % SLC_BLOCK
    kc[pos, LANES + pos // SLC_BLOCK] = 1.0
    return kc


def _slopes_col(g, rows, tq):
    r = lax.shift_right_logical(lax.broadcasted_iota(jnp.int32, (rows, 1), 0), int(np.log2(tq)))
    col = jnp.zeros((rows, 1), F32)
    for rr in range(NSA_GROUP):
        col = jnp.where(r == rr, 2.0 ** (-(g * NSA_GROUP + rr + 1.0)), col)
    return col


def _softmax_rows(s, mask):
    n = s.shape[1]
    s = jnp.where(mask, s, NEG_BIG)
    m = _rep(_row_max(s))
    e = jnp.where(mask, jnp.exp(s - _tile_lanes(m, n)), 0.0)
    inv = 1.0 / jnp.maximum(_rep(_row_sum(e)), 1e-30)
    return e * _tile_lanes(inv, n)


def _top_blocks(score, n_pick):
    nb = score.shape[0]
    blk = lax.broadcasted_iota(jnp.int32, score.shape, 0).astype(F32)
    sel = jnp.zeros(score.shape, F32)
    for _ in range(n_pick):
        m = jnp.max(score, axis=0, keepdims=True)
        idx = jnp.min(jnp.where(score == m, blk, float(nb)), axis=0, keepdims=True)
        hit = blk == idx
        sel = jnp.where(jnp.logical_and(hit, m > 0.5 * NEG_BIG), 1.0, sel)
        score = jnp.where(hit, 2.0 * NEG_BIG, score)
    return sel


def _split3(x):
    hi = x.astype(BF16)
    r1 = x - hi.astype(F32)
    mid = r1.astype(BF16)
    lo = (r1 - mid.astype(F32)).astype(BF16)
    return hi, mid, lo


def _cmp_prompt_kernel(x_ref, pe_ref, w_ref, kvc_ref):
    x = x_ref[...]
    h1 = _dot((x + pe_ref[0]).astype(BF16), w_ref[0])
    h2 = _dot((x + pe_ref[1]).astype(BF16), w_ref[1])
    n = x.shape[0]
    kvc_ref[...] = (h1 + pltpu.roll(h2, n - 1, 0)).astype(kvc_ref.dtype)


def _cmp_prompt(kv_cmp, w_big, pe_big, batch, seq):
    nseg = seq // CMP_STRIDE
    x = kv_cmp.reshape(batch * nseg, SEG_W)
    return pl.pallas_call(
        _cmp_prompt_kernel,
        grid=(batch,),
        in_specs=[pl.BlockSpec((nseg, SEG_W), lambda b: (b, 0)), _full_spec(pe_big), _full_spec(w_big)],
        out_specs=pl.BlockSpec((nseg, KVP_W), lambda b: (b, 0)),
        out_shape=jax.ShapeDtypeStruct((batch * nseg, KVP_W), BF16),
        compiler_params=_cparams(("parallel",)),
        name="nsa_cmp_prompt",
    )(x, pe_big, w_big)


N_WIN_BLOCKS = WINDOW // 128 + 1


def _nsa_prompt_kernel(qn_ref, gl_ref, kvc_ref, amat_ref, slcp_ref, kc_ref, *rest, tq, tk, nseg):
    win_refs = rest[:N_WIN_BLOCKS]
    o_ref = rest[N_WIN_BLOCKS]
    stat = rest[N_WIN_BLOCKS + 1:N_WIN_BLOCKS + 1 + 3 * NSA_KV_HEADS]
    flag_ref = rest[-1]
    nt_all = slcp_ref.shape[0] // tk
    qi = pl.program_id(1)
    s0 = qi * tq
    rows = NSA_GROUP * tq
    nl = amat_ref.shape[1]
    qpos = s0 + jnp.bitwise_and(lax.broadcasted_iota(jnp.int32, (rows, 1), 0), tq - 1)
    qpos_t = s0 + lax.broadcasted_iota(jnp.int32, (tq, 1), 0)
    gates = jax.nn.sigmoid(gl_ref[...])
    n_tiles = (s0 + tq + tk - 1) // tk
    lane = lax.broadcasted_iota(jnp.int32, (1, LANES), 1)
    is_pos_lane = jnp.logical_or(lane == NSA_DH, lane == NSA_DH + 1)
    blk = lax.broadcasted_iota(jnp.int32, (1, nl), 1)
    cur = lax.shift_right_logical(qpos_t, int(np.log2(SLC_BLOCK)))
    valid = blk * SLC_BLOCK <= qpos_t
    forced = jnp.logical_or(blk == 0, jnp.logical_or(blk == cur, blk == cur - 1))
    cpos = lax.broadcasted_iota(jnp.int32, (1, nseg), 1) * CMP_STRIDE + (CMP_BLOCK - 1)
    dist_c = (qpos - cpos).astype(F32)
    amat = amat_ref[...]

    slopes, qgs, o_cs, q_augs = [], [], [], []
    any_sel = jnp.zeros((1, nl), F32)
    for g in range(NSA_KV_HEADS):
        slope = _slopes_col(g, rows, tq)
        q_heads = [qn_ref[:, (g * NSA_GROUP + r) * LANES:(g * NSA_GROUP + r + 1) * LANES] for r in range(NSA_GROUP)]
        qg = jnp.concatenate(q_heads, axis=0)
        kc = kvc_ref[:, g * LANES:(g + 1) * LANES]
        vc = kvc_ref[:, (2 + g) * LANES:(3 + g) * LANES]
        p_c = _softmax_rows(_dot_nt(qg, kc) - slope * dist_c, dist_c >= 0)
        o_cs.append(_dot(p_c.astype(BF16), vc))
        p_grp = p_c[0:tq]
        for r in range(1, NSA_GROUP):
            p_grp = p_grp + p_c[r * tq:(r + 1) * tq]
        s_slc = sum(_dot(t, amat) for t in _split3(p_grp))
        score = jnp.where(forced, FORCE_SCORE, jnp.where(valid, s_slc, NEG_BIG))
        sel = _top_blocks(score.T, N_SELECT).T
        selneg = ((sel - 1.0) * MASK_BIG).astype(BF16)
        any_sel = jnp.maximum(any_sel, jnp.max(sel, axis=0, keepdims=True))
        q_augs.append(jnp.concatenate(
            [jnp.concatenate([jnp.where(is_pos_lane, (2.0 ** (-(g * NSA_GROUP + r + 1.0))), q_heads[r].astype(F32)).astype(BF16),
                              selneg], axis=1) for r in range(NSA_GROUP)], axis=0))
        slopes.append(slope)
        qgs.append(qg)
        m_sc, l_sc, acc_sc = stat[3 * g:3 * g + 3]
        m_sc[...] = jnp.full(m_sc.shape, NEG_BIG, F32)
        l_sc[...] = jnp.zeros(l_sc.shape, F32)
        acc_sc[...] = jnp.zeros(acc_sc.shape, F32)
    tile_of_blk = lax.shift_right_logical(blk, int(np.log2(tk // SLC_BLOCK)))
    for t in range(nt_all):
        flag_ref[t] = jnp.max(jnp.where(tile_of_blk == t, any_sel, 0.0)).astype(jnp.int32)

    def slc_tile(kt, causal):
        k0 = pl.multiple_of(kt * tk, tk)
        if causal:
            ok = (k0 + lax.broadcasted_iota(jnp.int32, (1, tk), 1)) <= qpos
        for g in range(NSA_KV_HEADS):
            m_sc, l_sc, acc_sc = stat[3 * g:3 * g + 3]
            kk = slcp_ref[pl.ds(k0, tk), g * LANES:(g + 1) * LANES] + kc_ref[pl.ds(k0, tk), 0:LANES]
            k_aug = jnp.concatenate([kk, kc_ref[pl.ds(k0, tk), LANES:]], axis=1)
            vv = slcp_ref[pl.ds(k0, tk), (2 + g) * LANES:(3 + g) * LANES]
            s = _dot_nt(q_augs[g], k_aug)
            if causal:
                s = jnp.where(ok, s, NEG_BIG)
            m_prev = m_sc[...]
            m_new = jnp.maximum(m_prev, _rep(_row_max(s)))
            alpha = jnp.exp(m_prev - m_new)
            p = jnp.exp(s - _tile_lanes(m_new, tk))
            if causal:
                p = jnp.where(ok, p, 0.0)
            l_sc[...] = alpha * l_sc[...] + _fold_lanes(p, jnp.add)
            acc_sc[...] = alpha * acc_sc[...] + _dot(p.astype(BF16), vv)
            m_sc[...] = m_new

    def body(kt, carry):
        @pl.when(flag_ref[kt] > 0)
        def _():
            slc_tile(kt, False)
        return carry

    lax.fori_loop(0, n_tiles - 1, body, 0)
    slc_tile(n_tiles - 1, True)
    nw = N_WIN_BLOCKS * 128
    wpos = s0 - WINDOW + lax.broadcasted_iota(jnp.int32, (1, nw), 1)
    dist_w = (qpos - wpos).astype(F32)
    mask_w = jnp.logical_and(jnp.logical_and(dist_w >= 0, dist_w <= WINDOW), wpos >= 0)
    for g in range(NSA_KV_HEADS):
        m_sc, l_sc, acc_sc = stat[3 * g:3 * g + 3]
        o_s = acc_sc[...] / jnp.maximum(jnp.sum(l_sc[...], axis=1, keepdims=True), 1e-30)
        kw = jnp.concatenate([w[:, g * LANES:(g + 1) * LANES] for w in win_refs], axis=0)
        vw = jnp.concatenate([w[:, (2 + g) * LANES:(3 + g) * LANES] for w in win_refs], axis=0)
        p_w = _softmax_rows(_dot_nt(qgs[g], kw) - slopes[g] * dist_w, mask_w)
        o_w = _dot(p_w.astype(BF16), vw)
        for r in range(NSA_GROUP):
            hd = g * NSA_GROUP + r
            sl = slice(r * tq, (r + 1) * tq)
            o_ref[:, hd * LANES:(hd + 1) * LANES] = (
                gates[:, 3 * hd:3 * hd + 1] * o_cs[g][sl] + gates[:, 3 * hd + 1:3 * hd + 2] * o_s[sl]
                + gates[:, 3 * hd + 2:3 * hd + 3] * o_w[sl])


def _nsa_prompt(qn, gl, kvc, slcp, winp, batch, seq, tq, tk):
    assert tq == 128 and seq % tk == 0 and tk % tq == 0
    nseg = seq // CMP_STRIDE
    n_slc = seq // SLC_BLOCK
    nl = LANES * ((n_slc + LANES - 1) // LANES)
    assert nl == LANES, "selection blocks must fit one lane group"
    nq = seq // tq
    amat = jnp.asarray(_slc_overlap_matrix(nseg, nl), BF16)
    kc = jnp.asarray(_key_position_features(seq, nl), BF16)

    def win_spec(j):
        return pl.BlockSpec((128, KVP_W), lambda b, qi: (b * nq + jnp.maximum(qi - (N_WIN_BLOCKS - 1) + j, 0), 0))

    rows = NSA_GROUP * tq
    return pl.pallas_call(
        functools.partial(_nsa_prompt_kernel, tq=tq, tk=tk, nseg=nseg),
        grid=(batch, nq),
        in_specs=[pl.BlockSpec((tq, NSA_HEADS * LANES), lambda b, qi: (b * nq + qi, 0)),
                  pl.BlockSpec((tq, LANES), lambda b, qi: (b * nq + qi, 0)),
                  pl.BlockSpec((nseg, KVP_W), lambda b, qi: (b, 0)),
                  _full_spec(amat),
                  pl.BlockSpec((seq, KVP_W), lambda b, qi: (b, 0)),
                  _full_spec(kc)] + [win_spec(j) for j in range(N_WIN_BLOCKS)],
        out_specs=pl.BlockSpec((tq, NSA_HEADS * LANES), lambda b, qi: (b * nq + qi, 0)),
        out_shape=jax.ShapeDtypeStruct((batch * seq, NSA_HEADS * LANES), F32),
        scratch_shapes=[pltpu.VMEM((rows, LANES), F32), pltpu.VMEM((rows, LANES), F32), pltpu.VMEM((rows, LANES), F32)]
        * NSA_KV_HEADS + [pltpu.SMEM((seq // tk,), jnp.int32)],
        compiler_params=_cparams(("parallel", "arbitrary")),
        name="nsa_prompt",
    )(qn, gl, kvc, amat, slcp, kc, *([winp] * N_WIN_BLOCKS))


def _silu(x):
    return x * jax.nn.sigmoid(x)


def _even_out_kernel(x_ref, lat_ref, gmla_ref, onsa_ref, gnsa_ref, wuv_ref, wo_ref, o_ref):
    o_mla = jnp.concatenate([_dot(lat_ref[hd], wuv_ref[hd]) for hd in range(MLA_HEADS)], axis=1)
    a = (o_mla * _silu(gmla_ref[...])).astype(BF16)
    b = (onsa_ref[...] * _silu(gnsa_ref[...])).astype(BF16)
    nm = MLA_HEADS * LANES
    o_ref[...] = x_ref[...] + _dot(a, wo_ref[0:nm]) + _dot(b, wo_ref[nm:])


def _even_output(x, lat, gmla, onsa, gnsa, wuv_ext, wo_ext, tm):
    t = x.shape[0]
    row = lambda w: pl.BlockSpec((tm, w), lambda i: (i, 0))
    return pl.pallas_call(
        _even_out_kernel,
        grid=(t // tm,),
        in_specs=[row(D_MODEL), pl.BlockSpec((MLA_HEADS, tm, MLA_KV_RANK), lambda i: (0, i, 0)),
                  row(MLA_HEADS * LANES), row(NSA_HEADS * LANES), row(NSA_HEADS * LANES),
                  _full_spec(wuv_ext), _full_spec(wo_ext)],
        out_specs=row(D_MODEL),
        out_shape=jax.ShapeDtypeStruct((t, D_MODEL), F32),
        compiler_params=_cparams(("parallel",)),
        name="even_out",
    )(x, lat, gmla, onsa, gnsa, wuv_ext, wo_ext)


_RET_QK = RET_HEADS * RET_DK
_RET_LOG_G = [float(np.log1p(-(2.0 ** (-5.0 - h)))) for h in range(RET_HEADS)]


def _rope_tables_ret(pos):
    half = RET_DK // 2
    inv = ROPE_BASE ** (-jnp.arange(half, dtype=F32) / half)
    ang = pos.astype(F32)[:, None] * inv
    cos, sin = jnp.cos(ang), jnp.sin(ang)
    return jnp.concatenate([cos, cos], 1), jnp.concatenate([-sin, sin], 1)


def _odd_proj_kernel(x_ref, gn_ref, w_ref, c_ref, s_ref, q_ref, k_ref, v_ref, g_ref):
    h = _rms(x_ref[...], gn_ref[...]).astype(BF16)
    c = c_ref[...]
    s = s_ref[...]
    half = RET_DK // 2

    def rot(z):
        return z * c + pltpu.roll(z, half, 1) * s

    for hd in range(RET_HEADS):
        sl = slice(hd * RET_DK, (hd + 1) * RET_DK)
        q_ref[:, sl] = rot(_dot(h, w_ref[:, sl])).astype(BF16)
        ks = slice(_RET_QK + hd * RET_DK, _RET_QK + (hd + 1) * RET_DK)
        k_ref[:, sl] = (rot(_dot(h, w_ref[:, ks])) * (RET_DK ** -0.5)).astype(BF16)
    v_ref[...] = _dot(h, w_ref[:, 2 * _RET_QK:2 * _RET_QK + RET_WIDTH])
    g_ref[...] = _dot(h, w_ref[:, 2 * _RET_QK + RET_WIDTH:])


def _odd_project(x, gn, w_bf, cos, sin, tm):
    t = x.shape[0]
    ntab = cos.shape[0] // tm
    row = lambda w: pl.BlockSpec((tm, w), lambda i: (i, 0))
    tab = pl.BlockSpec((tm, RET_DK), lambda i: (i % ntab, 0))
    return pl.pallas_call(
        _odd_proj_kernel,
        grid=(t // tm,),
        in_specs=[row(D_MODEL), _full_spec(gn), _full_spec(w_bf), tab, tab],
        out_specs=(row(_RET_QK), row(_RET_QK), row(RET_WIDTH), row(RET_WIDTH)),
        out_shape=(jax.ShapeDtypeStruct((t, _RET_QK), BF16), jax.ShapeDtypeStruct((t, _RET_QK), BF16),
                   jax.ShapeDtypeStruct((t, RET_WIDTH), F32), jax.ShapeDtypeStruct((t, RET_WIDTH), F32)),
        compiler_params=_cparams(("parallel",)),
        name="odd_proj",
    )(x, gn, w_bf, cos, sin)


def _group_norm_gate(o, gn_row, g):
    mu = jnp.mean(o, axis=-1, keepdims=True)
    var = jnp.mean(jnp.square(o - mu), axis=-1, keepdims=True)
    return _silu(g) * ((o - mu) * lax.rsqrt(var + EPS) * gn_row)


def _ret_prompt_kernel(q_ref, k_ref, v_ref, g_ref, gn_ref, y_ref, sfin_ref, s_sc, *, chunk):
    ci = pl.program_id(1)

    @pl.when(ci == 0)
    def _():
        s_sc[...] = jnp.zeros(s_sc.shape, F32)

    n_col = lax.broadcasted_iota(jnp.int32, (chunk, 1), 0).astype(F32)
    diff = n_col - lax.broadcasted_iota(jnp.int32, (1, chunk), 1).astype(F32)
    for hd in range(RET_HEADS):
        lg = _RET_LOG_G[hd]
        dmat = jnp.where(diff >= 0, jnp.exp(lg * jnp.maximum(diff, 0.0)), 0.0)
        xi = jnp.exp(lg * (n_col + 1.0))
        zeta = jnp.exp(lg * (chunk - 1.0 - n_col))
        qh = q_ref[:, hd * RET_DK:(hd + 1) * RET_DK]
        kh = k_ref[:, hd * RET_DK:(hd + 1) * RET_DK]
        vh = v_ref[:, hd * RET_DV:(hd + 1) * RET_DV]
        s_prev = s_sc[hd]
        inner = _dot_nt(qh, kh) * dmat
        o = _dot(inner.astype(BF16), vh.astype(BF16)) + _dot(qh, s_prev.astype(BF16)) * xi
        kv = lax.dot_general(kh, (vh * zeta).astype(BF16), (((0,), (0,)), ((), ())), preferred_element_type=F32)
        s_sc[hd] = float(np.exp(lg * chunk)) * s_prev + kv
        y_ref[:, hd * RET_DV:(hd + 1) * RET_DV] = _group_norm_gate(
            o, gn_ref[hd:hd + 1, :], g_ref[:, hd * RET_DV:(hd + 1) * RET_DV]).astype(y_ref.dtype)

    @pl.when(ci == pl.num_programs(1) - 1)
    def _():
        sfin_ref[0] = s_sc[...]


def _retention_prompt(q, k, v, g, gn, batch, seq):
    chunk = RET_CHUNK
    nc = seq // chunk
    row = lambda w: pl.BlockSpec((chunk, w), lambda b, c: (b * nc + c, 0))
    return pl.pallas_call(
        functools.partial(_ret_prompt_kernel, chunk=chunk),
        grid=(batch, nc),
        in_specs=[row(_RET_QK), row(_RET_QK), row(RET_WIDTH), row(RET_WIDTH), _full_spec(gn)],
        out_specs=(row(RET_WIDTH),
                   pl.BlockSpec((1, RET_HEADS, RET_DK, RET_DV), lambda b, c: (b, 0, 0, 0))),
        out_shape=(jax.ShapeDtypeStruct((batch * seq, RET_WIDTH), BF16),
                   jax.ShapeDtypeStruct((batch, RET_HEADS, RET_DK, RET_DV), F32)),
        scratch_shapes=[pltpu.VMEM((RET_HEADS, RET_DK, RET_DV), F32)],
        compiler_params=_cparams(("parallel", "arbitrary")),
        name="retention_prompt",
    )(q, k, v, g, gn)


def _odd_out_kernel(x_ref, y_ref, w_ref, fn_ref, o_ref):
    x2 = x_ref[...] + _dot(y_ref[...], w_ref[...])
    o_ref[...] = _rms(x2, fn_ref[...])


def _odd_output(x, y, w_bf, fn, tm):
    t = x.shape[0]
    row = lambda w: pl.BlockSpec((tm, w), lambda i: (i, 0))
    return pl.pallas_call(
        _odd_out_kernel,
        grid=(t // tm,),
        in_specs=[row(D_MODEL), row(RET_WIDTH), _full_spec(w_bf), _full_spec(fn)],
        out_specs=row(D_MODEL),
        out_shape=jax.ShapeDtypeStruct((t, D_MODEL), F32),
        compiler_params=_cparams(("parallel",)),
        name="odd_out",
    )(x, y, w_bf, fn)


def _ret_decode_kernel(q_ref, k_ref, v_ref, g_ref, gn_ref, s_ref, y_ref, snew_ref):
    eye = (lax.broadcasted_iota(jnp.int32, (RET_DK, RET_DK), 0)
           == lax.broadcasted_iota(jnp.int32, (RET_DK, RET_DK), 1))
    for hd in range(RET_HEADS):
        gam = float(np.exp(_RET_LOG_G[hd]))
        qh = q_ref[0, :, hd * RET_DK:(hd + 1) * RET_DK]
        kh = k_ref[0, :, hd * RET_DK:(hd + 1) * RET_DK]
        vh = v_ref[0, :, hd * RET_DV:(hd + 1) * RET_DV].astype(BF16).astype(F32)
        s_prev = s_ref[0, hd]
        inner = jnp.sum(qh.astype(F32) * kh.astype(F32), axis=1, keepdims=True).astype(BF16).astype(F32)
        qs = _dot(jnp.broadcast_to(qh, (8, RET_DK)), s_prev.astype(BF16))[0:1]
        o = inner * vh + qs * gam
        k_col = jnp.sum(jnp.where(eye, jnp.broadcast_to(kh.astype(F32), (RET_DK, RET_DK)), 0.0), axis=1, keepdims=True)
        snew_ref[0, hd] = gam * s_prev + k_col * vh
        y_ref[0, :, hd * RET_DV:(hd + 1) * RET_DV] = _group_norm_gate(
            o, gn_ref[hd:hd + 1, :], g_ref[0, :, hd * RET_DV:(hd + 1) * RET_DV]).astype(y_ref.dtype)


def _retention_decode(q, k, v, g, gn, state):
    db = q.shape[0]
    r3 = lambda a: a.reshape(db, 1, a.shape[-1])
    vec = lambda w: pl.BlockSpec((1, 1, w), lambda b: (b, 0, 0))
    st = pl.BlockSpec((1, RET_HEADS, RET_DK, RET_DV), lambda b: (b, 0, 0, 0))
    y, s_new = pl.pallas_call(
        _ret_decode_kernel,
        grid=(db,),
        in_specs=[vec(_RET_QK), vec(_RET_QK), vec(RET_WIDTH), vec(RET_WIDTH), _full_spec(gn), st],
        out_specs=(vec(RET_WIDTH), st),
        out_shape=(jax.ShapeDtypeStruct((db, 1, RET_WIDTH), BF16),
                   jax.ShapeDtypeStruct(state.shape, state.dtype)),
        compiler_params=_cparams(("parallel",)),
        name="retention_decode",
    )(r3(q), r3(k), r3(v), r3(g), gn, state)
    return y.reshape(db, RET_WIDTH), s_new


PAGES_PER_STEP = 32
SUB = 8


PAGE_SLOTS = 3


def _page_group_dmas(pt_ref, pool_ref, buf_ref, sem_ref, grp, lookup):
    slot = lax.rem(grp, PAGE_SLOTS)
    copies = []
    for j in range(PAGES_PER_STEP):
        page = pt_ref[grp * PAGES_PER_STEP + j] if lookup else 0
        copies.append(pltpu.make_async_copy(pool_ref.at[page], buf_ref.at[slot, j], sem_ref.at[slot]))
    return copies


def _page_pipeline_step(pt_ref, pool_ref, buf_ref, sem_ref, steps):
    b = pl.program_id(0)
    i = pl.program_id(1)
    grp = b * steps + i
    n_grp = pl.num_programs(0) * steps

    @pl.when(jnp.logical_and(b == 0, i == 0))
    def _():
        for g0 in range(min(2, PAGE_SLOTS - 1)):
            @pl.when(g0 < n_grp)
            def _(g0=g0):
                for c in _page_group_dmas(pt_ref, pool_ref, buf_ref, sem_ref, g0, True):
                    c.start()

    @pl.when(i < steps)
    def _():
        for c in _page_group_dmas(pt_ref, pool_ref, buf_ref, sem_ref, grp, False):
            c.wait()

        @pl.when(grp + 2 < n_grp)
        def _():
            for c in _page_group_dmas(pt_ref, pool_ref, buf_ref, sem_ref, grp + 2, True):
                c.start()

    return lax.rem(grp, PAGE_SLOTS)


def _mla_decode_kernel(pt_ref, q_ref, new_ref, *rest, steps):
    pool_ref, o_ref, m_sc, l_sc, acc_sc, r0_sc, r1_sc, s0_sc, s1_sc, page_sc, page_sem = rest
    bufs = ((r0_sc, s0_sc), (r1_sc, s1_sc))
    i = pl.program_id(1)
    slot = _page_pipeline_step(pt_ref, pool_ref, page_sc, page_sem, steps)
    q = q_ref[0]

    def load(par):
        r_sc, s_sc = bufs[par]
        rows_t = jnp.concatenate([page_sc[slot, j] for j in range(PAGES_PER_STEP)], axis=1).astype(BF16)
        r_sc[...] = rows_t
        s_sc[...] = _dot(q, rows_t) * MLA_SCALE

    def proc(par):
        r_sc, s_sc = bufs[par]
        s = s_sc[...]
        m_prev = m_sc[...]
        m_new = jnp.maximum(m_prev, _row_max(s))
        alpha = jnp.exp(m_prev - m_new)
        p = jnp.exp(s - m_new)
        l_sc[...] = alpha * l_sc[...] + _row_sum(p)
        acc_sc[...] = alpha * acc_sc[...] + _dot_nt(p.astype(BF16), r_sc[0:MLA_KV_RANK, :])
        m_sc[...] = m_new

    @pl.when(i == 0)
    def _():
        m_sc[...] = jnp.full(m_sc.shape, NEG_BIG, F32)
        l_sc[...] = jnp.zeros(l_sc.shape, F32)
        acc_sc[...] = jnp.zeros(acc_sc.shape, F32)
        load(0)

    even = lax.rem(i, 2) == 0
    steady = jnp.logical_and(i > 0, i < steps)
    for par in range(2):
        @pl.when(jnp.logical_and(steady, even == (par == 0)))
        def _(par=par):
            load(par)
            proc(1 - par)

    @pl.when(i == steps)
    def _():
        proc((steps - 1) % 2)
        new = new_ref[0].astype(BF16).astype(F32)
        s_n = jnp.sum(q.astype(F32) * new, axis=1, keepdims=True) * MLA_SCALE
        m_prev = m_sc[...]
        m_new = jnp.maximum(m_prev, s_n)
        alpha = jnp.exp(m_prev - m_new)
        p_n = jnp.exp(s_n - m_new)
        l = alpha * l_sc[...] + p_n
        acc = alpha * acc_sc[...] + p_n.astype(BF16).astype(F32) * new[:, :MLA_KV_RANK]
        o_ref[0] = (acc / jnp.maximum(l, 1e-30)).astype(o_ref.dtype)


def _mla_decode(q_dec, rows_new, pool, page_table):
    db, n_pages = page_table.shape
    assert n_pages % PAGES_PER_STEP == 0
    steps = n_pages // PAGES_PER_STEP
    w = pool.shape[1]
    n_tok = PAGES_PER_STEP * PAGE_SIZE

    grid_spec = pltpu.PrefetchScalarGridSpec(
        num_scalar_prefetch=1,
        grid=(db, steps + 1),
        in_specs=[pl.BlockSpec((1, MLA_HEADS, w), lambda b, i, pt: (b, 0, 0)),
                  pl.BlockSpec((1, 1, w), lambda b, i, pt: (b, 0, 0)),
                  pl.BlockSpec(memory_space=pl.ANY)],
        out_specs=pl.BlockSpec((1, MLA_HEADS, MLA_KV_RANK), lambda b, i, pt: (b, 0, 0)),
        scratch_shapes=[pltpu.VMEM((MLA_HEADS, 1), F32), pltpu.VMEM((MLA_HEADS, 1), F32),
                        pltpu.VMEM((MLA_HEADS, MLA_KV_RANK), F32),
                        pltpu.VMEM((w, n_tok), BF16), pltpu.VMEM((w, n_tok), BF16),
                        pltpu.VMEM((MLA_HEADS, n_tok), F32), pltpu.VMEM((MLA_HEADS, n_tok), F32),
                        pltpu.VMEM((PAGE_SLOTS, PAGES_PER_STEP, w, PAGE_SIZE), F32),
                        pltpu.SemaphoreType.DMA((PAGE_SLOTS,))],
    )
    return pl.pallas_call(
        functools.partial(_mla_decode_kernel, steps=steps),
        grid_spec=grid_spec,
        out_shape=jax.ShapeDtypeStruct((db, MLA_HEADS, MLA_KV_RANK), BF16),
        compiler_params=_cparams(("arbitrary", "arbitrary")),
        name="mla_decode",
    )(page_table.reshape(-1), q_dec, rows_new.reshape(db, 1, w), pool)


def _decode_blocks(p_len, n_new):
    n_slc = -(-(p_len + n_new) // SLC_BLOCK)
    return n_slc, LANES * ((n_slc + LANES - 1) // LANES)


CMP_PAIRS = CMP_STRIDE // 2
GRP_W = 2 * NSA_DH


def _prep_cmp_pair_weights(w_cmp, pe_cmp):
    wh = w_cmp.reshape(2, 2, CMP_PAIRS, 2, NSA_DH, NSA_DH)
    t = jnp.transpose(wh, (2, 3, 0, 4, 1, 5)).astype(BF16)
    eye_c = jnp.asarray(np.eye(2, dtype=np.float32).reshape(1, 1, 2, 1, 1, 2, 1), BF16)
    big = t[:, :, :, :, :, None, :] * eye_c
    w_pair = big.reshape(CMP_PAIRS, 2 * GRP_W, 2 * GRP_W)
    peh = pe_cmp.reshape(2, 2, CMP_PAIRS, 2, NSA_DH)
    pe_pair = jnp.transpose(peh, (1, 2, 3, 0, 4)).reshape(2, CMP_PAIRS, 1, 2 * GRP_W)
    return w_pair, pe_pair


def _cmp_bias_kernel(pe_ref, w_ref, b_ref):
    for half in range(2):
        acc = jnp.zeros((SUB, GRP_W), F32)
        for jp in range(CMP_PAIRS):
            w = w_ref[jp][:, half * GRP_W:(half + 1) * GRP_W]
            for t in _split3(jnp.broadcast_to(pe_ref[half, jp], (SUB, 2 * GRP_W))):
                acc = acc + _dot(t, w)
        b_ref[half] = acc


def _cmp_bias(w_pair, pe_pair):
    return pl.pallas_call(
        _cmp_bias_kernel,
        grid=(1,),
        in_specs=[_full_spec(pe_pair), _full_spec(w_pair)],
        out_specs=pl.BlockSpec((2, SUB, GRP_W), lambda i: (0, 0, 0)),
        out_shape=jax.ShapeDtypeStruct((2, SUB, GRP_W), F32),
        compiler_params=_cparams(("arbitrary",)),
        name="nsa_cmp_bias",
    )(pe_pair, w_pair)


def _cmp_decode_kernel(pt_ref, qn_ref, cnew_ref, bias_ref, w_ref, amat_ref, *rest, p_len, steps):
    pool_ref, oc_ref, sslc_ref, xa_sc, xb_sc, h_sc, page_sc, page_sem = rest
    x_bufs = (xa_sc, xb_sc)
    i = pl.program_id(1)
    slot = _page_pipeline_step(pt_ref, pool_ref, page_sc, page_sem, steps)
    n_tok = PAGES_PER_STEP * PAGE_SIZE
    n_step = n_tok // CMP_STRIDE
    nseg = h_sc.shape[1]

    def load(par):
        x_sc = x_bufs[par]
        for j in range(PAGES_PER_STEP):
            for g in range(NSA_KV_HEADS):
                x_sc[g, j * PAGE_SIZE:(j + 1) * PAGE_SIZE, :] = page_sc[slot, j, g * GRP_W:(g + 1) * GRP_W, :].T

    def compress(par, step):
        x_sc = x_bufs[par]
        r0 = pl.multiple_of(step * n_step, n_step)
        for g in range(NSA_KV_HEADS):
            acc = jnp.zeros((n_step, 2 * GRP_W), F32)
            for jp in range(CMP_PAIRS):
                rows = [x_sc[g, pl.ds(2 * jp + jj, n_step, stride=CMP_STRIDE), :] for jj in range(2)]
                acc = acc + _dot(jnp.concatenate(rows, axis=1).astype(BF16), w_ref[jp])
            h_sc[g, pl.ds(r0, n_step), :] = acc

    @pl.when(i == 0)
    def _():
        load(0)

    even = lax.rem(i, 2) == 0
    steady = jnp.logical_and(i > 0, i < steps)
    for par in range(2):
        @pl.when(jnp.logical_and(steady, even == (par == 0)))
        def _(par=par):
            load(par)
            compress(1 - par, i - 1)

    @pl.when(i == steps)
    def _():
        compress((steps - 1) % 2, steps - 1)

    @pl.when(i == steps)
    def _():
        qpos = p_len
        cpos = lax.broadcasted_iota(jnp.int32, (1, nseg), 1) * CMP_STRIDE + (CMP_BLOCK - 1)
        dist = (qpos - cpos).astype(F32)
        row = lax.broadcasted_iota(jnp.int32, (SUB, 1), 0)
        seg = lax.broadcasted_iota(jnp.int32, (nseg, 1), 0)
        amat = amat_ref[...]
        b1 = bias_ref[0, 0:1, :]
        b2 = bias_ref[1, 0:1, :]
        for g in range(NSA_KV_HEADS):
            h = h_sc[g]
            new_seg = jnp.concatenate([cnew_ref[0, :, g * GRP_W:(g + 1) * GRP_W], jnp.zeros((1, GRP_W), F32)], axis=1)
            h2n = _dot(jnp.broadcast_to(new_seg, (SUB, 2 * GRP_W)).astype(BF16), w_ref[0])[0:1, GRP_W:] + b2
            h2 = jnp.where(seg == nseg - 1, h2n, pltpu.roll(h[:, GRP_W:], nseg - 1, 0) + b2)
            kvc = (h[:, :GRP_W] + b1 + h2).astype(BF16)
            slope = jnp.zeros((SUB, 1), F32)
            for rr in range(NSA_GROUP):
                slope = jnp.where(row == rr, 2.0 ** (-(g * NSA_GROUP + rr + 1.0)), slope)
            qg = qn_ref[0, g]
            p_c = _softmax_rows(_dot_nt(qg, kvc) - slope * dist, dist >= 0)
            p_c = jnp.where(row < NSA_GROUP, p_c, 0.0)
            oc_ref[0, g] = pltpu.roll(_dot(p_c.astype(BF16), kvc), LANES - NSA_DH, 1)
            s_rows = sum(_dot(t, amat) for t in _split3(p_c))
            sslc_ref[0, g] = jnp.broadcast_to(jnp.sum(s_rows, axis=0, keepdims=True), s_rows.shape)


def _cmp_decode(qn_dec, cmp_new, pool_t, page_table, w_pair, bias):
    db, n_pages = page_table.shape
    steps = n_pages // PAGES_PER_STEP
    p_len = n_pages * PAGE_SIZE
    nseg = p_len // CMP_STRIDE
    n_slc, nl = _decode_blocks(p_len, 1)
    amat = jnp.asarray(_slc_overlap_matrix(nseg, nl), BF16)

    grid_spec = pltpu.PrefetchScalarGridSpec(
        num_scalar_prefetch=1,
        grid=(db, steps + 1),
        in_specs=[pl.BlockSpec((1, NSA_KV_HEADS, SUB, LANES), lambda b, i, pt: (b, 0, 0, 0)),
                  pl.BlockSpec((1, 1, NSA_KV_W), lambda b, i, pt: (b, 0, 0)),
                  pl.BlockSpec(bias.shape, lambda b, i, pt: (0, 0, 0)),
                  pl.BlockSpec(w_pair.shape, lambda b, i, pt: (0, 0, 0)),
                  pl.BlockSpec(amat.shape, lambda b, i, pt: (0, 0)),
                  pl.BlockSpec(memory_space=pl.ANY)],
        out_specs=(pl.BlockSpec((1, NSA_KV_HEADS, SUB, LANES), lambda b, i, pt: (b, 0, 0, 0)),
                   pl.BlockSpec((1, NSA_KV_HEADS, SUB, nl), lambda b, i, pt: (b, 0, 0, 0))),
        scratch_shapes=[pltpu.VMEM((NSA_KV_HEADS, PAGES_PER_STEP * PAGE_SIZE, GRP_W), F32),
                        pltpu.VMEM((NSA_KV_HEADS, PAGES_PER_STEP * PAGE_SIZE, GRP_W), F32),
                        pltpu.VMEM((NSA_KV_HEADS, nseg, 2 * GRP_W), F32),
                        pltpu.VMEM((PAGE_SLOTS, PAGES_PER_STEP, NSA_KV_W, PAGE_SIZE), F32),
                        pltpu.SemaphoreType.DMA((PAGE_SLOTS,))],
    )
    return pl.pallas_call(
        functools.partial(_cmp_decode_kernel, p_len=p_len, steps=steps),
        grid_spec=grid_spec,
        out_shape=(jax.ShapeDtypeStruct((db, NSA_KV_HEADS, SUB, LANES), F32),
                   jax.ShapeDtypeStruct((db, NSA_KV_HEADS, SUB, nl), F32)),
        compiler_params=_cparams(("arbitrary", "arbitrary")),
        name="nsa_cmp_decode",
    )(page_table.reshape(-1), qn_dec, cmp_new.reshape(db, 1, NSA_KV_W), bias, w_pair, amat, pool_t)


def _topk_decode_kernel(s_ref, idx_ref, *, qpos):
    s_slc = s_ref[...]
    blk = lax.broadcasted_iota(jnp.int32, s_slc.shape, 1)
    cur = qpos // SLC_BLOCK
    valid = blk * SLC_BLOCK <= qpos
    forced = jnp.logical_or(blk == 0, jnp.logical_or(blk == cur, blk == cur - 1))
    score = jnp.where(forced, FORCE_SCORE, jnp.where(valid, s_slc, NEG_BIG))
    lane = lax.broadcasted_iota(jnp.int32, idx_ref.shape, 1)
    out = jnp.full(idx_ref.shape, -1, jnp.int32)
    for t in range(N_SELECT):
        m = jnp.max(score, axis=1, keepdims=True)
        idx = jnp.min(jnp.where(score == m, blk, score.shape[1]), axis=1, keepdims=True)
        out = jnp.where(lane == t, jnp.where(m > 0.5 * NEG_BIG, idx, -1), out)
        score = jnp.where(blk == idx, 2.0 * NEG_BIG, score)
    idx_ref[...] = out


def _topk_decode(s_slc, qpos):
    rows = s_slc.shape[0]
    return pl.pallas_call(
        functools.partial(_topk_decode_kernel, qpos=qpos),
        grid=(1,),
        in_specs=[_full_spec(s_slc)],
        out_specs=pl.BlockSpec((rows, LANES), lambda i: (0, 0)),
        out_shape=jax.ShapeDtypeStruct((rows, LANES), jnp.int32),
        compiler_params=_cparams(("arbitrary",)),
        name="nsa_topk_decode",
    )(s_slc)


def _sel_decode_kernel(pt_ref, sel_ref, qn_ref, gl_ref, oc_ref, snew_ref, wnew_ref, wnewt_ref, win_ref, *rest, p_len):
    del pt_ref
    nb = NSA_KV_HEADS * N_SELECT
    blk_refs = rest[:nb]
    o_ref, nwin_ref = rest[nb:]
    b = pl.program_id(0)
    qpos = p_len
    n_past_blocks = p_len // SLC_BLOCK
    per_page = PAGE_SIZE // SLC_BLOCK
    row = lax.broadcasted_iota(jnp.int32, (SUB, 1), 0)
    lane = lax.broadcasted_iota(jnp.int32, (SUB, LANES), 1)
    gates = jnp.broadcast_to(jax.nn.sigmoid(gl_ref[0]), (SUB, LANES))
    nw = win_ref.shape[2]
    win = win_ref[0]
    wnew = wnew_ref[0]
    snew = snew_ref[0]

    def merge_new(s, mask, kv_t, q8, new_row, slope, use_new):
        new_f = new_row.astype(BF16).astype(F32)
        s_n = jnp.sum(q8.astype(F32) * new_f, axis=1, keepdims=True)
        s = jnp.where(mask, s, NEG_BIG)
        m = jnp.maximum(_row_max(s), jnp.where(use_new, s_n, NEG_BIG))
        e = jnp.where(mask, jnp.exp(s - m), 0.0)
        e_n = jnp.where(use_new, jnp.exp(s_n - m), 0.0)
        denom = jnp.maximum(_row_sum(e) + e_n, 1e-30)
        p = e / denom
        p_n = e_n / denom
        o = _dot_nt(p.astype(BF16), kv_t) + p_n.astype(BF16).astype(F32) * new_f
        return pltpu.roll(o, LANES - NSA_DH, 1)

    for g in range(NSA_KV_HEADS):
        slope = jnp.zeros((SUB, 1), F32)
        for rr in range(NSA_GROUP):
            slope = jnp.where(row == rr, 2.0 ** (-(g * NSA_GROUP + rr + 1.0)), slope)
        q8 = qn_ref[0, g]
        kv_t = jnp.concatenate([r[...] for r in blk_refs[g * N_SELECT:(g + 1) * N_SELECT]], axis=1).astype(BF16)
        nk = N_SELECT * PAGE_SIZE
        key = lax.broadcasted_iota(jnp.int32, (1, nk), 1)
        slot = lax.shift_right_logical(key, int(np.log2(PAGE_SIZE)))
        blk_in_page = jnp.bitwise_and(lax.shift_right_logical(key, int(np.log2(SLC_BLOCK))), per_page - 1)
        off = jnp.bitwise_and(key, SLC_BLOCK - 1)
        kblk = jnp.zeros((1, nk), jnp.int32)
        use_new = False
        for t in range(N_SELECT):
            st = sel_ref[(b * NSA_KV_HEADS + g) * N_SELECT + t]
            kblk = jnp.where(slot == t, st, kblk)
            use_new = jnp.logical_or(use_new, st == n_past_blocks)
        kpos = kblk * SLC_BLOCK + off
        ok = jnp.logical_and(jnp.logical_and(kblk >= 0, kblk < n_past_blocks),
                             jnp.logical_and(blk_in_page == jnp.bitwise_and(kblk, per_page - 1), kpos <= qpos))
        dist = (qpos - kpos).astype(F32)
        o_s = merge_new(_dot(q8, kv_t) - slope * dist, ok, kv_t, q8, snew[:, g * LANES:(g + 1) * LANES], slope, use_new)
        kvw_t = win[g * LANES:(g + 1) * LANES, :].astype(BF16)
        wpos = p_len - nw + lax.broadcasted_iota(jnp.int32, (1, nw), 1)
        dist_w = (qpos - wpos).astype(F32)
        mask_w = jnp.logical_and(jnp.logical_and(dist_w >= 0, dist_w <= WINDOW), wpos >= 0)
        o_w = merge_new(_dot(q8, kvw_t) - slope * dist_w, mask_w, kvw_t, q8, wnew[:, g * LANES:(g + 1) * LANES], slope, True)
        hd = g * NSA_GROUP + row
        gate = lambda c: jnp.sum(jnp.where(lane == 3 * hd + c, gates, 0.0), axis=1, keepdims=True)
        mix = gate(0) * oc_ref[0, g] + gate(1) * o_s + gate(2) * o_w
        o_ref[0, g * NSA_GROUP:(g + 1) * NSA_GROUP, :] = mix[0:NSA_GROUP]
    req = lax.broadcasted_iota(jnp.int32, wnewt_ref.shape, 1)
    col = jnp.sum(jnp.where(req == b, wnewt_ref[...], 0.0), axis=1, keepdims=True)
    t_idx = lax.broadcasted_iota(jnp.int32, (1, nw), 1)
    nwin_ref[0] = jnp.where(t_idx == nw - 1, col, pltpu.roll(win, nw - 1, 1))


def _sel_decode(qn_dec, gl, o_c, sel_idx, slc_new, win_new, slc_pool_t, win_buf_t, page_table):
    db, n_pages = page_table.shape
    p_len = n_pages * PAGE_SIZE
    n_past_blocks = p_len // SLC_BLOCK
    per_page = PAGE_SIZE // SLC_BLOCK
    nw = win_buf_t.shape[2]
    win_new_t = jnp.transpose(win_new)

    def blk_spec(g, t):
        def imap(b, pt, sel):
            j = jnp.clip(sel[(b * NSA_KV_HEADS + g) * N_SELECT + t], 0, n_past_blocks - 1)
            return (pt[b * n_pages + j // per_page], g, 0)
        return pl.BlockSpec((None, GRP_W, PAGE_SIZE), imap)

    vec = lambda w: pl.BlockSpec((1, 1, w), lambda b, pt, sel: (b, 0, 0))
    grp = pl.BlockSpec((1, NSA_KV_HEADS, SUB, LANES), lambda b, pt, sel: (b, 0, 0, 0))
    grid_spec = pltpu.PrefetchScalarGridSpec(
        num_scalar_prefetch=2,
        grid=(db,),
        in_specs=[grp, vec(LANES), grp, vec(NSA_KV_W), vec(NSA_KV_W),
                  pl.BlockSpec(win_new_t.shape, lambda b, pt, sel: (0, 0)),
                  pl.BlockSpec((1, NSA_KV_W, nw), lambda b, pt, sel: (b, 0, 0))]
                 + [blk_spec(g, t) for g in range(NSA_KV_HEADS) for t in range(N_SELECT)],
        out_specs=(pl.BlockSpec((1, NSA_HEADS, LANES), lambda b, pt, sel: (b, 0, 0)),
                   pl.BlockSpec((1, NSA_KV_W, nw), lambda b, pt, sel: (b, 0, 0))),
    )
    r3 = lambda a: a.reshape(db, 1, a.shape[-1])
    return pl.pallas_call(
        functools.partial(_sel_decode_kernel, p_len=p_len),
        grid_spec=grid_spec,
        out_shape=(jax.ShapeDtypeStruct((db, NSA_HEADS, LANES), F32),
                   jax.ShapeDtypeStruct(win_buf_t.shape, win_buf_t.dtype)),
        compiler_params=_cparams(("parallel",)),
        name="nsa_sel_decode",
    )(page_table.reshape(-1), sel_idx, qn_dec, r3(gl), o_c, r3(slc_new), r3(win_new), win_new_t, win_buf_t,
      *([slc_pool_t] * (NSA_KV_HEADS * N_SELECT)))


TM_PROMPT = 256
TQ_MLA, TK_MLA = 256, 512
TQ_NSA, TK_NSA = 128, 512


def _mla_rows(rows):
    return jnp.concatenate([rows[:, :MLA_KV_RANK], rows[:, MLA_KV_RANK:MLA_KV_RANK + HALF_ROPE],
                            rows[:, MLA_KV_RANK + LANES:MLA_KV_RANK + LANES + HALF_ROPE]], axis=1)


def kernel(x_prompt, x_sample, cache_mla, cache_nsa_cmp, cache_nsa_slc, state_nsa_win, state_ret, page_table,
           norm_even, w_in_even, mla_gq, mla_gkv, mla_wuq, mla_wuk, mla_wuv, nsa_cmp_w, nsa_cmp_pe, w_out_even,
           norm_odd, w_in_odd, ret_gn, w_out_odd, final_norm):
    b, s, d = x_prompt.shape
    db, n_new, _ = x_sample.shape
    assert n_new == 1 and norm_even.shape[0] == 1 and norm_odd.shape[0] == 1
    n_pages = page_table.shape[1]
    p_len = n_pages * PAGE_SIZE
    kv_row = (NSA_KV_HEADS, 2, NSA_DH)
    tm = min(TM_PROMPT, b * s)

    w_ext, wuq_ext, wuk_ext, wuv_ext, wo_ext = _prep_even_weights(
        w_in_even[0], mla_wuq[0], mla_wuk[0], mla_wuv[0], w_out_even[0])
    w_big, pe_big = _prep_cmp_weights(nsa_cmp_w[0], nsa_cmp_pe[0])
    w_odd = w_in_odd[0].astype(BF16)
    wo_odd = w_out_odd[0].astype(BF16)
    gn_e, gq, gkv = norm_even[0][None], mla_gq[0][None], mla_gkv[0][None]
    gn_o, fn = norm_odd[0][None], final_norm[None]

    xp = x_prompt.reshape(b * s, d)
    pos_p = jnp.arange(s)
    cos_m, sin_m = _rope_tables_mla(pos_p)
    (qext, rows, kext, gmla, qn, cmp, slc, slcp, win, winp, gl, gnsa) = _even_project(
        xp, gn_e, w_ext, gq, gkv, wuq_ext, wuk_ext, cos_m, sin_m, tm)
    lat = _mla_prompt_attention(qext, kext, b, s, min(TQ_MLA, s), min(TK_MLA, s))
    kvc = _cmp_prompt(cmp, w_big, pe_big, b, s)
    onsa = _nsa_prompt(qn, gl, kvc, slcp, winp, b, s, TQ_NSA, min(TK_NSA, s))
    x1 = _even_output(xp, lat, gmla, onsa, gnsa, wuv_ext, wo_ext, tm)
    cos_r, sin_r = _rope_tables_ret(pos_p)
    q, k, v, g = _odd_project(x1, gn_o, w_odd, cos_r, sin_r, tm)
    y, ret_p = _retention_prompt(q, k, v, g, ret_gn[0], b, s)
    y_prompt = _odd_output(x1, y, wo_odd, fn, tm).reshape(b, s, d)
    nwin = min(WINDOW, s)
    mla_p = _mla_rows(rows).reshape(1, b, s, MLA_KV_RANK + MLA_ROPE)
    cmp_p = cmp.reshape((1, b, s) + kv_row)
    slc_p = slc.reshape((1, b, s) + kv_row)
    win_p = win.reshape((b, s) + kv_row)[:, s - nwin:][None]

    xs = x_sample.reshape(db, d)
    pos_s = p_len + jnp.arange(n_new)
    cos_s, sin_s = [jnp.broadcast_to(t, (db, LANES)) for t in _rope_tables_mla(pos_s)]
    (qext_s, rows_s, _, gmla_s, qn_s, cmp_s, slc_s, _, win_s, _, gl_s, gnsa_s) = _even_project(
        xs, gn_e, w_ext, gq, gkv, wuq_ext, wuk_ext, cos_s, sin_s, db)
    rows_new = _mla_rows(rows_s)
    rope = lambda a: a.astype(F32).reshape(MLA_HEADS, db, MLA_HEADS, HALF_ROPE).sum(2).astype(BF16)
    q_dec = jnp.concatenate([qext_s[:, :, :MLA_KV_RANK], rope(qext_s[:, :, MLA_KV_RANK:MLA_KV_RANK + LANES]),
                             rope(qext_s[:, :, MLA_KV_RANK + LANES:])], axis=-1)
    pool = cache_nsa_cmp.shape[1]
    feat_major = lambda a, n: jnp.transpose(a[0], (0, 2, 3, 4, 1)).reshape(n, NSA_KV_W, a.shape[2])
    mla_pool_t = jnp.transpose(cache_mla[0], (0, 2, 1))
    lat_s = _mla_decode(jnp.transpose(q_dec, (1, 0, 2)), rows_new, mla_pool_t, page_table)
    qn_dec = jnp.pad(qn_s.reshape(db, NSA_KV_HEADS, NSA_GROUP, LANES), ((0, 0), (0, 0), (0, SUB - NSA_GROUP), (0, 0)))
    w_pair, pe_pair = _prep_cmp_pair_weights(nsa_cmp_w[0], nsa_cmp_pe[0])
    o_c, s_slc = _cmp_decode(qn_dec, cmp_s, feat_major(cache_nsa_cmp, pool), page_table, w_pair,
                             _cmp_bias(w_pair, pe_pair))
    sel = _topk_decode(s_slc[:, :, 0, :].reshape(db * NSA_KV_HEADS, -1), p_len)
    sel_idx = sel[:, :N_SELECT].reshape(-1)
    onsa_s, win_new_t = _sel_decode(qn_dec, gl_s, o_c, sel_idx, slc_s, win_s, feat_major(cache_nsa_slc, pool),
                                    feat_major(state_nsa_win, db), page_table)
    nw = win_new_t.shape[2]
    win_so = jnp.transpose(win_new_t.reshape((db,) + kv_row + (nw,)), (0, 4, 1, 2, 3))[None]
    x1s = _even_output(xs, jnp.transpose(lat_s, (1, 0, 2)), gmla_s, onsa_s.reshape(db, NSA_HEADS * LANES), gnsa_s,
                       wuv_ext, wo_ext, db)
    cos_rs, sin_rs = [jnp.broadcast_to(t, (db, RET_DK)) for t in _rope_tables_ret(pos_s)]
    q, k, v, g = _odd_project(x1s, gn_o, w_odd, cos_rs, sin_rs, db)
    ys, ret_s = _retention_decode(q, k, v, g, ret_gn[0], state_ret.reshape(state_ret.shape[1:]))
    y_sample = _odd_output(x1s, ys, wo_odd, fn, db).reshape(db, n_new, d)
    mla_s = rows_new.reshape(1, db, n_new, MLA_KV_RANK + MLA_ROPE)
    cmp_so = cmp_s.reshape((1, db, n_new) + kv_row)
    slc_so = slc_s.reshape((1, db, n_new) + kv_row)
    return (y_prompt, y_sample, mla_p, cmp_p, slc_p, win_p, ret_p[None],
            mla_s, cmp_so, slc_so, win_so, ret_s[None])
```

```python
import functools

import numpy as np
import jax
import jax.numpy as jnp
from jax import lax
from jax.experimental import pallas as pl
from jax.experimental.pallas import tpu as pltpu

F32 = jnp.float32
BF16 = jnp.bfloat16

D_MODEL = 1024
PAGE_SIZE = 128
MLA_HEADS = 8
MLA_NOPE = 64
MLA_ROPE = 32
MLA_V = 64
MLA_Q_RANK = 768
MLA_KV_RANK = 256
MLA_WIDTH = MLA_HEADS * MLA_V
MLA_SCALE = (MLA_NOPE + MLA_ROPE) ** -0.5
NSA_HEADS = 8
NSA_KV_HEADS = 2
NSA_GROUP = NSA_HEADS // NSA_KV_HEADS
NSA_DH = 64
NSA_WIDTH = NSA_HEADS * NSA_DH
NSA_KV_W = NSA_KV_HEADS * 2 * NSA_DH
CMP_BLOCK = 32
CMP_STRIDE = 16
SLC_BLOCK = 64
N_SELECT = 16
WINDOW = 512
FORCE_SCORE = 1e4
RET_HEADS = 8
RET_DK = 128
RET_DV = 256
RET_WIDTH = RET_HEADS * RET_DV
RET_CHUNK = 128
ROPE_BASE = 10000.0
EPS = 1e-6
EVEN_SPLITS = (MLA_Q_RANK, MLA_KV_RANK, MLA_ROPE, MLA_WIDTH, NSA_WIDTH, NSA_KV_W, NSA_KV_W, NSA_KV_W,
               3 * NSA_HEADS, NSA_WIDTH)

LANES = 128
VMEM_LIMIT_BYTES = 56 * 1024 * 1024
NEG_BIG = -1e30
MASK_BIG = 16384.0

HALF_ROPE = MLA_ROPE // 2


def _cparams(sem):
    return pltpu.CompilerParams(dimension_semantics=sem, vmem_limit_bytes=VMEM_LIMIT_BYTES)


def _full_spec(a):
    nd = a.ndim
    return pl.BlockSpec(a.shape, lambda *_: (0,) * nd)


def _rms(x, g):
    y = x * lax.rsqrt(jnp.mean(x * x, axis=-1, keepdims=True) + EPS)
    return y * g


def _dot(a, b):
    return jnp.dot(a, b, preferred_element_type=F32)


def _dot_nt(a, b):
    return lax.dot_general(a, b, (((1,), (1,)), ((), ())), preferred_element_type=F32)


def _fold_lanes(x, op):
    n = x.shape[1]
    if n % LANES:
        return x
    parts = [x[:, i:i + LANES] for i in range(0, n, LANES)]
    while len(parts) > 1:
        parts = [op(parts[i], parts[i + 1]) if i + 1 < len(parts) else parts[i] for i in range(0, len(parts), 2)]
    return parts[0]


def _row_max(x):
    return jnp.max(_fold_lanes(x, jnp.maximum), axis=1, keepdims=True)


def _row_sum(x):
    return jnp.sum(_fold_lanes(x, jnp.add), axis=1, keepdims=True)


def _rep(col):
    return jnp.broadcast_to(col, (col.shape[0], LANES))


def _tile_lanes(x, n):
    if n % LANES:
        return jnp.broadcast_to(x[:, :1], (x.shape[0], n))
    return x if n == LANES else jnp.tile(x, (1, n // LANES))


_EVEN_GROUPS = (
    ("cq", MLA_Q_RANK), ("ckv", MLA_KV_RANK), ("kr1", LANES), ("kr2", LANES),
    ("gmla", MLA_HEADS * LANES), ("qn", NSA_HEADS * LANES), ("cmp", NSA_KV_W),
    ("slc", NSA_KV_W), ("slcp", 4 * LANES), ("win", NSA_KV_W), ("winp", 4 * LANES),
    ("gl", LANES), ("gnsa", NSA_HEADS * LANES),
)
_EVEN_OFF = {}
_o = 0
for _n, _w in _EVEN_GROUPS:
    _EVEN_OFF[_n] = (_o, _o + _w)
    _o += _w
EVEN_EXT = _o


def _pad_heads(w, nh, dh):
    k = w.shape[0]
    w = w.reshape(k, nh, dh)
    return jnp.pad(w, ((0, 0), (0, 0), (0, LANES - dh))).reshape(k, nh * LANES)


def _kv_pad(w):
    k = w.shape[0]
    w4 = w.reshape(k, NSA_KV_HEADS, 2, NSA_DH)
    w4 = jnp.transpose(w4, (0, 2, 1, 3))
    return jnp.pad(w4, ((0, 0), (0, 0), (0, 0), (0, LANES - NSA_DH))).reshape(k, 4 * LANES)


def _prep_even_weights(w_in, wuq, wuk, wuv, w_out):
    offs = np.cumsum((0,) + EVEN_SPLITS)
    cq, ckv, kr, g_mla, q_nsa, kv_cmp, kv_slc, kv_win, gl, g_nsa = [
        w_in[:, offs[i]:offs[i + 1]] for i in range(len(EVEN_SPLITS))]
    parts = {
        "cq": cq, "ckv": ckv,
        "kr1": jnp.tile(kr[:, :HALF_ROPE], (1, MLA_HEADS)),
        "kr2": jnp.tile(kr[:, HALF_ROPE:], (1, MLA_HEADS)),
        "gmla": _pad_heads(g_mla, MLA_HEADS, MLA_V),
        "qn": _pad_heads(q_nsa, NSA_HEADS, NSA_DH),
        "cmp": kv_cmp, "slc": kv_slc, "slcp": _kv_pad(kv_slc),
        "win": kv_win, "winp": _kv_pad(kv_win),
        "gl": jnp.pad(gl, ((0, 0), (0, LANES - gl.shape[1]))),
        "gnsa": _pad_heads(g_nsa, NSA_HEADS, NSA_DH),
    }
    w_ext = jnp.concatenate([parts[n] for n, _ in _EVEN_GROUPS], axis=1).astype(BF16)
    c = wuq.shape[0]
    nope = jnp.pad(wuq[:, :, :MLA_NOPE], ((0, 0), (0, 0), (0, LANES - MLA_NOPE))).reshape(c, MLA_HEADS * LANES)
    r1 = wuq[:, :, MLA_NOPE:MLA_NOPE + HALF_ROPE].reshape(c, MLA_HEADS * HALF_ROPE)
    r2 = wuq[:, :, MLA_NOPE + HALF_ROPE:].reshape(c, MLA_HEADS * HALF_ROPE)
    wuq_ext = jnp.concatenate([nope, r1, r2], axis=1).astype(BF16)
    wuk_ext = jnp.pad(jnp.transpose(wuk, (1, 2, 0)), ((0, 0), (0, LANES - MLA_NOPE), (0, 0))).astype(BF16)
    wuv_ext = jnp.pad(jnp.transpose(wuv, (1, 0, 2)), ((0, 0), (0, 0), (0, LANES - MLA_V))).astype(BF16)
    d = w_out.shape[1]
    wo = w_out.reshape(MLA_HEADS + NSA_HEADS, MLA_V, d)
    wo_ext = jnp.pad(wo, ((0, 0), (0, LANES - MLA_V), (0, 0))).reshape((MLA_HEADS + NSA_HEADS) * LANES, d).astype(BF16)
    return w_ext, wuq_ext, wuk_ext, wuv_ext, wo_ext


def _rope_tables_mla(pos):
    inv = ROPE_BASE ** (-jnp.arange(HALF_ROPE, dtype=F32) / HALF_ROPE)
    ang = pos.astype(F32)[:, None] * inv
    return jnp.tile(jnp.cos(ang), (1, LANES // HALF_ROPE)), jnp.tile(jnp.sin(ang), (1, LANES // HALF_ROPE))


def _even_proj_kernel(x_ref, gn_ref, w_ref, wt_ref, gq_ref, gkv_ref, wuq_ref, wuk_ref, cos_ref, sin_ref,
                      qext_ref, rows_ref, kext_ref, gmla_ref, qn_ref, cmp_ref, slc_ref, slcp_ref,
                      win_ref, winp_ref, gl_ref, gnsa_ref, cmpt_ref, slct_ref):
    h = _rms(x_ref[...], gn_ref[...]).astype(BF16)
    cmpt_ref[...] = _dot_nt(wt_ref[0:NSA_KV_W, :], h)
    slct_ref[...] = _dot_nt(wt_ref[NSA_KV_W:, :], h)

    def proj(name):
        a, b = _EVEN_OFF[name]
        return _dot(h, w_ref[:, a:b])

    cos = cos_ref[...]
    sin = sin_ref[...]
    cqn = _rms(proj("cq"), gq_ref[...]).astype(BF16)
    nh = MLA_HEADS * LANES
    r1 = _dot(cqn, wuq_ref[:, nh:nh + LANES])
    r2 = _dot(cqn, wuq_ref[:, nh + LANES:nh + 2 * LANES])
    o1 = r1 * cos - r2 * sin
    o2 = r1 * sin + r2 * cos
    head_of_lane = lax.broadcasted_iota(jnp.int32, (1, LANES), 1) // HALF_ROPE
    for hd in range(MLA_HEADS):
        nope = _dot(cqn, wuq_ref[:, hd * LANES:(hd + 1) * LANES]).astype(BF16)
        qext_ref[hd, :, 0:MLA_KV_RANK] = _dot(nope, wuk_ref[hd]).astype(BF16)
        sel = head_of_lane == hd
        qext_ref[hd, :, MLA_KV_RANK:MLA_KV_RANK + LANES] = jnp.where(sel, o1, 0.0).astype(BF16)
        qext_ref[hd, :, MLA_KV_RANK + LANES:] = jnp.where(sel, o2, 0.0).astype(BF16)
    latn = _rms(proj("ckv"), gkv_ref[...])
    kr1 = proj("kr1")
    kr2 = proj("kr2")
    k1 = kr1 * cos - kr2 * sin
    k2 = kr1 * sin + kr2 * cos
    rows_ref[:, 0:MLA_KV_RANK] = latn
    rows_ref[:, MLA_KV_RANK:MLA_KV_RANK + LANES] = k1
    rows_ref[:, MLA_KV_RANK + LANES:] = k2
    kext_ref[:, 0:MLA_KV_RANK] = latn.astype(BF16)
    kext_ref[:, MLA_KV_RANK:MLA_KV_RANK + LANES] = k1.astype(BF16)
    kext_ref[:, MLA_KV_RANK + LANES:] = k2.astype(BF16)
    gmla_ref[...] = proj("gmla")
    qn_ref[...] = (proj("qn") * (NSA_DH ** -0.5)).astype(BF16)
    cmp_ref[...] = proj("cmp")
    slc_ref[...] = proj("slc")
    slcp_ref[...] = proj("slcp").astype(BF16)
    win_ref[...] = proj("win")
    winp_ref[...] = proj("winp").astype(BF16)
    gl_ref[...] = proj("gl")
    gnsa_ref[...] = proj("gnsa")


def _even_project(x, gn, w_ext, gq, gkv, wuq_ext, wuk_ext, cos, sin, tm, seq):
    t = x.shape[0]
    nt = t // tm
    nst = seq // tm
    a, b = _EVEN_OFF["cmp"][0], _EVEN_OFF["slc"][1]
    w_t = jnp.transpose(w_ext[:, a:b])
    ntab = cos.shape[0] // tm
    row = lambda w: pl.BlockSpec((tm, w), lambda i: (i, 0))
    tab = pl.BlockSpec((tm, LANES), lambda i: (i % ntab, 0))
    kext_w = MLA_KV_RANK + 2 * LANES
    out_shapes = (
        jax.ShapeDtypeStruct((MLA_HEADS, t, kext_w), BF16),
        jax.ShapeDtypeStruct((t, kext_w), F32),
        jax.ShapeDtypeStruct((t, kext_w), BF16),
        jax.ShapeDtypeStruct((t, MLA_HEADS * LANES), F32),
        jax.ShapeDtypeStruct((t, NSA_HEADS * LANES), BF16),
        jax.ShapeDtypeStruct((t, NSA_KV_W), F32),
        jax.ShapeDtypeStruct((t, NSA_KV_W), F32),
        jax.ShapeDtypeStruct((t, 4 * LANES), BF16),
        jax.ShapeDtypeStruct((t, NSA_KV_W), F32),
        jax.ShapeDtypeStruct((t, 4 * LANES), BF16),
        jax.ShapeDtypeStruct((t, LANES), F32),
        jax.ShapeDtypeStruct((t, NSA_HEADS * LANES), F32),
        jax.ShapeDtypeStruct((t // seq, NSA_KV_W, seq), F32),
        jax.ShapeDtypeStruct((t // seq, NSA_KV_W, seq), F32),
    )
    fm = pl.BlockSpec((None, NSA_KV_W, tm), lambda i: (i // nst, 0, i % nst))
    out_specs = (
        pl.BlockSpec((MLA_HEADS, tm, kext_w), lambda i: (0, i, 0)),
        row(kext_w), row(kext_w), row(MLA_HEADS * LANES), row(NSA_HEADS * LANES), row(NSA_KV_W),
        row(NSA_KV_W), row(4 * LANES), row(NSA_KV_W), row(4 * LANES), row(LANES), row(NSA_HEADS * LANES), fm, fm,
    )
    return pl.pallas_call(
        _even_proj_kernel,
        grid=(nt,),
        in_specs=[row(D_MODEL), _full_spec(gn), _full_spec(w_ext), _full_spec(w_t), _full_spec(gq), _full_spec(gkv),
                  _full_spec(wuq_ext), _full_spec(wuk_ext), tab, tab],
        out_specs=out_specs,
        out_shape=out_shapes,
        compiler_params=_cparams(("parallel",)),
        name="even_proj",
    )(x, gn, w_ext, w_t, gq, gkv, wuq_ext, wuk_ext, cos, sin)


LOG2E = float(np.log2(np.e))


def _mla_attn_kernel(q_ref, k_ref, kn_ref, o_ref, m_sc, l_sc, acc_sc, sa_sc, sb_sc, *, tq, tk):
    qi = pl.program_id(1)
    ki = pl.program_id(2)
    n_needed = ((qi + 1) * tq + tk - 1) // tk
    m_rows = MLA_HEADS * tq
    c_exp = MLA_SCALE * LOG2E

    def scores(kref):
        return _dot_nt(q_ref[...].reshape(m_rows, q_ref.shape[-1]), kref[...])

    @pl.when(ki == 0)
    def _():
        m_sc[...] = jnp.full(m_sc.shape, NEG_BIG, F32)
        l_sc[...] = jnp.zeros(l_sc.shape, F32)
        acc_sc[...] = jnp.zeros(acc_sc.shape, F32)
        sa_sc[...] = scores(k_ref)

    def step(masked, cur_sc, nxt_sc):
        nxt_sc[...] = scores(kn_ref)
        s = cur_sc[...]
        if masked:
            qpos = qi * tq + jnp.bitwise_and(lax.broadcasted_iota(jnp.int32, (m_rows, 1), 0), tq - 1)
            s = jnp.where((ki * tk + lax.broadcasted_iota(jnp.int32, (1, tk), 1)) <= qpos, s, NEG_BIG)
        m_prev = m_sc[...]
        m_new = jnp.maximum(m_prev, _rep(_row_max(s)))
        alpha = jnp.exp2((m_prev - m_new) * c_exp)
        p = jnp.exp2((s - _tile_lanes(m_new, tk)) * c_exp)
        l_sc[...] = alpha * l_sc[...] + _fold_lanes(p, jnp.add)
        acc_sc[...] = _tile_lanes(alpha, MLA_KV_RANK) * acc_sc[...] + _dot(p.astype(BF16), k_ref[:, :MLA_KV_RANK])
        m_sc[...] = m_new

    crosses = (ki + 1) * tk - 1 > qi * tq
    even = lax.rem(ki, 2) == 0
    for par, (cur_sc, nxt_sc) in enumerate(((sa_sc, sb_sc), (sb_sc, sa_sc))):
        for masked in (True, False):
            cond = jnp.logical_and(ki < n_needed, jnp.logical_and(even == (par == 0), crosses == masked))
            pl.when(cond)(functools.partial(step, masked, cur_sc, nxt_sc))

    @pl.when(ki == n_needed - 1)
    def _():
        out = acc_sc[...] / jnp.maximum(jnp.sum(l_sc[...], axis=1, keepdims=True), 1e-30)
        o_ref[...] = out.reshape(MLA_HEADS, tq, MLA_KV_RANK).astype(o_ref.dtype)


def _mla_prompt_attention(q_ext, k_ext, batch, seq, tq, tk):
    nq, nk = seq // tq, seq // tk
    kw = q_ext.shape[-1]

    def k_map(off):
        def imap(b, qi, ki):
            last = ((qi + 1) * tq + tk - 1) // tk - 1
            return (b * nk + jnp.minimum(ki + off, last), 0)
        return imap

    return pl.pallas_call(
        functools.partial(_mla_attn_kernel, tq=tq, tk=tk),
        grid=(batch, nq, nk),
        in_specs=[pl.BlockSpec((MLA_HEADS, tq, kw), lambda b, qi, ki: (0, b * nq + qi, 0)),
                  pl.BlockSpec((tk, kw), k_map(0)), pl.BlockSpec((tk, kw), k_map(1))],
        out_specs=pl.BlockSpec((MLA_HEADS, tq, MLA_KV_RANK), lambda b, qi, ki: (0, b * nq + qi, 0)),
        out_shape=jax.ShapeDtypeStruct((MLA_HEADS, batch * seq, MLA_KV_RANK), BF16),
        scratch_shapes=[pltpu.VMEM((MLA_HEADS * tq, LANES), F32), pltpu.VMEM((MLA_HEADS * tq, LANES), F32),
                        pltpu.VMEM((MLA_HEADS * tq, MLA_KV_RANK), F32),
                        pltpu.VMEM((MLA_HEADS * tq, tk), F32), pltpu.VMEM((MLA_HEADS * tq, tk), F32)],
        compiler_params=_cparams(("parallel", "parallel", "arbitrary")),
        name="mla_prompt_attn",
    )(q_ext, k_ext, k_ext)


SEG_W = CMP_STRIDE * NSA_KV_W
KVP_W = 4 * LANES


def _prep_cmp_weights(w_cmp, pe_cmp):
    wh = w_cmp.reshape(2, 2, CMP_STRIDE, NSA_DH, NSA_DH)
    t = jnp.transpose(jnp.pad(wh, ((0, 0),) * 4 + ((0, LANES - NSA_DH),)), (1, 2, 0, 3, 4)).astype(BF16)
    eye_c = np.eye(2, dtype=np.float32).reshape(1, 1, 1, 2, 1, 2, 1, 1)
    eye_g = np.eye(NSA_KV_HEADS, dtype=np.float32).reshape(1, 1, NSA_KV_HEADS, 1, 1, 1, NSA_KV_HEADS, 1)
    mask = jnp.asarray(eye_c * eye_g, BF16)
    big = t[:, :, None, :, :, None, None, :] * mask
    w_big = big.reshape(2, SEG_W, KVP_W)
    peh = pe_cmp.reshape(2, 2, CMP_STRIDE, NSA_DH)
    pe_big = jnp.broadcast_to(jnp.transpose(peh, (1, 2, 0, 3))[:, :, None], (2, CMP_STRIDE, NSA_KV_HEADS, 2, NSA_DH))
    return w_big, pe_big.reshape(2, 1, SEG_W)


def _slc_overlap_matrix(n_cmp_rows, n_lanes):
    ratio = SLC_BLOCK // CMP_STRIDE
    lead = CMP_BLOCK // CMP_STRIDE - 1
    a = np.zeros((n_cmp_rows, n_lanes), np.float32)
    for j in range(n_lanes):
        for o in range(-lead, ratio):
            n = ratio * j + o
            if 0 <= n < n_cmp_rows:
                a[n, j] = (min(CMP_STRIDE * o + CMP_BLOCK, SLC_BLOCK) - max(CMP_STRIDE * o, 0)) / CMP_BLOCK
    return a


def _key_position_features(n_keys, n_lanes):
    kc = np.zeros((n_keys, LANES + n_lanes), np.float32)
    pos = np.arange(n_keys)
    kc[:, NSA_DH] = (pos // SLC_BLOCK) * SLC_BLOCK
    kc[:, NSA_DH + 1] = pos % SLC_BLOCK
    kc[pos, LANES + pos // SLC_BLOCK] = 1.0
    return kc


def _slopes_col(g, rows, tq):
    r = lax.shift_right_logical(lax.broadcasted_iota(jnp.int32, (rows, 1), 0), int(np.log2(tq)))
    col = jnp.zeros((rows, 1), F32)
    for rr in range(NSA_GROUP):
        col = jnp.where(r == rr, 2.0 ** (-(g * NSA_GROUP + rr + 1.0)), col)
    return col


def _softmax_rows(s, mask):
    n = s.shape[1]
    s = jnp.where(mask, s, NEG_BIG)
    m = _rep(_row_max(s))
    e = jnp.where(mask, jnp.exp(s - _tile_lanes(m, n)), 0.0)
    inv = 1.0 / jnp.maximum(_rep(_row_sum(e)), 1e-30)
    return e * _tile_lanes(inv, n)


def _top_blocks(score, n_pick):
    nb = score.shape[0]
    blk = lax.broadcasted_iota(jnp.int32, score.shape, 0).astype(F32)
    sel = jnp.zeros(score.shape, F32)
    for _ in range(n_pick):
        m = jnp.max(score, axis=0, keepdims=True)
        idx = jnp.min(jnp.where(score == m, blk, float(nb)), axis=0, keepdims=True)
        hit = blk == idx
        sel = jnp.where(jnp.logical_and(hit, m > 0.5 * NEG_BIG), 1.0, sel)
        score = jnp.where(hit, 2.0 * NEG_BIG, score)
    return sel


def _split3(x):
    hi = x.astype(BF16)
    r1 = x - hi.astype(F32)
    mid = r1.astype(BF16)
    lo = (r1 - mid.astype(F32)).astype(BF16)
    return hi, mid, lo


def _cmp_prompt_kernel(x_ref, pe_ref, w_ref, kvc_ref):
    x = x_ref[...]
    h1 = _dot((x + pe_ref[0]).astype(BF16), w_ref[0])
    h2 = _dot((x + pe_ref[1]).astype(BF16), w_ref[1])
    n = x.shape[0]
    kvc_ref[...] = (h1 + pltpu.roll(h2, n - 1, 0)).astype(kvc_ref.dtype)


def _cmp_prompt(kv_cmp, w_big, pe_big, batch, seq):
    nseg = seq // CMP_STRIDE
    x = kv_cmp.reshape(batch * nseg, SEG_W)
    return pl.pallas_call(
        _cmp_prompt_kernel,
        grid=(batch,),
        in_specs=[pl.BlockSpec((nseg, SEG_W), lambda b: (b, 0)), _full_spec(pe_big), _full_spec(w_big)],
        out_specs=pl.BlockSpec((nseg, KVP_W), lambda b: (b, 0)),
        out_shape=jax.ShapeDtypeStruct((batch * nseg, KVP_W), BF16),
        compiler_params=_cparams(("parallel",)),
        name="nsa_cmp_prompt",
    )(x, pe_big, w_big)


N_WIN_BLOCKS = WINDOW // 128 + 1


def _nsa_prompt_kernel(qn_ref, gl_ref, kvc_ref, amat_ref, slcp_ref, kc_ref, *rest, tq, tk, nseg):
    win_refs = rest[:N_WIN_BLOCKS]
    o_ref = rest[N_WIN_BLOCKS]
    stat = rest[N_WIN_BLOCKS + 1:N_WIN_BLOCKS + 1 + 3 * NSA_KV_HEADS]
    flag_ref = rest[-1]
    nt_all = slcp_ref.shape[0] // tk
    qi = pl.program_id(1)
    s0 = qi * tq
    rows = NSA_GROUP * tq
    nl = amat_ref.shape[1]
    qpos = s0 + jnp.bitwise_and(lax.broadcasted_iota(jnp.int32, (rows, 1), 0), tq - 1)
    qpos_t = s0 + lax.broadcasted_iota(jnp.int32, (tq, 1), 0)
    gates = jax.nn.sigmoid(gl_ref[...])
    n_tiles = (s0 + tq + tk - 1) // tk
    lane = lax.broadcasted_iota(jnp.int32, (1, LANES), 1)
    is_pos_lane = jnp.logical_or(lane == NSA_DH, lane == NSA_DH + 1)
    blk = lax.broadcasted_iota(jnp.int32, (1, nl), 1)
    cur = lax.shift_right_logical(qpos_t, int(np.log2(SLC_BLOCK)))
    valid = blk * SLC_BLOCK <= qpos_t
    forced = jnp.logical_or(blk == 0, jnp.logical_or(blk == cur, blk == cur - 1))
    cpos = lax.broadcasted_iota(jnp.int32, (1, nseg), 1) * CMP_STRIDE + (CMP_BLOCK - 1)
    dist_c = (qpos - cpos).astype(F32)
    amat = amat_ref[...]

    slopes, qgs, o_cs, q_augs = [], [], [], []
    any_sel = jnp.zeros((1, nl), F32)
    for g in range(NSA_KV_HEADS):
        slope = _slopes_col(g, rows, tq)
        q_heads = [qn_ref[:, (g * NSA_GROUP + r) * LANES:(g * NSA_GROUP + r + 1) * LANES] for r in range(NSA_GROUP)]
        qg = jnp.concatenate(q_heads, axis=0)
        kc = kvc_ref[:, g * LANES:(g + 1) * LANES]
        vc = kvc_ref[:, (2 + g) * LANES:(3 + g) * LANES]
        p_c = _softmax_rows(_dot_nt(qg, kc) - slope * dist_c, dist_c >= 0)
        o_cs.append(_dot(p_c.astype(BF16), vc))
        p_grp = p_c[0:tq]
        for r in range(1, NSA_GROUP):
            p_grp = p_grp + p_c[r * tq:(r + 1) * tq]
        s_slc = sum(_dot(t, amat) for t in _split3(p_grp))
        score = jnp.where(forced, FORCE_SCORE, jnp.where(valid, s_slc, NEG_BIG))
        sel = _top_blocks(score.T, N_SELECT).T
        selneg = ((sel - 1.0) * MASK_BIG).astype(BF16)
        any_sel = jnp.maximum(any_sel, jnp.max(sel, axis=0, keepdims=True))
        q_augs.append(jnp.concatenate(
            [jnp.concatenate([jnp.where(is_pos_lane, (2.0 ** (-(g * NSA_GROUP + r + 1.0))), q_heads[r].astype(F32)).astype(BF16),
                              selneg], axis=1) for r in range(NSA_GROUP)], axis=0))
        slopes.append(slope)
        qgs.append(qg)
        m_sc, l_sc, acc_sc = stat[3 * g:3 * g + 3]
        m_sc[...] = jnp.full(m_sc.shape, NEG_BIG, F32)
        l_sc[...] = jnp.zeros(l_sc.shape, F32)
        acc_sc[...] = jnp.zeros(acc_sc.shape, F32)
    tile_of_blk = lax.shift_right_logical(blk, int(np.log2(tk // SLC_BLOCK)))
    for t in range(nt_all):
        flag_ref[t] = jnp.max(jnp.where(tile_of_blk == t, any_sel, 0.0)).astype(jnp.int32)

    def slc_tile(kt, causal):
        k0 = pl.multiple_of(kt * tk, tk)
        if causal:
            ok = (k0 + lax.broadcasted_iota(jnp.int32, (1, tk), 1)) <= qpos
        for g in range(NSA_KV_HEADS):
            m_sc, l_sc, acc_sc = stat[3 * g:3 * g + 3]
            kk = slcp_ref[pl.ds(k0, tk), g * LANES:(g + 1) * LANES] + kc_ref[pl.ds(k0, tk), 0:LANES]
            k_aug = jnp.concatenate([kk, kc_ref[pl.ds(k0, tk), LANES:]], axis=1)
            vv = slcp_ref[pl.ds(k0, tk), (2 + g) * LANES:(3 + g) * LANES]
            s = _dot_nt(q_augs[g], k_aug)
            if causal:
                s = jnp.where(ok, s, NEG_BIG)
            m_prev = m_sc[...]
            m_new = jnp.maximum(m_prev, _rep(_row_max(s)))
            alpha = jnp.exp(m_prev - m_new)
            p = jnp.exp(s - _tile_lanes(m_new, tk))
            if causal:
                p = jnp.where(ok, p, 0.0)
            l_sc[...] = alpha * l_sc[...] + _fold_lanes(p, jnp.add)
            acc_sc[...] = alpha * acc_sc[...] + _dot(p.astype(BF16), vv)
            m_sc[...] = m_new

    def body(kt, carry):
        @pl.when(flag_ref[kt] > 0)
        def _():
            slc_tile(kt, False)
        return carry

    lax.fori_loop(0, n_tiles - 1, body, 0)
    slc_tile(n_tiles - 1, True)
    nw = N_WIN_BLOCKS * 128
    wpos = s0 - WINDOW + lax.broadcasted_iota(jnp.int32, (1, nw), 1)
    dist_w = (qpos - wpos).astype(F32)
    mask_w = jnp.logical_and(jnp.logical_and(dist_w >= 0, dist_w <= WINDOW), wpos >= 0)
    for g in range(NSA_KV_HEADS):
        m_sc, l_sc, acc_sc = stat[3 * g:3 * g + 3]
        o_s = acc_sc[...] / jnp.maximum(jnp.sum(l_sc[...], axis=1, keepdims=True), 1e-30)
        kw = jnp.concatenate([w[:, g * LANES:(g + 1) * LANES] for w in win_refs], axis=0)
        vw = jnp.concatenate([w[:, (2 + g) * LANES:(3 + g) * LANES] for w in win_refs], axis=0)
        p_w = _softmax_rows(_dot_nt(qgs[g], kw) - slopes[g] * dist_w, mask_w)
        o_w = _dot(p_w.astype(BF16), vw)
        for r in range(NSA_GROUP):
            hd = g * NSA_GROUP + r
            sl = slice(r * tq, (r + 1) * tq)
            o_ref[:, hd * LANES:(hd + 1) * LANES] = (
                gates[:, 3 * hd:3 * hd + 1] * o_cs[g][sl] + gates[:, 3 * hd + 1:3 * hd + 2] * o_s[sl]
                + gates[:, 3 * hd + 2:3 * hd + 3] * o_w[sl])


def _nsa_prompt(qn, gl, kvc, slcp, winp, batch, seq, tq, tk):
    assert tq == 128 and seq % tk == 0 and tk % tq == 0
    nseg = seq // CMP_STRIDE
    n_slc = seq // SLC_BLOCK
    nl = LANES * ((n_slc + LANES - 1) // LANES)
    assert nl == LANES, "selection blocks must fit one lane group"
    nq = seq // tq
    amat = jnp.asarray(_slc_overlap_matrix(nseg, nl), BF16)
    kc = jnp.asarray(_key_position_features(seq, nl), BF16)

    def win_spec(j):
        return pl.BlockSpec((128, KVP_W), lambda b, qi: (b * nq + jnp.maximum(qi - (N_WIN_BLOCKS - 1) + j, 0), 0))

    rows = NSA_GROUP * tq
    return pl.pallas_call(
        functools.partial(_nsa_prompt_kernel, tq=tq, tk=tk, nseg=nseg),
        grid=(batch, nq),
        in_specs=[pl.BlockSpec((tq, NSA_HEADS * LANES), lambda b, qi: (b * nq + qi, 0)),
                  pl.BlockSpec((tq, LANES), lambda b, qi: (b * nq + qi, 0)),
                  pl.BlockSpec((nseg, KVP_W), lambda b, qi: (b, 0)),
                  _full_spec(amat),
                  pl.BlockSpec((seq, KVP_W), lambda b, qi: (b, 0)),
                  _full_spec(kc)] + [win_spec(j) for j in range(N_WIN_BLOCKS)],
        out_specs=pl.BlockSpec((tq, NSA_HEADS * LANES), lambda b, qi: (b * nq + qi, 0)),
        out_shape=jax.ShapeDtypeStruct((batch * seq, NSA_HEADS * LANES), F32),
        scratch_shapes=[pltpu.VMEM((rows, LANES), F32), pltpu.VMEM((rows, LANES), F32), pltpu.VMEM((rows, LANES), F32)]
        * NSA_KV_HEADS + [pltpu.SMEM((seq // tk,), jnp.int32)],
        compiler_params=_cparams(("parallel", "arbitrary")),
        name="nsa_prompt",
    )(qn, gl, kvc, amat, slcp, kc, *([winp] * N_WIN_BLOCKS))


def _silu(x):
    return x * jax.nn.sigmoid(x)


def _even_out_kernel(x_ref, lat_ref, gmla_ref, onsa_ref, gnsa_ref, wuv_ref, wo_ref, o_ref):
    o_mla = jnp.concatenate([_dot(lat_ref[hd], wuv_ref[hd]) for hd in range(MLA_HEADS)], axis=1)
    a = (o_mla * _silu(gmla_ref[...])).astype(BF16)
    b = (onsa_ref[...] * _silu(gnsa_ref[...])).astype(BF16)
    nm = MLA_HEADS * LANES
    o_ref[...] = x_ref[...] + _dot(a, wo_ref[0:nm]) + _dot(b, wo_ref[nm:])


def _even_output(x, lat, gmla, onsa, gnsa, wuv_ext, wo_ext, tm):
    t = x.shape[0]
    row = lambda w: pl.BlockSpec((tm, w), lambda i: (i, 0))
    return pl.pallas_call(
        _even_out_kernel,
        grid=(t // tm,),
        in_specs=[row(D_MODEL), pl.BlockSpec((MLA_HEADS, tm, MLA_KV_RANK), lambda i: (0, i, 0)),
                  row(MLA_HEADS * LANES), row(NSA_HEADS * LANES), row(NSA_HEADS * LANES),
                  _full_spec(wuv_ext), _full_spec(wo_ext)],
        out_specs=row(D_MODEL),
        out_shape=jax.ShapeDtypeStruct((t, D_MODEL), F32),
        compiler_params=_cparams(("parallel",)),
        name="even_out",
    )(x, lat, gmla, onsa, gnsa, wuv_ext, wo_ext)


_RET_QK = RET_HEADS * RET_DK
_RET_LOG_G = [float(np.log1p(-(2.0 ** (-5.0 - h)))) for h in range(RET_HEADS)]


def _rope_tables_ret(pos):
    half = RET_DK // 2
    inv = ROPE_BASE ** (-jnp.arange(half, dtype=F32) / half)
    ang = pos.astype(F32)[:, None] * inv
    cos, sin = jnp.cos(ang), jnp.sin(ang)
    return jnp.concatenate([cos, cos], 1), jnp.concatenate([-sin, sin], 1)


def _odd_proj_kernel(x_ref, gn_ref, w_ref, c_ref, s_ref, q_ref, k_ref, v_ref, g_ref):
    h = _rms(x_ref[...], gn_ref[...]).astype(BF16)
    c = c_ref[...]
    s = s_ref[...]
    half = RET_DK // 2

    def rot(z):
        return z * c + pltpu.roll(z, half, 1) * s

    for hd in range(RET_HEADS):
        sl = slice(hd * RET_DK, (hd + 1) * RET_DK)
        q_ref[:, sl] = rot(_dot(h, w_ref[:, sl])).astype(BF16)
        ks = slice(_RET_QK + hd * RET_DK, _RET_QK + (hd + 1) * RET_DK)
        k_ref[:, sl] = (rot(_dot(h, w_ref[:, ks])) * (RET_DK ** -0.5)).astype(BF16)
    v_ref[...] = _dot(h, w_ref[:, 2 * _RET_QK:2 * _RET_QK + RET_WIDTH])
    g_ref[...] = _dot(h, w_ref[:, 2 * _RET_QK + RET_WIDTH:])


def _odd_project(x, gn, w_bf, cos, sin, tm):
    t = x.shape[0]
    ntab = cos.shape[0] // tm
    row = lambda w: pl.BlockSpec((tm, w), lambda i: (i, 0))
    tab = pl.BlockSpec((tm, RET_DK), lambda i: (i % ntab, 0))
    return pl.pallas_call(
        _odd_proj_kernel,
        grid=(t // tm,),
        in_specs=[row(D_MODEL), _full_spec(gn), _full_spec(w_bf), tab, tab],
        out_specs=(row(_RET_QK), row(_RET_QK), row(RET_WIDTH), row(RET_WIDTH)),
        out_shape=(jax.ShapeDtypeStruct((t, _RET_QK), BF16), jax.ShapeDtypeStruct((t, _RET_QK), BF16),
                   jax.ShapeDtypeStruct((t, RET_WIDTH), F32), jax.ShapeDtypeStruct((t, RET_WIDTH), F32)),
        compiler_params=_cparams(("parallel",)),
        name="odd_proj",
    )(x, gn, w_bf, cos, sin)


def _group_norm_gate(o, gn_row, g):
    mu = jnp.mean(o, axis=-1, keepdims=True)
    var = jnp.mean(jnp.square(o - mu), axis=-1, keepdims=True)
    return _silu(g) * ((o - mu) * lax.rsqrt(var + EPS) * gn_row)


def _ret_prompt_kernel(q_ref, k_ref, v_ref, g_ref, gn_ref, y_ref, sfin_ref, s_sc, *, chunk):
    ci = pl.program_id(1)

    @pl.when(ci == 0)
    def _():
        s_sc[...] = jnp.zeros(s_sc.shape, F32)

    n_col = lax.broadcasted_iota(jnp.int32, (chunk, 1), 0).astype(F32)
    diff = n_col - lax.broadcasted_iota(jnp.int32, (1, chunk), 1).astype(F32)
    for hd in range(RET_HEADS):
        lg = _RET_LOG_G[hd]
        dmat = jnp.where(diff >= 0, jnp.exp(lg * jnp.maximum(diff, 0.0)), 0.0)
        xi = jnp.exp(lg * (n_col + 1.0))
        zeta = jnp.exp(lg * (chunk - 1.0 - n_col))
        qh = q_ref[:, hd * RET_DK:(hd + 1) * RET_DK]
        kh = k_ref[:, hd * RET_DK:(hd + 1) * RET_DK]
        vh = v_ref[:, hd * RET_DV:(hd + 1) * RET_DV]
        s_prev = s_sc[hd]
        inner = _dot_nt(qh, kh) * dmat
        o = _dot(inner.astype(BF16), vh.astype(BF16)) + _dot(qh, s_prev.astype(BF16)) * xi
        kv = lax.dot_general(kh, (vh * zeta).astype(BF16), (((0,), (0,)), ((), ())), preferred_element_type=F32)
        s_sc[hd] = float(np.exp(lg * chunk)) * s_prev + kv
        y_ref[:, hd * RET_DV:(hd + 1) * RET_DV] = _group_norm_gate(
            o, gn_ref[hd:hd + 1, :], g_ref[:, hd * RET_DV:(hd + 1) * RET_DV]).astype(y_ref.dtype)

    @pl.when(ci == pl.num_programs(1) - 1)
    def _():
        sfin_ref[0] = s_sc[...]


def _retention_prompt(q, k, v, g, gn, batch, seq):
    chunk = RET_CHUNK
    nc = seq // chunk
    row = lambda w: pl.BlockSpec((chunk, w), lambda b, c: (b * nc + c, 0))
    return pl.pallas_call(
        functools.partial(_ret_prompt_kernel, chunk=chunk),
        grid=(batch, nc),
        in_specs=[row(_RET_QK), row(_RET_QK), row(RET_WIDTH), row(RET_WIDTH), _full_spec(gn)],
        out_specs=(row(RET_WIDTH),
                   pl.BlockSpec((1, RET_HEADS, RET_DK, RET_DV), lambda b, c: (b, 0, 0, 0))),
        out_shape=(jax.ShapeDtypeStruct((batch * seq, RET_WIDTH), BF16),
                   jax.ShapeDtypeStruct((batch, RET_HEADS, RET_DK, RET_DV), F32)),
        scratch_shapes=[pltpu.VMEM((RET_HEADS, RET_DK, RET_DV), F32)],
        compiler_params=_cparams(("parallel", "arbitrary")),
        name="retention_prompt",
    )(q, k, v, g, gn)


def _odd_out_kernel(x_ref, y_ref, w_ref, fn_ref, o_ref):
    x2 = x_ref[...] + _dot(y_ref[...], w_ref[...])
    o_ref[...] = _rms(x2, fn_ref[...])


def _odd_output(x, y, w_bf, fn, tm):
    t = x.shape[0]
    row = lambda w: pl.BlockSpec((tm, w), lambda i: (i, 0))
    return pl.pallas_call(
        _odd_out_kernel,
        grid=(t // tm,),
        in_specs=[row(D_MODEL), row(RET_WIDTH), _full_spec(w_bf), _full_spec(fn)],
        out_specs=row(D_MODEL),
        out_shape=jax.ShapeDtypeStruct((t, D_MODEL), F32),
        compiler_params=_cparams(("parallel",)),
        name="odd_out",
    )(x, y, w_bf, fn)


def _ret_decode_kernel(q_ref, k_ref, v_ref, g_ref, gn_ref, s_ref, y_ref, snew_ref):
    eye = (lax.broadcasted_iota(jnp.int32, (RET_DK, RET_DK), 0)
           == lax.broadcasted_iota(jnp.int32, (RET_DK, RET_DK), 1))
    for hd in range(RET_HEADS):
        gam = float(np.exp(_RET_LOG_G[hd]))
        qh = q_ref[0, :, hd * RET_DK:(hd + 1) * RET_DK]
        kh = k_ref[0, :, hd * RET_DK:(hd + 1) * RET_DK]
        vh = v_ref[0, :, hd * RET_DV:(hd + 1) * RET_DV].astype(BF16).astype(F32)
        s_prev = s_ref[0, hd]
        inner = jnp.sum(qh.astype(F32) * kh.astype(F32), axis=1, keepdims=True).astype(BF16).astype(F32)
        qs = _dot(jnp.broadcast_to(qh, (8, RET_DK)), s_prev.astype(BF16))[0:1]
        o = inner * vh + qs * gam
        k_col = jnp.sum(jnp.where(eye, jnp.broadcast_to(kh.astype(F32), (RET_DK, RET_DK)), 0.0), axis=1, keepdims=True)
        snew_ref[0, hd] = gam * s_prev + k_col * vh
        y_ref[0, :, hd * RET_DV:(hd + 1) * RET_DV] = _group_norm_gate(
            o, gn_ref[hd:hd + 1, :], g_ref[0, :, hd * RET_DV:(hd + 1) * RET_DV]).astype(y_ref.dtype)


def _retention_decode(q, k, v, g, gn, state):
    db = q.shape[0]
    r3 = lambda a: a.reshape(db, 1, a.shape[-1])
    vec = lambda w: pl.BlockSpec((1, 1, w), lambda b: (b, 0, 0))
    st = pl.BlockSpec((1, RET_HEADS, RET_DK, RET_DV), lambda b: (b, 0, 0, 0))
    y, s_new = pl.pallas_call(
        _ret_decode_kernel,
        grid=(db,),
        in_specs=[vec(_RET_QK), vec(_RET_QK), vec(RET_WIDTH), vec(RET_WIDTH), _full_spec(gn), st],
        out_specs=(vec(RET_WIDTH), st),
        out_shape=(jax.ShapeDtypeStruct((db, 1, RET_WIDTH), BF16),
                   jax.ShapeDtypeStruct(state.shape, state.dtype)),
        compiler_params=_cparams(("parallel",)),
        name="retention_decode",
    )(r3(q), r3(k), r3(v), r3(g), gn, state)
    return y.reshape(db, RET_WIDTH), s_new


PAGES_PER_STEP = 32
SUB = 8


PAGE_SLOTS = 3


def _page_group_dmas(pt_ref, pool_ref, buf_ref, sem_ref, grp, lookup):
    slot = lax.rem(grp, PAGE_SLOTS)
    copies = []
    for j in range(PAGES_PER_STEP):
        page = pt_ref[grp * PAGES_PER_STEP + j] if lookup else 0
        copies.append(pltpu.make_async_copy(pool_ref.at[page], buf_ref.at[slot, j], sem_ref.at[slot]))
    return copies


def _page_pipeline_step(pt_ref, pool_ref, buf_ref, sem_ref, steps):
    b = pl.program_id(0)
    i = pl.program_id(1)
    grp = b * steps + i
    n_grp = pl.num_programs(0) * steps

    @pl.when(jnp.logical_and(b == 0, i == 0))
    def _():
        for g0 in range(min(2, PAGE_SLOTS - 1)):
            @pl.when(g0 < n_grp)
            def _(g0=g0):
                for c in _page_group_dmas(pt_ref, pool_ref, buf_ref, sem_ref, g0, True):
                    c.start()

    @pl.when(i < steps)
    def _():
        for c in _page_group_dmas(pt_ref, pool_ref, buf_ref, sem_ref, grp, False):
            c.wait()

        @pl.when(grp + 2 < n_grp)
        def _():
            for c in _page_group_dmas(pt_ref, pool_ref, buf_ref, sem_ref, grp + 2, True):
                c.start()

    return lax.rem(grp, PAGE_SLOTS)


def _mla_decode_kernel(pt_ref, q_ref, new_ref, *rest, steps):
    pool_ref, o_ref, m_sc, l_sc, acc_sc, r0_sc, r1_sc, s0_sc, s1_sc, page_sc, page_sem = rest
    bufs = ((r0_sc, s0_sc), (r1_sc, s1_sc))
    i = pl.program_id(1)
    slot = _page_pipeline_step(pt_ref, pool_ref, page_sc, page_sem, steps)
    q = q_ref[0]

    def load(par):
        r_sc, s_sc = bufs[par]
        rows_t = jnp.concatenate([page_sc[slot, j] for j in range(PAGES_PER_STEP)], axis=1).astype(BF16)
        r_sc[...] = rows_t
        s_sc[...] = _dot(q, rows_t) * MLA_SCALE

    def proc(par):
        r_sc, s_sc = bufs[par]
        s = s_sc[...]
        m_prev = m_sc[...]
        m_new = jnp.maximum(m_prev, _row_max(s))
        alpha = jnp.exp(m_prev - m_new)
        p = jnp.exp(s - m_new)
        l_sc[...] = alpha * l_sc[...] + _row_sum(p)
        acc_sc[...] = alpha * acc_sc[...] + _dot_nt(p.astype(BF16), r_sc[0:MLA_KV_RANK, :])
        m_sc[...] = m_new

    @pl.when(i == 0)
    def _():
        m_sc[...] = jnp.full(m_sc.shape, NEG_BIG, F32)
        l_sc[...] = jnp.zeros(l_sc.shape, F32)
        acc_sc[...] = jnp.zeros(acc_sc.shape, F32)
        load(0)

    even = lax.rem(i, 2) == 0
    steady = jnp.logical_and(i > 0, i < steps)
    for par in range(2):
        @pl.when(jnp.logical_and(steady, even == (par == 0)))
        def _(par=par):
            load(par)
            proc(1 - par)

    @pl.when(i == steps)
    def _():
        proc((steps - 1) % 2)
        new = new_ref[0].astype(BF16).astype(F32)
        s_n = jnp.sum(q.astype(F32) * new, axis=1, keepdims=True) * MLA_SCALE
        m_prev = m_sc[...]
        m_new = jnp.maximum(m_prev, s_n)
        alpha = jnp.exp(m_prev - m_new)
        p_n = jnp.exp(s_n - m_new)
        l = alpha * l_sc[...] + p_n
        acc = alpha * acc_sc[...] + p_n.astype(BF16).astype(F32) * new[:, :MLA_KV_RANK]
        o_ref[0] = (acc / jnp.maximum(l, 1e-30)).astype(o_ref.dtype)


def _mla_decode(q_dec, rows_new, pool, page_table):
    db, n_pages = page_table.shape
    assert n_pages % PAGES_PER_STEP == 0
    steps = n_pages // PAGES_PER_STEP
    w = pool.shape[1]
    n_tok = PAGES_PER_STEP * PAGE_SIZE

    grid_spec = pltpu.PrefetchScalarGridSpec(
        num_scalar_prefetch=1,
        grid=(db, steps + 1),
        in_specs=[pl.BlockSpec((1, MLA_HEADS, w), lambda b, i, pt: (b, 0, 0)),
                  pl.BlockSpec((1, 1, w), lambda b, i, pt: (b, 0, 0)),
                  pl.BlockSpec(memory_space=pl.ANY)],
        out_specs=pl.BlockSpec((1, MLA_HEADS, MLA_KV_RANK), lambda b, i, pt: (b, 0, 0)),
        scratch_shapes=[pltpu.VMEM((MLA_HEADS, 1), F32), pltpu.VMEM((MLA_HEADS, 1), F32),
                        pltpu.VMEM((MLA_HEADS, MLA_KV_RANK), F32),
                        pltpu.VMEM((w, n_tok), BF16), pltpu.VMEM((w, n_tok), BF16),
                        pltpu.VMEM((MLA_HEADS, n_tok), F32), pltpu.VMEM((MLA_HEADS, n_tok), F32),
                        pltpu.VMEM((PAGE_SLOTS, PAGES_PER_STEP, w, PAGE_SIZE), F32),
                        pltpu.SemaphoreType.DMA((PAGE_SLOTS,))],
    )
    return pl.pallas_call(
        functools.partial(_mla_decode_kernel, steps=steps),
        grid_spec=grid_spec,
        out_shape=jax.ShapeDtypeStruct((db, MLA_HEADS, MLA_KV_RANK), BF16),
        compiler_params=_cparams(("arbitrary", "arbitrary")),
        name="mla_decode",
    )(page_table.reshape(-1), q_dec, rows_new.reshape(db, 1, w), pool)


def _decode_blocks(p_len, n_new):
    n_slc = -(-(p_len + n_new) // SLC_BLOCK)
    return n_slc, LANES * ((n_slc + LANES - 1) // LANES)


CMP_PAIRS = CMP_STRIDE // 2
GRP_W = 2 * NSA_DH


def _prep_cmp_pair_weights(w_cmp, pe_cmp):
    wh = w_cmp.reshape(2, 2, CMP_PAIRS, 2, NSA_DH, NSA_DH)
    t = jnp.transpose(wh, (2, 3, 0, 4, 1, 5)).astype(BF16)
    eye_c = jnp.asarray(np.eye(2, dtype=np.float32).reshape(1, 1, 2, 1, 1, 2, 1), BF16)
    big = t[:, :, :, :, :, None, :] * eye_c
    w_pair = big.reshape(CMP_PAIRS, 2 * GRP_W, 2 * GRP_W)
    peh = pe_cmp.reshape(2, 2, CMP_PAIRS, 2, NSA_DH)
    pe_pair = jnp.transpose(peh, (1, 2, 3, 0, 4)).reshape(2, CMP_PAIRS, 1, 2 * GRP_W)
    return w_pair, pe_pair


def _cmp_bias_kernel(pe_ref, w_ref, b_ref):
    for half in range(2):
        acc = jnp.zeros((SUB, GRP_W), F32)
        for jp in range(CMP_PAIRS):
            w = w_ref[jp][:, half * GRP_W:(half + 1) * GRP_W]
            for t in _split3(jnp.broadcast_to(pe_ref[half, jp], (SUB, 2 * GRP_W))):
                acc = acc + _dot(t, w)
        b_ref[half] = acc


def _cmp_bias(w_pair, pe_pair):
    return pl.pallas_call(
        _cmp_bias_kernel,
        grid=(1,),
        in_specs=[_full_spec(pe_pair), _full_spec(w_pair)],
        out_specs=pl.BlockSpec((2, SUB, GRP_W), lambda i: (0, 0, 0)),
        out_shape=jax.ShapeDtypeStruct((2, SUB, GRP_W), F32),
        compiler_params=_cparams(("arbitrary",)),
        name="nsa_cmp_bias",
    )(pe_pair, w_pair)


def _cmp_decode_kernel(pt_ref, qn_ref, cnew_ref, bias_ref, w_ref, amat_ref, *rest, p_len, steps):
    pool_ref, oc_ref, sslc_ref, xa_sc, xb_sc, h_sc, page_sc, page_sem = rest
    x_bufs = (xa_sc, xb_sc)
    i = pl.program_id(1)
    slot = _page_pipeline_step(pt_ref, pool_ref, page_sc, page_sem, steps)
    n_tok = PAGES_PER_STEP * PAGE_SIZE
    n_step = n_tok // CMP_STRIDE
    nseg = h_sc.shape[1]

    def load(par):
        x_sc = x_bufs[par]
        for j in range(PAGES_PER_STEP):
            for g in range(NSA_KV_HEADS):
                x_sc[g, j * PAGE_SIZE:(j + 1) * PAGE_SIZE, :] = page_sc[slot, j, g * GRP_W:(g + 1) * GRP_W, :].T

    def compress(par, step):
        x_sc = x_bufs[par]
        r0 = pl.multiple_of(step * n_step, n_step)
        for g in range(NSA_KV_HEADS):
            acc = jnp.zeros((n_step, 2 * GRP_W), F32)
            for jp in range(CMP_PAIRS):
                rows = [x_sc[g, pl.ds(2 * jp + jj, n_step, stride=CMP_STRIDE), :] for jj in range(2)]
                acc = acc + _dot(jnp.concatenate(rows, axis=1).astype(BF16), w_ref[jp])
            h_sc[g, pl.ds(r0, n_step), :] = acc

    @pl.when(i == 0)
    def _():
        load(0)

    even = lax.rem(i, 2) == 0
    steady = jnp.logical_and(i > 0, i < steps)
    for par in range(2):
        @pl.when(jnp.logical_and(steady, even == (par == 0)))
        def _(par=par):
            load(par)
            compress(1 - par, i - 1)

    @pl.when(i == steps)
    def _():
        compress((steps - 1) % 2, steps - 1)

    @pl.when(i == steps)
    def _():
        qpos = p_len
        cpos = lax.broadcasted_iota(jnp.int32, (1, nseg), 1) * CMP_STRIDE + (CMP_BLOCK - 1)
        dist = (qpos - cpos).astype(F32)
        row = lax.broadcasted_iota(jnp.int32, (SUB, 1), 0)
        seg = lax.broadcasted_iota(jnp.int32, (nseg, 1), 0)
        amat = amat_ref[...]
        b1 = bias_ref[0, 0:1, :]
        b2 = bias_ref[1, 0:1, :]
        for g in range(NSA_KV_HEADS):
            h = h_sc[g]
            new_seg = jnp.concatenate([cnew_ref[0, :, g * GRP_W:(g + 1) * GRP_W], jnp.zeros((1, GRP_W), F32)], axis=1)
            h2n = _dot(jnp.broadcast_to(new_seg, (SUB, 2 * GRP_W)).astype(BF16), w_ref[0])[0:1, GRP_W:] + b2
            h2 = jnp.where(seg == nseg - 1, h2n, pltpu.roll(h[:, GRP_W:], nseg - 1, 0) + b2)
            kvc = (h[:, :GRP_W] + b1 + h2).astype(BF16)
            slope = jnp.zeros((SUB, 1), F32)
            for rr in range(NSA_GROUP):
                slope = jnp.where(row == rr, 2.0 ** (-(g * NSA_GROUP + rr + 1.0)), slope)
            qg = qn_ref[0, g]
            p_c = _softmax_rows(_dot_nt(qg, kvc) - slope * dist, dist >= 0)
            p_c = jnp.where(row < NSA_GROUP, p_c, 0.0)
            oc_ref[0, g] = pltpu.roll(_dot(p_c.astype(BF16), kvc), LANES - NSA_DH, 1)
            s_rows = sum(_dot(t, amat) for t in _split3(p_c))
            sslc_ref[0, g] = jnp.broadcast_to(jnp.sum(s_rows, axis=0, keepdims=True), s_rows.shape)


def _cmp_decode(qn_dec, cmp_new, pool_t, page_table, w_pair, bias):
    db, n_pages = page_table.shape
    steps = n_pages // PAGES_PER_STEP
    p_len = n_pages * PAGE_SIZE
    nseg = p_len // CMP_STRIDE
    n_slc, nl = _decode_blocks(p_len, 1)
    amat = jnp.asarray(_slc_overlap_matrix(nseg, nl), BF16)

    grid_spec = pltpu.PrefetchScalarGridSpec(
        num_scalar_prefetch=1,
        grid=(db, steps + 1),
        in_specs=[pl.BlockSpec((1, NSA_KV_HEADS, SUB, LANES), lambda b, i, pt: (b, 0, 0, 0)),
                  pl.BlockSpec((1, 1, NSA_KV_W), lambda b, i, pt: (b, 0, 0)),
                  pl.BlockSpec(bias.shape, lambda b, i, pt: (0, 0, 0)),
                  pl.BlockSpec(w_pair.shape, lambda b, i, pt: (0, 0, 0)),
                  pl.BlockSpec(amat.shape, lambda b, i, pt: (0, 0)),
                  pl.BlockSpec(memory_space=pl.ANY)],
        out_specs=(pl.BlockSpec((1, NSA_KV_HEADS, SUB, LANES), lambda b, i, pt: (b, 0, 0, 0)),
                   pl.BlockSpec((1, NSA_KV_HEADS, SUB, nl), lambda b, i, pt: (b, 0, 0, 0))),
        scratch_shapes=[pltpu.VMEM((NSA_KV_HEADS, PAGES_PER_STEP * PAGE_SIZE, GRP_W), F32),
                        pltpu.VMEM((NSA_KV_HEADS, PAGES_PER_STEP * PAGE_SIZE, GRP_W), F32),
                        pltpu.VMEM((NSA_KV_HEADS, nseg, 2 * GRP_W), F32),
                        pltpu.VMEM((PAGE_SLOTS, PAGES_PER_STEP, NSA_KV_W, PAGE_SIZE), F32),
                        pltpu.SemaphoreType.DMA((PAGE_SLOTS,))],
    )
    return pl.pallas_call(
        functools.partial(_cmp_decode_kernel, p_len=p_len, steps=steps),
        grid_spec=grid_spec,
        out_shape=(jax.ShapeDtypeStruct((db, NSA_KV_HEADS, SUB, LANES), F32),
                   jax.ShapeDtypeStruct((db, NSA_KV_HEADS, SUB, nl), F32)),
        compiler_params=_cparams(("arbitrary", "arbitrary")),
        name="nsa_cmp_decode",
    )(page_table.reshape(-1), qn_dec, cmp_new.reshape(db, 1, NSA_KV_W), bias, w_pair, amat, pool_t)


def _topk_decode_kernel(s_ref, idx_ref, *, qpos):
    s_slc = s_ref[...]
    blk = lax.broadcasted_iota(jnp.int32, s_slc.shape, 1)
    cur = qpos // SLC_BLOCK
    valid = blk * SLC_BLOCK <= qpos
    forced = jnp.logical_or(blk == 0, jnp.logical_or(blk == cur, blk == cur - 1))
    score = jnp.where(forced, FORCE_SCORE, jnp.where(valid, s_slc, NEG_BIG))
    lane = lax.broadcasted_iota(jnp.int32, idx_ref.shape, 1)
    out = jnp.full(idx_ref.shape, -1, jnp.int32)
    for t in range(N_SELECT):
        m = jnp.max(score, axis=1, keepdims=True)
        idx = jnp.min(jnp.where(score == m, blk, score.shape[1]), axis=1, keepdims=True)
        out = jnp.where(lane == t, jnp.where(m > 0.5 * NEG_BIG, idx, -1), out)
        score = jnp.where(blk == idx, 2.0 * NEG_BIG, score)
    idx_ref[...] = out


def _topk_decode(s_slc, qpos):
    rows = s_slc.shape[0]
    return pl.pallas_call(
        functools.partial(_topk_decode_kernel, qpos=qpos),
        grid=(1,),
        in_specs=[_full_spec(s_slc)],
        out_specs=pl.BlockSpec((rows, LANES), lambda i: (0, 0)),
        out_shape=jax.ShapeDtypeStruct((rows, LANES), jnp.int32),
        compiler_params=_cparams(("arbitrary",)),
        name="nsa_topk_decode",
    )(s_slc)


def _sel_decode_kernel(pt_ref, sel_ref, qn_ref, gl_ref, oc_ref, snew_ref, wnew_ref, wnewt_ref, win_ref, *rest, p_len):
    del pt_ref
    nb = NSA_KV_HEADS * N_SELECT
    blk_refs = rest[:nb]
    o_ref, nwin_ref = rest[nb:]
    b = pl.program_id(0)
    qpos = p_len
    n_past_blocks = p_len // SLC_BLOCK
    per_page = PAGE_SIZE // SLC_BLOCK
    row = lax.broadcasted_iota(jnp.int32, (SUB, 1), 0)
    lane = lax.broadcasted_iota(jnp.int32, (SUB, LANES), 1)
    gates = jnp.broadcast_to(jax.nn.sigmoid(gl_ref[0]), (SUB, LANES))
    nw = win_ref.shape[2]
    win = win_ref[0]
    wnew = wnew_ref[0]
    snew = snew_ref[0]

    def merge_new(s, mask, kv_t, q8, new_row, slope, use_new):
        new_f = new_row.astype(BF16).astype(F32)
        s_n = jnp.sum(q8.astype(F32) * new_f, axis=1, keepdims=True)
        s = jnp.where(mask, s, NEG_BIG)
        m = jnp.maximum(_row_max(s), jnp.where(use_new, s_n, NEG_BIG))
        e = jnp.where(mask, jnp.exp(s - m), 0.0)
        e_n = jnp.where(use_new, jnp.exp(s_n - m), 0.0)
        denom = jnp.maximum(_row_sum(e) + e_n, 1e-30)
        p = e / denom
        p_n = e_n / denom
        o = _dot_nt(p.astype(BF16), kv_t) + p_n.astype(BF16).astype(F32) * new_f
        return pltpu.roll(o, LANES - NSA_DH, 1)

    for g in range(NSA_KV_HEADS):
        slope = jnp.zeros((SUB, 1), F32)
        for rr in range(NSA_GROUP):
            slope = jnp.where(row == rr, 2.0 ** (-(g * NSA_GROUP + rr + 1.0)), slope)
        q8 = qn_ref[0, g]
        kv_t = jnp.concatenate([r[...] for r in blk_refs[g * N_SELECT:(g + 1) * N_SELECT]], axis=1).astype(BF16)
        nk = N_SELECT * PAGE_SIZE
        key = lax.broadcasted_iota(jnp.int32, (1, nk), 1)
        slot = lax.shift_right_logical(key, int(np.log2(PAGE_SIZE)))
        blk_in_page = jnp.bitwise_and(lax.shift_right_logical(key, int(np.log2(SLC_BLOCK))), per_page - 1)
        off = jnp.bitwise_and(key, SLC_BLOCK - 1)
        kblk = jnp.zeros((1, nk), jnp.int32)
        use_new = False
        for t in range(N_SELECT):
            st = sel_ref[(b * NSA_KV_HEADS + g) * N_SELECT + t]
            kblk = jnp.where(slot == t, st, kblk)
            use_new = jnp.logical_or(use_new, st == n_past_blocks)
        kpos = kblk * SLC_BLOCK + off
        ok = jnp.logical_and(jnp.logical_and(kblk >= 0, kblk < n_past_blocks),
                             jnp.logical_and(blk_in_page == jnp.bitwise_and(kblk, per_page - 1), kpos <= qpos))
        dist = (qpos - kpos).astype(F32)
        o_s = merge_new(_dot(q8, kv_t) - slope * dist, ok, kv_t, q8, snew[:, g * LANES:(g + 1) * LANES], slope, use_new)
        kvw_t = win[g * LANES:(g + 1) * LANES, :].astype(BF16)
        wpos = p_len - nw + lax.broadcasted_iota(jnp.int32, (1, nw), 1)
        dist_w = (qpos - wpos).astype(F32)
        mask_w = jnp.logical_and(jnp.logical_and(dist_w >= 0, dist_w <= WINDOW), wpos >= 0)
        o_w = merge_new(_dot(q8, kvw_t) - slope * dist_w, mask_w, kvw_t, q8, wnew[:, g * LANES:(g + 1) * LANES], slope, True)
        hd = g * NSA_GROUP + row
        gate = lambda c: jnp.sum(jnp.where(lane == 3 * hd + c, gates, 0.0), axis=1, keepdims=True)
        mix = gate(0) * oc_ref[0, g] + gate(1) * o_s + gate(2) * o_w
        o_ref[0, g * NSA_GROUP:(g + 1) * NSA_GROUP, :] = mix[0:NSA_GROUP]
    req = lax.broadcasted_iota(jnp.int32, wnewt_ref.shape, 1)
    col = jnp.sum(jnp.where(req == b, wnewt_ref[...], 0.0), axis=1, keepdims=True)
    t_idx = lax.broadcasted_iota(jnp.int32, (1, nw), 1)
    nwin_ref[0] = jnp.where(t_idx == nw - 1, col, pltpu.roll(win, nw - 1, 1))


def _sel_decode(qn_dec, gl, o_c, sel_idx, slc_new, win_new, slc_pool_t, win_buf_t, page_table):
    db, n_pages = page_table.shape
    p_len = n_pages * PAGE_SIZE
    n_past_blocks = p_len // SLC_BLOCK
    per_page = PAGE_SIZE // SLC_BLOCK
    nw = win_buf_t.shape[2]
    win_new_t = jnp.transpose(win_new)

    def blk_spec(g, t):
        def imap(b, pt, sel):
            j = jnp.clip(sel[(b * NSA_KV_HEADS + g) * N_SELECT + t], 0, n_past_blocks - 1)
            return (pt[b * n_pages + j // per_page], g, 0)
        return pl.BlockSpec((None, GRP_W, PAGE_SIZE), imap)

    vec = lambda w: pl.BlockSpec((1, 1, w), lambda b, pt, sel: (b, 0, 0))
    grp = pl.BlockSpec((1, NSA_KV_HEADS, SUB, LANES), lambda b, pt, sel: (b, 0, 0, 0))
    grid_spec = pltpu.PrefetchScalarGridSpec(
        num_scalar_prefetch=2,
        grid=(db,),
        in_specs=[grp, vec(LANES), grp, vec(NSA_KV_W), vec(NSA_KV_W),
                  pl.BlockSpec(win_new_t.shape, lambda b, pt, sel: (0, 0)),
                  pl.BlockSpec((1, NSA_KV_W, nw), lambda b, pt, sel: (b, 0, 0))]
                 + [blk_spec(g, t) for g in range(NSA_KV_HEADS) for t in range(N_SELECT)],
        out_specs=(pl.BlockSpec((1, NSA_HEADS, LANES), lambda b, pt, sel: (b, 0, 0)),
                   pl.BlockSpec((1, NSA_KV_W, nw), lambda b, pt, sel: (b, 0, 0))),
    )
    r3 = lambda a: a.reshape(db, 1, a.shape[-1])
    return pl.pallas_call(
        functools.partial(_sel_decode_kernel, p_len=p_len),
        grid_spec=grid_spec,
        out_shape=(jax.ShapeDtypeStruct((db, NSA_HEADS, LANES), F32),
                   jax.ShapeDtypeStruct(win_buf_t.shape, win_buf_t.dtype)),
        compiler_params=_cparams(("parallel",)),
        name="nsa_sel_decode",
    )(page_table.reshape(-1), sel_idx, qn_dec, r3(gl), o_c, r3(slc_new), r3(win_new), win_new_t, win_buf_t,
      *([slc_pool_t] * (NSA_KV_HEADS * N_SELECT)))


TM_PROMPT = 256
TQ_MLA, TK_MLA = 512, 512
TQ_NSA, TK_NSA = 128, 512


def _mla_rows(rows):
    return jnp.concatenate([rows[:, :MLA_KV_RANK], rows[:, MLA_KV_RANK:MLA_KV_RANK + HALF_ROPE],
                            rows[:, MLA_KV_RANK + LANES:MLA_KV_RANK + LANES + HALF_ROPE]], axis=1)


def kernel(x_prompt, x_sample, cache_mla, cache_nsa_cmp, cache_nsa_slc, state_nsa_win, state_ret, page_table,
           norm_even, w_in_even, mla_gq, mla_gkv, mla_wuq, mla_wuk, mla_wuv, nsa_cmp_w, nsa_cmp_pe, w_out_even,
           norm_odd, w_in_odd, ret_gn, w_out_odd, final_norm):
    b, s, d = x_prompt.shape
    db, n_new, _ = x_sample.shape
    assert n_new == 1 and norm_even.shape[0] == 1 and norm_odd.shape[0] == 1
    n_pages = page_table.shape[1]
    p_len = n_pages * PAGE_SIZE
    kv_row = (NSA_KV_HEADS, 2, NSA_DH)
    tm = min(TM_PROMPT, b * s)

    w_ext, wuq_ext, wuk_ext, wuv_ext, wo_ext = _prep_even_weights(
        w_in_even[0], mla_wuq[0], mla_wuk[0], mla_wuv[0], w_out_even[0])
    w_big, pe_big = _prep_cmp_weights(nsa_cmp_w[0], nsa_cmp_pe[0])
    w_odd = w_in_odd[0].astype(BF16)
    wo_odd = w_out_odd[0].astype(BF16)
    gn_e, gq, gkv = norm_even[0][None], mla_gq[0][None], mla_gkv[0][None]
    gn_o, fn = norm_odd[0][None], final_norm[None]

    xp = x_prompt.reshape(b * s, d)
    pos_p = jnp.arange(s)
    cos_m, sin_m = _rope_tables_mla(pos_p)
    (qext, rows, kext, gmla, qn, cmp, _, slcp, win, winp, gl, gnsa, cmp_t, slc_t) = _even_project(
        xp, gn_e, w_ext, gq, gkv, wuq_ext, wuk_ext, cos_m, sin_m, tm, s)
    lat = _mla_prompt_attention(qext, kext, b, s, min(TQ_MLA, s), min(TK_MLA, s))
    kvc = _cmp_prompt(cmp, w_big, pe_big, b, s)
    onsa = _nsa_prompt(qn, gl, kvc, slcp, winp, b, s, TQ_NSA, min(TK_NSA, s))
    x1 = _even_output(xp, lat, gmla, onsa, gnsa, wuv_ext, wo_ext, tm)
    cos_r, sin_r = _rope_tables_ret(pos_p)
    q, k, v, g = _odd_project(x1, gn_o, w_odd, cos_r, sin_r, tm)
    y, ret_p = _retention_prompt(q, k, v, g, ret_gn[0], b, s)
    y_prompt = _odd_output(x1, y, wo_odd, fn, tm).reshape(b, s, d)
    nwin = min(WINDOW, s)
    mla_p = _mla_rows(rows).reshape(1, b, s, MLA_KV_RANK + MLA_ROPE)
    cache_rows = lambda a_t: jnp.transpose(a_t.reshape((b,) + kv_row + (s,)), (0, 4, 1, 2, 3))[None]
    cmp_p = cache_rows(cmp_t)
    slc_p = cache_rows(slc_t)
    win_p = win.reshape((b, s) + kv_row)[:, s - nwin:][None]

    xs = x_sample.reshape(db, d)
    pos_s = p_len + jnp.arange(n_new)
    cos_s, sin_s = [jnp.broadcast_to(t, (db, LANES)) for t in _rope_tables_mla(pos_s)]
    (qext_s, rows_s, _, gmla_s, qn_s, cmp_s, slc_s, _, win_s, _, gl_s, gnsa_s, _, _) = _even_project(
        xs, gn_e, w_ext, gq, gkv, wuq_ext, wuk_ext, cos_s, sin_s, db, db)
    rows_new = _mla_rows(rows_s)
    rope = lambda a: a.astype(F32).reshape(MLA_HEADS, db, MLA_HEADS, HALF_ROPE).sum(2).astype(BF16)
    q_dec = jnp.concatenate([qext_s[:, :, :MLA_KV_RANK], rope(qext_s[:, :, MLA_KV_RANK:MLA_KV_RANK + LANES]),
                             rope(qext_s[:, :, MLA_KV_RANK + LANES:])], axis=-1)
    pool = cache_nsa_cmp.shape[1]
    feat_major = lambda a, n: jnp.transpose(a[0], (0, 2, 3, 4, 1)).reshape(n, NSA_KV_W, a.shape[2])
    mla_pool_t = jnp.transpose(cache_mla[0], (0, 2, 1))
    lat_s = _mla_decode(jnp.transpose(q_dec, (1, 0, 2)), rows_new, mla_pool_t, page_table)
    qn_dec = jnp.pad(qn_s.reshape(db, NSA_KV_HEADS, NSA_GROUP, LANES), ((0, 0), (0, 0), (0, SUB - NSA_GROUP), (0, 0)))
    w_pair, pe_pair = _prep_cmp_pair_weights(nsa_cmp_w[0], nsa_cmp_pe[0])
    o_c, s_slc = _cmp_decode(qn_dec, cmp_s, feat_major(cache_nsa_cmp, pool), page_table, w_pair,
                             _cmp_bias(w_pair, pe_pair))
    sel = _topk_decode(s_slc[:, :, 0, :].reshape(db * NSA_KV_HEADS, -1), p_len)
    sel_idx = sel[:, :N_SELECT].reshape(-1)
    onsa_s, win_new_t = _sel_decode(qn_dec, gl_s, o_c, sel_idx, slc_s, win_s, feat_major(cache_nsa_slc, pool),
                                    feat_major(state_nsa_win, db), page_table)
    nw = win_new_t.shape[2]
    win_so = jnp.transpose(win_new_t.reshape((db,) + kv_row + (nw,)), (0, 4, 1, 2, 3))[None]
    x1s = _even_output(xs, jnp.transpose(lat_s, (1, 0, 2)), gmla_s, onsa_s.reshape(db, NSA_HEADS * LANES), gnsa_s,
                       wuv_ext, wo_ext, db)
    cos_rs, sin_rs = [jnp.broadcast_to(t, (db, RET_DK)) for t in _rope_tables_ret(pos_s)]
    q, k, v, g = _odd_project(x1s, gn_o, w_odd, cos_rs, sin_rs, db)
    ys, ret_s = _retention_decode(q, k, v, g, ret_gn[0], state_ret.reshape(state_ret.shape[1:]))
    y_sample = _odd_output(x1s, ys, wo_odd, fn, db).reshape(db, n_new, d)
    mla_s = rows_new.reshape(1, db, n_new, MLA_KV_RANK + MLA_ROPE)
    cmp_so = cmp_s.reshape((1, db, n_new) + kv_row)
    slc_so = slc_s.reshape((1, db, n_new) + kv_row)
    return (y_prompt, y_sample, mla_p, cmp_p, slc_p, win_p, ret_p[None],
            mla_s, cmp_so, slc_so, win_so, ret_s[None])
```

```python
import functools

import numpy as np
import jax
import jax.numpy as jnp
from jax import lax
from jax.experimental import pallas as pl
from jax.experimental.pallas import tpu as pltpu

F32 = jnp.float32
BF16 = jnp.bfloat16

D_MODEL = 1024
PAGE_SIZE = 128
MLA_HEADS = 8
MLA_NOPE = 64
MLA_ROPE = 32
MLA_V = 64
MLA_Q_RANK = 768
MLA_KV_RANK = 256
MLA_WIDTH = MLA_HEADS * MLA_V
MLA_SCALE = (MLA_NOPE + MLA_ROPE) ** -0.5
NSA_HEADS = 8
NSA_KV_HEADS = 2
NSA_GROUP = NSA_HEADS // NSA_KV_HEADS
NSA_DH = 64
NSA_WIDTH = NSA_HEADS * NSA_DH
NSA_KV_W = NSA_KV_HEADS * 2 * NSA_DH
CMP_BLOCK = 32
CMP_STRIDE = 16
SLC_BLOCK = 64
N_SELECT = 16
WINDOW = 512
FORCE_SCORE = 1e4
RET_HEADS = 8
RET_DK = 128
RET_DV = 256
RET_WIDTH = RET_HEADS * RET_DV
RET_CHUNK = 128
ROPE_BASE = 10000.0
EPS = 1e-6
EVEN_SPLITS = (MLA_Q_RANK, MLA_KV_RANK, MLA_ROPE, MLA_WIDTH, NSA_WIDTH, NSA_KV_W, NSA_KV_W, NSA_KV_W,
               3 * NSA_HEADS, NSA_WIDTH)

LANES = 128
VMEM_LIMIT_BYTES = 56 * 1024 * 1024
NEG_BIG = -1e30
MASK_BIG = 16384.0

HALF_ROPE = MLA_ROPE // 2


def _cparams(sem):
    return pltpu.CompilerParams(dimension_semantics=sem, vmem_limit_bytes=VMEM_LIMIT_BYTES)


def _full_spec(a):
    nd = a.ndim
    return pl.BlockSpec(a.shape, lambda *_: (0,) * nd)


def _rms(x, g):
    y = x * lax.rsqrt(jnp.mean(x * x, axis=-1, keepdims=True) + EPS)
    return y * g


def _dot(a, b):
    return jnp.dot(a, b, preferred_element_type=F32)


def _dot_nt(a, b):
    return lax.dot_general(a, b, (((1,), (1,)), ((), ())), preferred_element_type=F32)


def _fold_lanes(x, op):
    n = x.shape[1]
    if n % LANES:
        return x
    parts = [x[:, i:i + LANES] for i in range(0, n, LANES)]
    while len(parts) > 1:
        parts = [op(parts[i], parts[i + 1]) if i + 1 < len(parts) else parts[i] for i in range(0, len(parts), 2)]
    return parts[0]


def _row_max(x):
    return jnp.max(_fold_lanes(x, jnp.maximum), axis=1, keepdims=True)


def _row_sum(x):
    return jnp.sum(_fold_lanes(x, jnp.add), axis=1, keepdims=True)


def _rep(col):
    return jnp.broadcast_to(col, (col.shape[0], LANES))


def _tile_lanes(x, n):
    if n % LANES:
        return jnp.broadcast_to(x[:, :1], (x.shape[0], n))
    return x if n == LANES else jnp.tile(x, (1, n // LANES))


_EVEN_GROUPS = (
    ("cq", MLA_Q_RANK), ("ckv", MLA_KV_RANK), ("kr1", LANES), ("kr2", LANES),
    ("gmla", MLA_HEADS * LANES), ("qn", NSA_HEADS * LANES), ("cmp", NSA_KV_W),
    ("slc", NSA_KV_W), ("slcp", 4 * LANES), ("win", NSA_KV_W), ("winp", 4 * LANES),
    ("gl", LANES), ("gnsa", NSA_HEADS * LANES),
)
_EVEN_OFF = {}
_o = 0
for _n, _w in _EVEN_GROUPS:
    _EVEN_OFF[_n] = (_o, _o + _w)
    _o += _w
EVEN_EXT = _o


def _pad_heads(w, nh, dh):
    k = w.shape[0]
    w = w.reshape(k, nh, dh)
    return jnp.pad(w, ((0, 0), (0, 0), (0, LANES - dh))).reshape(k, nh * LANES)


def _kv_pad(w):
    k = w.shape[0]
    w4 = w.reshape(k, NSA_KV_HEADS, 2, NSA_DH)
    w4 = jnp.transpose(w4, (0, 2, 1, 3))
    return jnp.pad(w4, ((0, 0), (0, 0), (0, 0), (0, LANES - NSA_DH))).reshape(k, 4 * LANES)


def _prep_even_weights(w_in, wuq, wuk, wuv, w_out):
    offs = np.cumsum((0,) + EVEN_SPLITS)
    cq, ckv, kr, g_mla, q_nsa, kv_cmp, kv_slc, kv_win, gl, g_nsa = [
        w_in[:, offs[i]:offs[i + 1]] for i in range(len(EVEN_SPLITS))]
    parts = {
        "cq": cq, "ckv": ckv,
        "kr1": jnp.tile(kr[:, :HALF_ROPE], (1, MLA_HEADS)),
        "kr2": jnp.tile(kr[:, HALF_ROPE:], (1, MLA_HEADS)),
        "gmla": _pad_heads(g_mla, MLA_HEADS, MLA_V),
        "qn": _pad_heads(q_nsa, NSA_HEADS, NSA_DH),
        "cmp": kv_cmp, "slc": kv_slc, "slcp": _kv_pad(kv_slc),
        "win": kv_win, "winp": _kv_pad(kv_win),
        "gl": jnp.pad(gl, ((0, 0), (0, LANES - gl.shape[1]))),
        "gnsa": _pad_heads(g_nsa, NSA_HEADS, NSA_DH),
    }
    w_ext = jnp.concatenate([parts[n] for n, _ in _EVEN_GROUPS], axis=1).astype(BF16)
    c = wuq.shape[0]
    nope = jnp.pad(wuq[:, :, :MLA_NOPE], ((0, 0), (0, 0), (0, LANES - MLA_NOPE))).reshape(c, MLA_HEADS * LANES)
    r1 = wuq[:, :, MLA_NOPE:MLA_NOPE + HALF_ROPE].reshape(c, MLA_HEADS * HALF_ROPE)
    r2 = wuq[:, :, MLA_NOPE + HALF_ROPE:].reshape(c, MLA_HEADS * HALF_ROPE)
    wuq_ext = jnp.concatenate([nope, r1, r2], axis=1).astype(BF16)
    wuk_ext = jnp.pad(jnp.transpose(wuk, (1, 2, 0)), ((0, 0), (0, LANES - MLA_NOPE), (0, 0))).astype(BF16)
    wuv_ext = jnp.pad(jnp.transpose(wuv, (1, 0, 2)), ((0, 0), (0, 0), (0, LANES - MLA_V))).astype(BF16)
    d = w_out.shape[1]
    wo = w_out.reshape(MLA_HEADS + NSA_HEADS, MLA_V, d)
    wo_ext = jnp.pad(wo, ((0, 0), (0, LANES - MLA_V), (0, 0))).reshape((MLA_HEADS + NSA_HEADS) * LANES, d).astype(BF16)
    return w_ext, wuq_ext, wuk_ext, wuv_ext, wo_ext


def _rope_tables_mla(pos):
    inv = ROPE_BASE ** (-jnp.arange(HALF_ROPE, dtype=F32) / HALF_ROPE)
    ang = pos.astype(F32)[:, None] * inv
    return jnp.tile(jnp.cos(ang), (1, LANES // HALF_ROPE)), jnp.tile(jnp.sin(ang), (1, LANES // HALF_ROPE))


def _even_proj_kernel(x_ref, gn_ref, w_ref, wt_ref, gq_ref, gkv_ref, wuq_ref, wuk_ref, cos_ref, sin_ref,
                      qext_ref, rows_ref, kext_ref, gmla_ref, qn_ref, cmp_ref, slc_ref, slcp_ref,
                      win_ref, winp_ref, gl_ref, gnsa_ref, cmpt_ref, slct_ref):
    h = _rms(x_ref[...], gn_ref[...]).astype(BF16)
    cmpt_ref[...] = _dot_nt(wt_ref[0:NSA_KV_W, :], h)
    slct_ref[...] = _dot_nt(wt_ref[NSA_KV_W:, :], h)

    def proj(name):
        a, b = _EVEN_OFF[name]
        return _dot(h, w_ref[:, a:b])

    cos = cos_ref[...]
    sin = sin_ref[...]
    cqn = _rms(proj("cq"), gq_ref[...]).astype(BF16)
    nh = MLA_HEADS * LANES
    r1 = _dot(cqn, wuq_ref[:, nh:nh + LANES])
    r2 = _dot(cqn, wuq_ref[:, nh + LANES:nh + 2 * LANES])
    o1 = r1 * cos - r2 * sin
    o2 = r1 * sin + r2 * cos
    head_of_lane = lax.broadcasted_iota(jnp.int32, (1, LANES), 1) // HALF_ROPE
    for hd in range(MLA_HEADS):
        nope = _dot(cqn, wuq_ref[:, hd * LANES:(hd + 1) * LANES]).astype(BF16)
        qext_ref[hd, :, 0:MLA_KV_RANK] = _dot(nope, wuk_ref[hd]).astype(BF16)
        sel = head_of_lane == hd
        qext_ref[hd, :, MLA_KV_RANK:MLA_KV_RANK + LANES] = jnp.where(sel, o1, 0.0).astype(BF16)
        qext_ref[hd, :, MLA_KV_RANK + LANES:] = jnp.where(sel, o2, 0.0).astype(BF16)
    latn = _rms(proj("ckv"), gkv_ref[...])
    kr1 = proj("kr1")
    kr2 = proj("kr2")
    k1 = kr1 * cos - kr2 * sin
    k2 = kr1 * sin + kr2 * cos
    rows_ref[:, 0:MLA_KV_RANK] = latn
    rows_ref[:, MLA_KV_RANK:MLA_KV_RANK + LANES] = k1
    rows_ref[:, MLA_KV_RANK + LANES:] = k2
    kext_ref[:, 0:MLA_KV_RANK] = latn.astype(BF16)
    kext_ref[:, MLA_KV_RANK:MLA_KV_RANK + LANES] = k1.astype(BF16)
    kext_ref[:, MLA_KV_RANK + LANES:] = k2.astype(BF16)
    gmla_ref[...] = proj("gmla")
    qn_ref[...] = (proj("qn") * (NSA_DH ** -0.5)).astype(BF16)
    cmp_ref[...] = proj("cmp")
    slc_ref[...] = proj("slc")
    slcp_ref[...] = proj("slcp").astype(BF16)
    win_ref[...] = proj("win")
    winp_ref[...] = proj("winp").astype(BF16)
    gl_ref[...] = proj("gl")
    gnsa_ref[...] = proj("gnsa")


def _even_project(x, gn, w_ext, gq, gkv, wuq_ext, wuk_ext, cos, sin, tm, seq):
    t = x.shape[0]
    nt = t // tm
    nst = seq // tm
    a, b = _EVEN_OFF["cmp"][0], _EVEN_OFF["slc"][1]
    w_t = jnp.transpose(w_ext[:, a:b])
    ntab = cos.shape[0] // tm
    row = lambda w: pl.BlockSpec((tm, w), lambda i: (i, 0))
    tab = pl.BlockSpec((tm, LANES), lambda i: (i % ntab, 0))
    kext_w = MLA_KV_RANK + 2 * LANES
    out_shapes = (
        jax.ShapeDtypeStruct((MLA_HEADS, t, kext_w), BF16),
        jax.ShapeDtypeStruct((t, kext_w), F32),
        jax.ShapeDtypeStruct((t, kext_w), BF16),
        jax.ShapeDtypeStruct((t, MLA_HEADS * LANES), F32),
        jax.ShapeDtypeStruct((t, NSA_HEADS * LANES), BF16),
        jax.ShapeDtypeStruct((t, NSA_KV_W), F32),
        jax.ShapeDtypeStruct((t, NSA_KV_W), F32),
        jax.ShapeDtypeStruct((t, 4 * LANES), BF16),
        jax.ShapeDtypeStruct((t, NSA_KV_W), F32),
        jax.ShapeDtypeStruct((t, 4 * LANES), BF16),
        jax.ShapeDtypeStruct((t, LANES), F32),
        jax.ShapeDtypeStruct((t, NSA_HEADS * LANES), F32),
        jax.ShapeDtypeStruct((t // seq, NSA_KV_W, seq), F32),
        jax.ShapeDtypeStruct((t // seq, NSA_KV_W, seq), F32),
    )
    fm = pl.BlockSpec((None, NSA_KV_W, tm), lambda i: (i // nst, 0, i % nst))
    out_specs = (
        pl.BlockSpec((MLA_HEADS, tm, kext_w), lambda i: (0, i, 0)),
        row(kext_w), row(kext_w), row(MLA_HEADS * LANES), row(NSA_HEADS * LANES), row(NSA_KV_W),
        row(NSA_KV_W), row(4 * LANES), row(NSA_KV_W), row(4 * LANES), row(LANES), row(NSA_HEADS * LANES), fm, fm,
    )
    return pl.pallas_call(
        _even_proj_kernel,
        grid=(nt,),
        in_specs=[row(D_MODEL), _full_spec(gn), _full_spec(w_ext), _full_spec(w_t), _full_spec(gq), _full_spec(gkv),
                  _full_spec(wuq_ext), _full_spec(wuk_ext), tab, tab],
        out_specs=out_specs,
        out_shape=out_shapes,
        compiler_params=_cparams(("parallel",)),
        name="even_proj",
    )(x, gn, w_ext, w_t, gq, gkv, wuq_ext, wuk_ext, cos, sin)


LOG2E = float(np.log2(np.e))


def _mla_attn_kernel(q_ref, k_ref, kn_ref, o_ref, m_sc, l_sc, acc_sc, sa_sc, sb_sc, *, tq, tk):
    qi = pl.program_id(1)
    ki = pl.program_id(2)
    n_needed = ((qi + 1) * tq + tk - 1) // tk
    m_rows = MLA_HEADS * tq
    c_exp = MLA_SCALE * LOG2E

    def scores(kref):
        return _dot_nt(q_ref[...].reshape(m_rows, q_ref.shape[-1]), kref[...])

    @pl.when(ki == 0)
    def _():
        m_sc[...] = jnp.full(m_sc.shape, NEG_BIG, F32)
        l_sc[...] = jnp.zeros(l_sc.shape, F32)
        acc_sc[...] = jnp.zeros(acc_sc.shape, F32)
        sa_sc[...] = scores(k_ref)

    def step(masked, cur_sc, nxt_sc):
        nxt_sc[...] = scores(kn_ref)
        s = cur_sc[...]
        if masked:
            qpos = qi * tq + jnp.bitwise_and(lax.broadcasted_iota(jnp.int32, (m_rows, 1), 0), tq - 1)
            s = jnp.where((ki * tk + lax.broadcasted_iota(jnp.int32, (1, tk), 1)) <= qpos, s, NEG_BIG)
        m_prev = m_sc[...]
        m_new = jnp.maximum(m_prev, _rep(_row_max(s)))
        alpha = jnp.exp2((m_prev - m_new) * c_exp)
        p = jnp.exp2((s - _tile_lanes(m_new, tk)) * c_exp)
        l_sc[...] = alpha * l_sc[...] + _fold_lanes(p, jnp.add)
        acc_sc[...] = _tile_lanes(alpha, MLA_KV_RANK) * acc_sc[...] + _dot(p.astype(BF16), k_ref[:, :MLA_KV_RANK])
        m_sc[...] = m_new

    crosses = (ki + 1) * tk - 1 > qi * tq
    even = lax.rem(ki, 2) == 0
    for par, (cur_sc, nxt_sc) in enumerate(((sa_sc, sb_sc), (sb_sc, sa_sc))):
        for masked in (True, False):
            cond = jnp.logical_and(ki < n_needed, jnp.logical_and(even == (par == 0), crosses == masked))
            pl.when(cond)(functools.partial(step, masked, cur_sc, nxt_sc))

    @pl.when(ki == n_needed - 1)
    def _():
        out = acc_sc[...] / jnp.maximum(jnp.sum(l_sc[...], axis=1, keepdims=True), 1e-30)
        o_ref[...] = out.reshape(MLA_HEADS, tq, MLA_KV_RANK).astype(o_ref.dtype)


def _mla_prompt_attention(q_ext, k_ext, batch, seq, tq, tk):
    nq, nk = seq // tq, seq // tk
    kw = q_ext.shape[-1]

    def k_map(off):
        def imap(b, qi, ki):
            last = ((qi + 1) * tq + tk - 1) // tk - 1
            return (b * nk + jnp.minimum(ki + off, last), 0)
        return imap

    return pl.pallas_call(
        functools.partial(_mla_attn_kernel, tq=tq, tk=tk),
        grid=(batch, nq, nk),
        in_specs=[pl.BlockSpec((MLA_HEADS, tq, kw), lambda b, qi, ki: (0, b * nq + qi, 0)),
                  pl.BlockSpec((tk, kw), k_map(0)), pl.BlockSpec((tk, kw), k_map(1))],
        out_specs=pl.BlockSpec((MLA_HEADS, tq, MLA_KV_RANK), lambda b, qi, ki: (0, b * nq + qi, 0)),
        out_shape=jax.ShapeDtypeStruct((MLA_HEADS, batch * seq, MLA_KV_RANK), BF16),
        scratch_shapes=[pltpu.VMEM((MLA_HEADS * tq, LANES), F32), pltpu.VMEM((MLA_HEADS * tq, LANES), F32),
                        pltpu.VMEM((MLA_HEADS * tq, MLA_KV_RANK), F32),
                        pltpu.VMEM((MLA_HEADS * tq, tk), F32), pltpu.VMEM((MLA_HEADS * tq, tk), F32)],
        compiler_params=_cparams(("parallel", "parallel", "arbitrary")),
        name="mla_prompt_attn",
    )(q_ext, k_ext, k_ext)


SEG_W = CMP_STRIDE * NSA_KV_W
KVP_W = 4 * LANES


def _prep_cmp_weights(w_cmp, pe_cmp):
    wh = w_cmp.reshape(2, 2, CMP_STRIDE, NSA_DH, NSA_DH)
    t = jnp.transpose(jnp.pad(wh, ((0, 0),) * 4 + ((0, LANES - NSA_DH),)), (1, 2, 0, 3, 4)).astype(BF16)
    eye_c = np.eye(2, dtype=np.float32).reshape(1, 1, 1, 2, 1, 2, 1, 1)
    eye_g = np.eye(NSA_KV_HEADS, dtype=np.float32).reshape(1, 1, NSA_KV_HEADS, 1, 1, 1, NSA_KV_HEADS, 1)
    mask = jnp.asarray(eye_c * eye_g, BF16)
    big = t[:, :, None, :, :, None, None, :] * mask
    w_big = big.reshape(2, SEG_W, KVP_W)
    peh = pe_cmp.reshape(2, 2, CMP_STRIDE, NSA_DH)
    pe_big = jnp.broadcast_to(jnp.transpose(peh, (1, 2, 0, 3))[:, :, None], (2, CMP_STRIDE, NSA_KV_HEADS, 2, NSA_DH))
    return w_big, pe_big.reshape(2, 1, SEG_W)


def _slc_overlap_matrix(n_cmp_rows, n_lanes):
    ratio = SLC_BLOCK // CMP_STRIDE
    lead = CMP_BLOCK // CMP_STRIDE - 1
    a = np.zeros((n_cmp_rows, n_lanes), np.float32)
    for j in range(n_lanes):
        for o in range(-lead, ratio):
            n = ratio * j + o
            if 0 <= n < n_cmp_rows:
                a[n, j] = (min(CMP_STRIDE * o + CMP_BLOCK, SLC_BLOCK) - max(CMP_STRIDE * o, 0)) / CMP_BLOCK
    return a


def _key_position_features(n_keys, n_lanes):
    kc = np.zeros((n_keys, LANES + n_lanes), np.float32)
    pos = np.arange(n_keys)
    kc[:, NSA_DH] = (pos // SLC_BLOCK) * SLC_BLOCK
    kc[:, NSA_DH + 1] = pos % SLC_BLOCK
    kc[pos, LANES + pos // SLC_BLOCK] = 1.0
    return kc


def _slopes_col(g, rows, tq):
    r = lax.shift_right_logical(lax.broadcasted_iota(jnp.int32, (rows, 1), 0), int(np.log2(tq)))
    col = jnp.zeros((rows, 1), F32)
    for rr in range(NSA_GROUP):
        col = jnp.where(r == rr, 2.0 ** (-(g * NSA_GROUP + rr + 1.0)), col)
    return col


def _softmax_rows(s, mask):
    n = s.shape[1]
    s = jnp.where(mask, s, NEG_BIG)
    m = _rep(_row_max(s))
    e = jnp.where(mask, jnp.exp(s - _tile_lanes(m, n)), 0.0)
    inv = 1.0 / jnp.maximum(_rep(_row_sum(e)), 1e-30)
    return e * _tile_lanes(inv, n)


def _top_blocks(score, n_pick):
    nb = score.shape[0]
    blk = lax.broadcasted_iota(jnp.int32, score.shape, 0).astype(F32)
    sel = jnp.zeros(score.shape, F32)
    for _ in range(n_pick):
        m = jnp.max(score, axis=0, keepdims=True)
        idx = jnp.min(jnp.where(score == m, blk, float(nb)), axis=0, keepdims=True)
        hit = blk == idx
        sel = jnp.where(jnp.logical_and(hit, m > 0.5 * NEG_BIG), 1.0, sel)
        score = jnp.where(hit, 2.0 * NEG_BIG, score)
    return sel


def _split3(x):
    hi = x.astype(BF16)
    r1 = x - hi.astype(F32)
    mid = r1.astype(BF16)
    lo = (r1 - mid.astype(F32)).astype(BF16)
    return hi, mid, lo


def _cmp_prompt_kernel(x_ref, pe_ref, w_ref, kvc_ref):
    x = x_ref[...]
    h1 = _dot((x + pe_ref[0]).astype(BF16), w_ref[0])
    h2 = _dot((x + pe_ref[1]).astype(BF16), w_ref[1])
    n = x.shape[0]
    kvc_ref[...] = (h1 + pltpu.roll(h2, n - 1, 0)).astype(kvc_ref.dtype)


def _cmp_prompt(kv_cmp, w_big, pe_big, batch, seq):
    nseg = seq // CMP_STRIDE
    x = kv_cmp.reshape(batch * nseg, SEG_W)
    return pl.pallas_call(
        _cmp_prompt_kernel,
        grid=(batch,),
        in_specs=[pl.BlockSpec((nseg, SEG_W), lambda b: (b, 0)), _full_spec(pe_big), _full_spec(w_big)],
        out_specs=pl.BlockSpec((nseg, KVP_W), lambda b: (b, 0)),
        out_shape=jax.ShapeDtypeStruct((batch * nseg, KVP_W), BF16),
        compiler_params=_cparams(("parallel",)),
        name="nsa_cmp_prompt",
    )(x, pe_big, w_big)


N_WIN_BLOCKS = WINDOW // 128 + 1


def _nsa_prompt_kernel(qn_ref, gl_ref, kvc_ref, amat_ref, slcp_ref, kc_ref, *rest, tq, tk, nseg):
    win_refs = rest[:N_WIN_BLOCKS]
    o_ref = rest[N_WIN_BLOCKS]
    stat = rest[N_WIN_BLOCKS + 1:N_WIN_BLOCKS + 1 + 3 * NSA_KV_HEADS]
    flag_ref = rest[-1]
    nt_all = slcp_ref.shape[0] // tk
    qi = pl.program_id(1)
    s0 = qi * tq
    rows = NSA_GROUP * tq
    nl = amat_ref.shape[1]
    qpos = s0 + jnp.bitwise_and(lax.broadcasted_iota(jnp.int32, (rows, 1), 0), tq - 1)
    qpos_t = s0 + lax.broadcasted_iota(jnp.int32, (tq, 1), 0)
    gates = jax.nn.sigmoid(gl_ref[...])
    n_tiles = (s0 + tq + tk - 1) // tk
    lane = lax.broadcasted_iota(jnp.int32, (1, LANES), 1)
    is_pos_lane = jnp.logical_or(lane == NSA_DH, lane == NSA_DH + 1)
    blk = lax.broadcasted_iota(jnp.int32, (1, nl), 1)
    cur = lax.shift_right_logical(qpos_t, int(np.log2(SLC_BLOCK)))
    valid = blk * SLC_BLOCK <= qpos_t
    forced = jnp.logical_or(blk == 0, jnp.logical_or(blk == cur, blk == cur - 1))
    cpos = lax.broadcasted_iota(jnp.int32, (1, nseg), 1) * CMP_STRIDE + (CMP_BLOCK - 1)
    dist_c = (qpos - cpos).astype(F32)
    amat = amat_ref[...]

    slopes, qgs, o_cs, q_augs = [], [], [], []
    any_sel = jnp.zeros((1, nl), F32)
    for g in range(NSA_KV_HEADS):
        slope = _slopes_col(g, rows, tq)
        q_heads = [qn_ref[:, (g * NSA_GROUP + r) * LANES:(g * NSA_GROUP + r + 1) * LANES] for r in range(NSA_GROUP)]
        qg = jnp.concatenate(q_heads, axis=0)
        kc = kvc_ref[:, g * LANES:(g + 1) * LANES]
        vc = kvc_ref[:, (2 + g) * LANES:(3 + g) * LANES]
        p_c = _softmax_rows(_dot_nt(qg, kc) - slope * dist_c, dist_c >= 0)
        o_cs.append(_dot(p_c.astype(BF16), vc))
        p_grp = p_c[0:tq]
        for r in range(1, NSA_GROUP):
            p_grp = p_grp + p_c[r * tq:(r + 1) * tq]
        s_slc = sum(_dot(t, amat) for t in _split3(p_grp))
        score = jnp.where(forced, FORCE_SCORE, jnp.where(valid, s_slc, NEG_BIG))
        sel = _top_blocks(score.T, N_SELECT).T
        selneg = ((sel - 1.0) * MASK_BIG).astype(BF16)
        any_sel = jnp.maximum(any_sel, jnp.max(sel, axis=0, keepdims=True))
        q_augs.append(jnp.concatenate(
            [jnp.concatenate([jnp.where(is_pos_lane, (2.0 ** (-(g * NSA_GROUP + r + 1.0))), q_heads[r].astype(F32)).astype(BF16),
                              selneg], axis=1) for r in range(NSA_GROUP)], axis=0))
        slopes.append(slope)
        qgs.append(qg)
        m_sc, l_sc, acc_sc = stat[3 * g:3 * g + 3]
        m_sc[...] = jnp.full(m_sc.shape, NEG_BIG, F32)
        l_sc[...] = jnp.zeros(l_sc.shape, F32)
        acc_sc[...] = jnp.zeros(acc_sc.shape, F32)
    tile_of_blk = lax.shift_right_logical(blk, int(np.log2(tk // SLC_BLOCK)))
    for t in range(nt_all):
        flag_ref[t] = jnp.max(jnp.where(tile_of_blk == t, any_sel, 0.0)).astype(jnp.int32)

    def slc_tile(kt, causal):
        k0 = pl.multiple_of(kt * tk, tk)
        if causal:
            ok = (k0 + lax.broadcasted_iota(jnp.int32, (1, tk), 1)) <= qpos
        for g in range(NSA_KV_HEADS):
            m_sc, l_sc, acc_sc = stat[3 * g:3 * g + 3]
            kk = slcp_ref[pl.ds(k0, tk), g * LANES:(g + 1) * LANES] + kc_ref[pl.ds(k0, tk), 0:LANES]
            k_aug = jnp.concatenate([kk, kc_ref[pl.ds(k0, tk), LANES:]], axis=1)
            vv = slcp_ref[pl.ds(k0, tk), (2 + g) * LANES:(3 + g) * LANES]
            s = _dot_nt(q_augs[g], k_aug)
            if causal:
                s = jnp.where(ok, s, NEG_BIG)
            m_prev = m_sc[...]
            m_new = jnp.maximum(m_prev, _rep(_row_max(s)))
            alpha = jnp.exp(m_prev - m_new)
            p = jnp.exp(s - _tile_lanes(m_new, tk))
            if causal:
                p = jnp.where(ok, p, 0.0)
            l_sc[...] = alpha * l_sc[...] + _fold_lanes(p, jnp.add)
            acc_sc[...] = alpha * acc_sc[...] + _dot(p.astype(BF16), vv)
            m_sc[...] = m_new

    def body(kt, carry):
        @pl.when(flag_ref[kt] > 0)
        def _():
            slc_tile(kt, False)
        return carry

    lax.fori_loop(0, n_tiles - 1, body, 0)
    slc_tile(n_tiles - 1, True)
    nw = N_WIN_BLOCKS * 128
    wpos = s0 - WINDOW + lax.broadcasted_iota(jnp.int32, (1, nw), 1)
    dist_w = (qpos - wpos).astype(F32)
    mask_w = jnp.logical_and(jnp.logical_and(dist_w >= 0, dist_w <= WINDOW), wpos >= 0)
    for g in range(NSA_KV_HEADS):
        m_sc, l_sc, acc_sc = stat[3 * g:3 * g + 3]
        o_s = acc_sc[...] / jnp.maximum(jnp.sum(l_sc[...], axis=1, keepdims=True), 1e-30)
        kw = jnp.concatenate([w[:, g * LANES:(g + 1) * LANES] for w in win_refs], axis=0)
        vw = jnp.concatenate([w[:, (2 + g) * LANES:(3 + g) * LANES] for w in win_refs], axis=0)
        p_w = _softmax_rows(_dot_nt(qgs[g], kw) - slopes[g] * dist_w, mask_w)
        o_w = _dot(p_w.astype(BF16), vw)
        for r in range(NSA_GROUP):
            hd = g * NSA_GROUP + r
            sl = slice(r * tq, (r + 1) * tq)
            o_ref[:, hd * LANES:(hd + 1) * LANES] = (
                gates[:, 3 * hd:3 * hd + 1] * o_cs[g][sl] + gates[:, 3 * hd + 1:3 * hd + 2] * o_s[sl]
                + gates[:, 3 * hd + 2:3 * hd + 3] * o_w[sl])


def _nsa_prompt(qn, gl, kvc, slcp, winp, batch, seq, tq, tk):
    assert tq == 128 and seq % tk == 0 and tk % tq == 0
    nseg = seq // CMP_STRIDE
    n_slc = seq // SLC_BLOCK
    nl = LANES * ((n_slc + LANES - 1) // LANES)
    assert nl == LANES, "selection blocks must fit one lane group"
    nq = seq // tq
    amat = jnp.asarray(_slc_overlap_matrix(nseg, nl), BF16)
    kc = jnp.asarray(_key_position_features(seq, nl), BF16)

    def win_spec(j):
        return pl.BlockSpec((128, KVP_W), lambda b, qi: (b * nq + jnp.maximum(qi - (N_WIN_BLOCKS - 1) + j, 0), 0))

    rows = NSA_GROUP * tq
    return pl.pallas_call(
        functools.partial(_nsa_prompt_kernel, tq=tq, tk=tk, nseg=nseg),
        grid=(batch, nq),
        in_specs=[pl.BlockSpec((tq, NSA_HEADS * LANES), lambda b, qi: (b * nq + qi, 0)),
                  pl.BlockSpec((tq, LANES), lambda b, qi: (b * nq + qi, 0)),
                  pl.BlockSpec((nseg, KVP_W), lambda b, qi: (b, 0)),
                  _full_spec(amat),
                  pl.BlockSpec((seq, KVP_W), lambda b, qi: (b, 0)),
                  _full_spec(kc)] + [win_spec(j) for j in range(N_WIN_BLOCKS)],
        out_specs=pl.BlockSpec((tq, NSA_HEADS * LANES), lambda b, qi: (b * nq + qi, 0)),
        out_shape=jax.ShapeDtypeStruct((batch * seq, NSA_HEADS * LANES), F32),
        scratch_shapes=[pltpu.VMEM((rows, LANES), F32), pltpu.VMEM((rows, LANES), F32), pltpu.VMEM((rows, LANES), F32)]
        * NSA_KV_HEADS + [pltpu.SMEM((seq // tk,), jnp.int32)],
        compiler_params=_cparams(("parallel", "arbitrary")),
        name="nsa_prompt",
    )(qn, gl, kvc, amat, slcp, kc, *([winp] * N_WIN_BLOCKS))


def _silu(x):
    return x * jax.nn.sigmoid(x)


def _even_out_kernel(x_ref, lat_ref, gmla_ref, onsa_ref, gnsa_ref, wuv_ref, wo_ref, o_ref):
    o_mla = jnp.concatenate([_dot(lat_ref[hd], wuv_ref[hd]) for hd in range(MLA_HEADS)], axis=1)
    a = (o_mla * _silu(gmla_ref[...])).astype(BF16)
    b = (onsa_ref[...] * _silu(gnsa_ref[...])).astype(BF16)
    nm = MLA_HEADS * LANES
    o_ref[...] = x_ref[...] + _dot(a, wo_ref[0:nm]) + _dot(b, wo_ref[nm:])


def _even_output(x, lat, gmla, onsa, gnsa, wuv_ext, wo_ext, tm):
    t = x.shape[0]
    row = lambda w: pl.BlockSpec((tm, w), lambda i: (i, 0))
    return pl.pallas_call(
        _even_out_kernel,
        grid=(t // tm,),
        in_specs=[row(D_MODEL), pl.BlockSpec((MLA_HEADS, tm, MLA_KV_RANK), lambda i: (0, i, 0)),
                  row(MLA_HEADS * LANES), row(NSA_HEADS * LANES), row(NSA_HEADS * LANES),
                  _full_spec(wuv_ext), _full_spec(wo_ext)],
        out_specs=row(D_MODEL),
        out_shape=jax.ShapeDtypeStruct((t, D_MODEL), F32),
        compiler_params=_cparams(("parallel",)),
        name="even_out",
    )(x, lat, gmla, onsa, gnsa, wuv_ext, wo_ext)


_RET_QK = RET_HEADS * RET_DK
_RET_LOG_G = [float(np.log1p(-(2.0 ** (-5.0 - h)))) for h in range(RET_HEADS)]


def _rope_tables_ret(pos):
    half = RET_DK // 2
    inv = ROPE_BASE ** (-jnp.arange(half, dtype=F32) / half)
    ang = pos.astype(F32)[:, None] * inv
    cos, sin = jnp.cos(ang), jnp.sin(ang)
    return jnp.concatenate([cos, cos], 1), jnp.concatenate([-sin, sin], 1)


def _odd_proj_kernel(x_ref, gn_ref, w_ref, c_ref, s_ref, q_ref, k_ref, v_ref, g_ref):
    h = _rms(x_ref[...], gn_ref[...]).astype(BF16)
    c = c_ref[...]
    s = s_ref[...]
    half = RET_DK // 2

    def rot(z):
        return z * c + pltpu.roll(z, half, 1) * s

    for hd in range(RET_HEADS):
        sl = slice(hd * RET_DK, (hd + 1) * RET_DK)
        q_ref[:, sl] = rot(_dot(h, w_ref[:, sl])).astype(BF16)
        ks = slice(_RET_QK + hd * RET_DK, _RET_QK + (hd + 1) * RET_DK)
        k_ref[:, sl] = (rot(_dot(h, w_ref[:, ks])) * (RET_DK ** -0.5)).astype(BF16)
    v_ref[...] = _dot(h, w_ref[:, 2 * _RET_QK:2 * _RET_QK + RET_WIDTH])
    g_ref[...] = _dot(h, w_ref[:, 2 * _RET_QK + RET_WIDTH:])


def _odd_project(x, gn, w_bf, cos, sin, tm):
    t = x.shape[0]
    ntab = cos.shape[0] // tm
    row = lambda w: pl.BlockSpec((tm, w), lambda i: (i, 0))
    tab = pl.BlockSpec((tm, RET_DK), lambda i: (i % ntab, 0))
    return pl.pallas_call(
        _odd_proj_kernel,
        grid=(t // tm,),
        in_specs=[row(D_MODEL), _full_spec(gn), _full_spec(w_bf), tab, tab],
        out_specs=(row(_RET_QK), row(_RET_QK), row(RET_WIDTH), row(RET_WIDTH)),
        out_shape=(jax.ShapeDtypeStruct((t, _RET_QK), BF16), jax.ShapeDtypeStruct((t, _RET_QK), BF16),
                   jax.ShapeDtypeStruct((t, RET_WIDTH), F32), jax.ShapeDtypeStruct((t, RET_WIDTH), F32)),
        compiler_params=_cparams(("parallel",)),
        name="odd_proj",
    )(x, gn, w_bf, cos, sin)


def _group_norm_gate(o, gn_row, g):
    mu = jnp.mean(o, axis=-1, keepdims=True)
    var = jnp.mean(jnp.square(o - mu), axis=-1, keepdims=True)
    return _silu(g) * ((o - mu) * lax.rsqrt(var + EPS) * gn_row)


def _ret_prompt_kernel(q_ref, k_ref, v_ref, g_ref, gn_ref, y_ref, sfin_ref, s_sc, *, chunk):
    ci = pl.program_id(1)

    @pl.when(ci == 0)
    def _():
        s_sc[...] = jnp.zeros(s_sc.shape, F32)

    n_col = lax.broadcasted_iota(jnp.int32, (chunk, 1), 0).astype(F32)
    diff = n_col - lax.broadcasted_iota(jnp.int32, (1, chunk), 1).astype(F32)
    for hd in range(RET_HEADS):
        lg = _RET_LOG_G[hd]
        dmat = jnp.where(diff >= 0, jnp.exp(lg * jnp.maximum(diff, 0.0)), 0.0)
        xi = jnp.exp(lg * (n_col + 1.0))
        zeta = jnp.exp(lg * (chunk - 1.0 - n_col))
        qh = q_ref[:, hd * RET_DK:(hd + 1) * RET_DK]
        kh = k_ref[:, hd * RET_DK:(hd + 1) * RET_DK]
        vh = v_ref[:, hd * RET_DV:(hd + 1) * RET_DV]
        s_prev = s_sc[hd]
        inner = _dot_nt(qh, kh) * dmat
        o = _dot(inner.astype(BF16), vh.astype(BF16)) + _dot(qh, s_prev.astype(BF16)) * xi
        kv = lax.dot_general(kh, (vh * zeta).astype(BF16), (((0,), (0,)), ((), ())), preferred_element_type=F32)
        s_sc[hd] = float(np.exp(lg * chunk)) * s_prev + kv
        y_ref[:, hd * RET_DV:(hd + 1) * RET_DV] = _group_norm_gate(
            o, gn_ref[hd:hd + 1, :], g_ref[:, hd * RET_DV:(hd + 1) * RET_DV]).astype(y_ref.dtype)

    @pl.when(ci == pl.num_programs(1) - 1)
    def _():
        sfin_ref[0] = s_sc[...]


def _retention_prompt(q, k, v, g, gn, batch, seq):
    chunk = RET_CHUNK
    nc = seq // chunk
    row = lambda w: pl.BlockSpec((chunk, w), lambda b, c: (b * nc + c, 0))
    return pl.pallas_call(
        functools.partial(_ret_prompt_kernel, chunk=chunk),
        grid=(batch, nc),
        in_specs=[row(_RET_QK), row(_RET_QK), row(RET_WIDTH), row(RET_WIDTH), _full_spec(gn)],
        out_specs=(row(RET_WIDTH),
                   pl.BlockSpec((1, RET_HEADS, RET_DK, RET_DV), lambda b, c: (b, 0, 0, 0))),
        out_shape=(jax.ShapeDtypeStruct((batch * seq, RET_WIDTH), BF16),
                   jax.ShapeDtypeStruct((batch, RET_HEADS, RET_DK, RET_DV), F32)),
        scratch_shapes=[pltpu.VMEM((RET_HEADS, RET_DK, RET_DV), F32)],
        compiler_params=_cparams(("parallel", "arbitrary")),
        name="retention_prompt",
    )(q, k, v, g, gn)


def _odd_out_kernel(x_ref, y_ref, w_ref, fn_ref, o_ref):
    x2 = x_ref[...] + _dot(y_ref[...], w_ref[...])
    o_ref[...] = _rms(x2, fn_ref[...])


def _odd_output(x, y, w_bf, fn, tm):
    t = x.shape[0]
    row = lambda w: pl.BlockSpec((tm, w), lambda i: (i, 0))
    return pl.pallas_call(
        _odd_out_kernel,
        grid=(t // tm,),
        in_specs=[row(D_MODEL), row(RET_WIDTH), _full_spec(w_bf), _full_spec(fn)],
        out_specs=row(D_MODEL),
        out_shape=jax.ShapeDtypeStruct((t, D_MODEL), F32),
        compiler_params=_cparams(("parallel",)),
        name="odd_out",
    )(x, y, w_bf, fn)


def _ret_decode_kernel(q_ref, k_ref, v_ref, g_ref, gn_ref, s_ref, y_ref, snew_ref):
    eye = (lax.broadcasted_iota(jnp.int32, (RET_DK, RET_DK), 0)
           == lax.broadcasted_iota(jnp.int32, (RET_DK, RET_DK), 1))
    for hd in range(RET_HEADS):
        gam = float(np.exp(_RET_LOG_G[hd]))
        qh = q_ref[0, :, hd * RET_DK:(hd + 1) * RET_DK]
        kh = k_ref[0, :, hd * RET_DK:(hd + 1) * RET_DK]
        vh = v_ref[0, :, hd * RET_DV:(hd + 1) * RET_DV].astype(BF16).astype(F32)
        s_prev = s_ref[0, hd]
        inner = jnp.sum(qh.astype(F32) * kh.astype(F32), axis=1, keepdims=True).astype(BF16).astype(F32)
        qs = _dot(jnp.broadcast_to(qh, (8, RET_DK)), s_prev.astype(BF16))[0:1]
        o = inner * vh + qs * gam
        k_col = jnp.sum(jnp.where(eye, jnp.broadcast_to(kh.astype(F32), (RET_DK, RET_DK)), 0.0), axis=1, keepdims=True)
        snew_ref[0, hd] = gam * s_prev + k_col * vh
        y_ref[0, :, hd * RET_DV:(hd + 1) * RET_DV] = _group_norm_gate(
            o, gn_ref[hd:hd + 1, :], g_ref[0, :, hd * RET_DV:(hd + 1) * RET_DV]).astype(y_ref.dtype)


def _retention_decode(q, k, v, g, gn, state):
    db = q.shape[0]
    r3 = lambda a: a.reshape(db, 1, a.shape[-1])
    vec = lambda w: pl.BlockSpec((1, 1, w), lambda b: (b, 0, 0))
    st = pl.BlockSpec((1, RET_HEADS, RET_DK, RET_DV), lambda b: (b, 0, 0, 0))
    y, s_new = pl.pallas_call(
        _ret_decode_kernel,
        grid=(db,),
        in_specs=[vec(_RET_QK), vec(_RET_QK), vec(RET_WIDTH), vec(RET_WIDTH), _full_spec(gn), st],
        out_specs=(vec(RET_WIDTH), st),
        out_shape=(jax.ShapeDtypeStruct((db, 1, RET_WIDTH), BF16),
                   jax.ShapeDtypeStruct(state.shape, state.dtype)),
        compiler_params=_cparams(("parallel",)),
        name="retention_decode",
    )(r3(q), r3(k), r3(v), r3(g), gn, state)
    return y.reshape(db, RET_WIDTH), s_new


PAGES_PER_STEP = 64
SUB = 8


PAGE_SLOTS = 3


def _page_group_dmas(pt_ref, pool_ref, buf_ref, sem_ref, grp, lookup):
    slot = lax.rem(grp, PAGE_SLOTS)
    copies = []
    for j in range(PAGES_PER_STEP):
        page = pt_ref[grp * PAGES_PER_STEP + j] if lookup else 0
        copies.append(pltpu.make_async_copy(pool_ref.at[page], buf_ref.at[slot, j], sem_ref.at[slot]))
    return copies


def _page_pipeline_step(pt_ref, pool_ref, buf_ref, sem_ref, steps):
    b = pl.program_id(0)
    i = pl.program_id(1)
    grp = b * steps + i
    n_grp = pl.num_programs(0) * steps

    @pl.when(jnp.logical_and(b == 0, i == 0))
    def _():
        for g0 in range(min(2, PAGE_SLOTS - 1)):
            @pl.when(g0 < n_grp)
            def _(g0=g0):
                for c in _page_group_dmas(pt_ref, pool_ref, buf_ref, sem_ref, g0, True):
                    c.start()

    @pl.when(i < steps)
    def _():
        for c in _page_group_dmas(pt_ref, pool_ref, buf_ref, sem_ref, grp, False):
            c.wait()

        @pl.when(grp + 2 < n_grp)
        def _():
            for c in _page_group_dmas(pt_ref, pool_ref, buf_ref, sem_ref, grp + 2, True):
                c.start()

    return lax.rem(grp, PAGE_SLOTS)


def _mla_decode_kernel(pt_ref, q_ref, new_ref, *rest, steps):
    pool_ref, o_ref, m_sc, l_sc, acc_sc, r0_sc, r1_sc, s0_sc, s1_sc, page_sc, page_sem = rest
    bufs = ((r0_sc, s0_sc), (r1_sc, s1_sc))
    i = pl.program_id(1)
    slot = _page_pipeline_step(pt_ref, pool_ref, page_sc, page_sem, steps)
    q = q_ref[0]

    def load(par):
        r_sc, s_sc = bufs[par]
        rows_t = jnp.concatenate([page_sc[slot, j] for j in range(PAGES_PER_STEP)], axis=1).astype(BF16)
        r_sc[...] = rows_t
        s_sc[...] = _dot(q, rows_t) * MLA_SCALE

    def proc(par):
        r_sc, s_sc = bufs[par]
        s = s_sc[...]
        m_prev = m_sc[...]
        m_new = jnp.maximum(m_prev, _row_max(s))
        alpha = jnp.exp(m_prev - m_new)
        p = jnp.exp(s - m_new)
        l_sc[...] = alpha * l_sc[...] + _row_sum(p)
        acc_sc[...] = alpha * acc_sc[...] + _dot_nt(p.astype(BF16), r_sc[0:MLA_KV_RANK, :])
        m_sc[...] = m_new

    @pl.when(i == 0)
    def _():
        m_sc[...] = jnp.full(m_sc.shape, NEG_BIG, F32)
        l_sc[...] = jnp.zeros(l_sc.shape, F32)
        acc_sc[...] = jnp.zeros(acc_sc.shape, F32)
        load(0)

    even = lax.rem(i, 2) == 0
    steady = jnp.logical_and(i > 0, i < steps)
    for par in range(2):
        @pl.when(jnp.logical_and(steady, even == (par == 0)))
        def _(par=par):
            load(par)
            proc(1 - par)

    @pl.when(i == steps)
    def _():
        proc((steps - 1) % 2)
        new = new_ref[0].astype(BF16).astype(F32)
        s_n = jnp.sum(q.astype(F32) * new, axis=1, keepdims=True) * MLA_SCALE
        m_prev = m_sc[...]
        m_new = jnp.maximum(m_prev, s_n)
        alpha = jnp.exp(m_prev - m_new)
        p_n = jnp.exp(s_n - m_new)
        l = alpha * l_sc[...] + p_n
        acc = alpha * acc_sc[...] + p_n.astype(BF16).astype(F32) * new[:, :MLA_KV_RANK]
        o_ref[0] = (acc / jnp.maximum(l, 1e-30)).astype(o_ref.dtype)


def _mla_decode(q_dec, rows_new, pool, page_table):
    db, n_pages = page_table.shape
    assert n_pages % PAGES_PER_STEP == 0
    steps = n_pages // PAGES_PER_STEP
    w = pool.shape[1]
    n_tok = PAGES_PER_STEP * PAGE_SIZE

    grid_spec = pltpu.PrefetchScalarGridSpec(
        num_scalar_prefetch=1,
        grid=(db, steps + 1),
        in_specs=[pl.BlockSpec((1, MLA_HEADS, w), lambda b, i, pt: (b, 0, 0)),
                  pl.BlockSpec((1, 1, w), lambda b, i, pt: (b, 0, 0)),
                  pl.BlockSpec(memory_space=pl.ANY)],
        out_specs=pl.BlockSpec((1, MLA_HEADS, MLA_KV_RANK), lambda b, i, pt: (b, 0, 0)),
        scratch_shapes=[pltpu.VMEM((MLA_HEADS, 1), F32), pltpu.VMEM((MLA_HEADS, 1), F32),
                        pltpu.VMEM((MLA_HEADS, MLA_KV_RANK), F32),
                        pltpu.VMEM((w, n_tok), BF16), pltpu.VMEM((w, n_tok), BF16),
                        pltpu.VMEM((MLA_HEADS, n_tok), F32), pltpu.VMEM((MLA_HEADS, n_tok), F32),
                        pltpu.VMEM((PAGE_SLOTS, PAGES_PER_STEP, w, PAGE_SIZE), F32),
                        pltpu.SemaphoreType.DMA((PAGE_SLOTS,))],
    )
    return pl.pallas_call(
        functools.partial(_mla_decode_kernel, steps=steps),
        grid_spec=grid_spec,
        out_shape=jax.ShapeDtypeStruct((db, MLA_HEADS, MLA_KV_RANK), BF16),
        compiler_params=_cparams(("arbitrary", "arbitrary")),
        name="mla_decode",
    )(page_table.reshape(-1), q_dec, rows_new.reshape(db, 1, w), pool)


def _decode_blocks(p_len, n_new):
    n_slc = -(-(p_len + n_new) // SLC_BLOCK)
    return n_slc, LANES * ((n_slc + LANES - 1) // LANES)


CMP_PAIRS = CMP_STRIDE // 2
GRP_W = 2 * NSA_DH


def _prep_cmp_pair_weights(w_cmp, pe_cmp):
    wh = w_cmp.reshape(2, 2, CMP_PAIRS, 2, NSA_DH, NSA_DH)
    t = jnp.transpose(wh, (2, 3, 0, 4, 1, 5)).astype(BF16)
    eye_c = jnp.asarray(np.eye(2, dtype=np.float32).reshape(1, 1, 2, 1, 1, 2, 1), BF16)
    big = t[:, :, :, :, :, None, :] * eye_c
    w_pair = big.reshape(CMP_PAIRS, 2 * GRP_W, 2 * GRP_W)
    peh = pe_cmp.reshape(2, 2, CMP_PAIRS, 2, NSA_DH)
    pe_pair = jnp.transpose(peh, (1, 2, 3, 0, 4)).reshape(2, CMP_PAIRS, 1, 2 * GRP_W)
    return w_pair, pe_pair


def _cmp_bias_kernel(pe_ref, w_ref, b_ref):
    for half in range(2):
        acc = jnp.zeros((SUB, GRP_W), F32)
        for jp in range(CMP_PAIRS):
            w = w_ref[jp][:, half * GRP_W:(half + 1) * GRP_W]
            for t in _split3(jnp.broadcast_to(pe_ref[half, jp], (SUB, 2 * GRP_W))):
                acc = acc + _dot(t, w)
        b_ref[half] = acc


def _cmp_bias(w_pair, pe_pair):
    return pl.pallas_call(
        _cmp_bias_kernel,
        grid=(1,),
        in_specs=[_full_spec(pe_pair), _full_spec(w_pair)],
        out_specs=pl.BlockSpec((2, SUB, GRP_W), lambda i: (0, 0, 0)),
        out_shape=jax.ShapeDtypeStruct((2, SUB, GRP_W), F32),
        compiler_params=_cparams(("arbitrary",)),
        name="nsa_cmp_bias",
    )(pe_pair, w_pair)


def _cmp_decode_kernel(pt_ref, qn_ref, cnew_ref, bias_ref, w_ref, amat_ref, *rest, p_len, steps):
    pool_ref, oc_ref, sslc_ref, xa_sc, xb_sc, h_sc, page_sc, page_sem = rest
    x_bufs = (xa_sc, xb_sc)
    i = pl.program_id(1)
    slot = _page_pipeline_step(pt_ref, pool_ref, page_sc, page_sem, steps)
    n_tok = PAGES_PER_STEP * PAGE_SIZE
    n_step = n_tok // CMP_STRIDE
    nseg = h_sc.shape[1]

    def load(par):
        x_sc = x_bufs[par]
        for j in range(PAGES_PER_STEP):
            for g in range(NSA_KV_HEADS):
                x_sc[g, j * PAGE_SIZE:(j + 1) * PAGE_SIZE, :] = page_sc[slot, j, g * GRP_W:(g + 1) * GRP_W, :].T

    def compress(par, step):
        x_sc = x_bufs[par]
        r0 = pl.multiple_of(step * n_step, n_step)
        for g in range(NSA_KV_HEADS):
            acc = jnp.zeros((n_step, 2 * GRP_W), F32)
            for jp in range(CMP_PAIRS):
                rows = [x_sc[g, pl.ds(2 * jp + jj, n_step, stride=CMP_STRIDE), :] for jj in range(2)]
                acc = acc + _dot(jnp.concatenate(rows, axis=1).astype(BF16), w_ref[jp])
            h_sc[g, pl.ds(r0, n_step), :] = acc

    @pl.when(i == 0)
    def _():
        load(0)

    even = lax.rem(i, 2) == 0
    steady = jnp.logical_and(i > 0, i < steps)
    for par in range(2):
        @pl.when(jnp.logical_and(steady, even == (par == 0)))
        def _(par=par):
            load(par)
            compress(1 - par, i - 1)

    @pl.when(i == steps)
    def _():
        compress((steps - 1) % 2, steps - 1)

    @pl.when(i == steps)
    def _():
        qpos = p_len
        cpos = lax.broadcasted_iota(jnp.int32, (1, nseg), 1) * CMP_STRIDE + (CMP_BLOCK - 1)
        dist = (qpos - cpos).astype(F32)
        row = lax.broadcasted_iota(jnp.int32, (SUB, 1), 0)
        seg = lax.broadcasted_iota(jnp.int32, (nseg, 1), 0)
        amat = amat_ref[...]
        b1 = bias_ref[0, 0:1, :]
        b2 = bias_ref[1, 0:1, :]
        for g in range(NSA_KV_HEADS):
            h = h_sc[g]
            new_seg = jnp.concatenate([cnew_ref[0, :, g * GRP_W:(g + 1) * GRP_W], jnp.zeros((1, GRP_W), F32)], axis=1)
            h2n = _dot(jnp.broadcast_to(new_seg, (SUB, 2 * GRP_W)).astype(BF16), w_ref[0])[0:1, GRP_W:] + b2
            h2 = jnp.where(seg == nseg - 1, h2n, pltpu.roll(h[:, GRP_W:], nseg - 1, 0) + b2)
            kvc = (h[:, :GRP_W] + b1 + h2).astype(BF16)
            slope = jnp.zeros((SUB, 1), F32)
            for rr in range(NSA_GROUP):
                slope = jnp.where(row == rr, 2.0 ** (-(g * NSA_GROUP + rr + 1.0)), slope)
            qg = qn_ref[0, g]
            p_c = _softmax_rows(_dot_nt(qg, kvc) - slope * dist, dist >= 0)
            p_c = jnp.where(row < NSA_GROUP, p_c, 0.0)
            oc_ref[0, g] = pltpu.roll(_dot(p_c.astype(BF16), kvc), LANES - NSA_DH, 1)
            s_rows = sum(_dot(t, amat) for t in _split3(p_c))
            sslc_ref[0, g] = jnp.broadcast_to(jnp.sum(s_rows, axis=0, keepdims=True), s_rows.shape)


def _cmp_decode(qn_dec, cmp_new, pool_t, page_table, w_pair, bias):
    db, n_pages = page_table.shape
    steps = n_pages // PAGES_PER_STEP
    p_len = n_pages * PAGE_SIZE
    nseg = p_len // CMP_STRIDE
    n_slc, nl = _decode_blocks(p_len, 1)
    amat = jnp.asarray(_slc_overlap_matrix(nseg, nl), BF16)

    grid_spec = pltpu.PrefetchScalarGridSpec(
        num_scalar_prefetch=1,
        grid=(db, steps + 1),
        in_specs=[pl.BlockSpec((1, NSA_KV_HEADS, SUB, LANES), lambda b, i, pt: (b, 0, 0, 0)),
                  pl.BlockSpec((1, 1, NSA_KV_W), lambda b, i, pt: (b, 0, 0)),
                  pl.BlockSpec(bias.shape, lambda b, i, pt: (0, 0, 0)),
                  pl.BlockSpec(w_pair.shape, lambda b, i, pt: (0, 0, 0)),
                  pl.BlockSpec(amat.shape, lambda b, i, pt: (0, 0)),
                  pl.BlockSpec(memory_space=pl.ANY)],
        out_specs=(pl.BlockSpec((1, NSA_KV_HEADS, SUB, LANES), lambda b, i, pt: (b, 0, 0, 0)),
                   pl.BlockSpec((1, NSA_KV_HEADS, SUB, nl), lambda b, i, pt: (b, 0, 0, 0))),
        scratch_shapes=[pltpu.VMEM((NSA_KV_HEADS, PAGES_PER_STEP * PAGE_SIZE, GRP_W), F32),
                        pltpu.VMEM((NSA_KV_HEADS, PAGES_PER_STEP * PAGE_SIZE, GRP_W), F32),
                        pltpu.VMEM((NSA_KV_HEADS, nseg, 2 * GRP_W), F32),
                        pltpu.VMEM((PAGE_SLOTS, PAGES_PER_STEP, NSA_KV_W, PAGE_SIZE), F32),
                        pltpu.SemaphoreType.DMA((PAGE_SLOTS,))],
    )
    return pl.pallas_call(
        functools.partial(_cmp_decode_kernel, p_len=p_len, steps=steps),
        grid_spec=grid_spec,
        out_shape=(jax.ShapeDtypeStruct((db, NSA_KV_HEADS, SUB, LANES), F32),
                   jax.ShapeDtypeStruct((db, NSA_KV_HEADS, SUB, nl), F32)),
        compiler_params=_cparams(("arbitrary", "arbitrary")),
        name="nsa_cmp_decode",
    )(page_table.reshape(-1), qn_dec, cmp_new.reshape(db, 1, NSA_KV_W), bias, w_pair, amat, pool_t)


def _topk_decode_kernel(s_ref, idx_ref, *, qpos):
    s_slc = s_ref[...]
    blk = lax.broadcasted_iota(jnp.int32, s_slc.shape, 1)
    cur = qpos // SLC_BLOCK
    valid = blk * SLC_BLOCK <= qpos
    forced = jnp.logical_or(blk == 0, jnp.logical_or(blk == cur, blk == cur - 1))
    score = jnp.where(forced, FORCE_SCORE, jnp.where(valid, s_slc, NEG_BIG))
    lane = lax.broadcasted_iota(jnp.int32, idx_ref.shape, 1)
    out = jnp.full(idx_ref.shape, -1, jnp.int32)
    for t in range(N_SELECT):
        m = jnp.max(score, axis=1, keepdims=True)
        idx = jnp.min(jnp.where(score == m, blk, score.shape[1]), axis=1, keepdims=True)
        out = jnp.where(lane == t, jnp.where(m > 0.5 * NEG_BIG, idx, -1), out)
        score = jnp.where(blk == idx, 2.0 * NEG_BIG, score)
    idx_ref[...] = out


def _topk_decode(s_slc, qpos):
    rows = s_slc.shape[0]
    return pl.pallas_call(
        functools.partial(_topk_decode_kernel, qpos=qpos),
        grid=(1,),
        in_specs=[_full_spec(s_slc)],
        out_specs=pl.BlockSpec((rows, LANES), lambda i: (0, 0)),
        out_shape=jax.ShapeDtypeStruct((rows, LANES), jnp.int32),
        compiler_params=_cparams(("arbitrary",)),
        name="nsa_topk_decode",
    )(s_slc)


def _sel_decode_kernel(pt_ref, sel_ref, qn_ref, gl_ref, oc_ref, snew_ref, wnew_ref, wnewt_ref, win_ref, *rest, p_len):
    del pt_ref
    nb = NSA_KV_HEADS * N_SELECT
    blk_refs = rest[:nb]
    o_ref, nwin_ref = rest[nb:]
    b = pl.program_id(0)
    qpos = p_len
    n_past_blocks = p_len // SLC_BLOCK
    per_page = PAGE_SIZE // SLC_BLOCK
    row = lax.broadcasted_iota(jnp.int32, (SUB, 1), 0)
    lane = lax.broadcasted_iota(jnp.int32, (SUB, LANES), 1)
    gates = jnp.broadcast_to(jax.nn.sigmoid(gl_ref[0]), (SUB, LANES))
    nw = win_ref.shape[2]
    win = win_ref[0]
    wnew = wnew_ref[0]
    snew = snew_ref[0]

    def merge_new(s, mask, kv_t, q8, new_row, slope, use_new):
        new_f = new_row.astype(BF16).astype(F32)
        s_n = jnp.sum(q8.astype(F32) * new_f, axis=1, keepdims=True)
        s = jnp.where(mask, s, NEG_BIG)
        m = jnp.maximum(_row_max(s), jnp.where(use_new, s_n, NEG_BIG))
        e = jnp.where(mask, jnp.exp(s - m), 0.0)
        e_n = jnp.where(use_new, jnp.exp(s_n - m), 0.0)
        denom = jnp.maximum(_row_sum(e) + e_n, 1e-30)
        p = e / denom
        p_n = e_n / denom
        o = _dot_nt(p.astype(BF16), kv_t) + p_n.astype(BF16).astype(F32) * new_f
        return pltpu.roll(o, LANES - NSA_DH, 1)

    for g in range(NSA_KV_HEADS):
        slope = jnp.zeros((SUB, 1), F32)
        for rr in range(NSA_GROUP):
            slope = jnp.where(row == rr, 2.0 ** (-(g * NSA_GROUP + rr + 1.0)), slope)
        q8 = qn_ref[0, g]
        kv_t = jnp.concatenate([r[...] for r in blk_refs[g * N_SELECT:(g + 1) * N_SELECT]], axis=1).astype(BF16)
        nk = N_SELECT * PAGE_SIZE
        key = lax.broadcasted_iota(jnp.int32, (1, nk), 1)
        slot = lax.shift_right_logical(key, int(np.log2(PAGE_SIZE)))
        blk_in_page = jnp.bitwise_and(lax.shift_right_logical(key, int(np.log2(SLC_BLOCK))), per_page - 1)
        off = jnp.bitwise_and(key, SLC_BLOCK - 1)
        kblk = jnp.zeros((1, nk), jnp.int32)
        use_new = False
        for t in range(N_SELECT):
            st = sel_ref[(b * NSA_KV_HEADS + g) * N_SELECT + t]
            kblk = jnp.where(slot == t, st, kblk)
            use_new = jnp.logical_or(use_new, st == n_past_blocks)
        kpos = kblk * SLC_BLOCK + off
        ok = jnp.logical_and(jnp.logical_and(kblk >= 0, kblk < n_past_blocks),
                             jnp.logical_and(blk_in_page == jnp.bitwise_and(kblk, per_page - 1), kpos <= qpos))
        dist = (qpos - kpos).astype(F32)
        o_s = merge_new(_dot(q8, kv_t) - slope * dist, ok, kv_t, q8, snew[:, g * LANES:(g + 1) * LANES], slope, use_new)
        kvw_t = win[g * LANES:(g + 1) * LANES, :].astype(BF16)
        wpos = p_len - nw + lax.broadcasted_iota(jnp.int32, (1, nw), 1)
        dist_w = (qpos - wpos).astype(F32)
        mask_w = jnp.logical_and(jnp.logical_and(dist_w >= 0, dist_w <= WINDOW), wpos >= 0)
        o_w = merge_new(_dot(q8, kvw_t) - slope * dist_w, mask_w, kvw_t, q8, wnew[:, g * LANES:(g + 1) * LANES], slope, True)
        hd = g * NSA_GROUP + row
        gate = lambda c: jnp.sum(jnp.where(lane == 3 * hd + c, gates, 0.0), axis=1, keepdims=True)
        mix = gate(0) * oc_ref[0, g] + gate(1) * o_s + gate(2) * o_w
        o_ref[0, g * NSA_GROUP:(g + 1) * NSA_GROUP, :] = mix[0:NSA_GROUP]
    req = lax.broadcasted_iota(jnp.int32, wnewt_ref.shape, 1)
    col = jnp.sum(jnp.where(req == b, wnewt_ref[...], 0.0), axis=1, keepdims=True)
    t_idx = lax.broadcasted_iota(jnp.int32, (1, nw), 1)
    nwin_ref[0] = jnp.where(t_idx == nw - 1, col, pltpu.roll(win, nw - 1, 1))


def _sel_decode(qn_dec, gl, o_c, sel_idx, slc_new, win_new, slc_pool_t, win_buf_t, page_table):
    db, n_pages = page_table.shape
    p_len = n_pages * PAGE_SIZE
    n_past_blocks = p_len // SLC_BLOCK
    per_page = PAGE_SIZE // SLC_BLOCK
    nw = win_buf_t.shape[2]
    win_new_t = jnp.transpose(win_new)

    def blk_spec(g, t):
        def imap(b, pt, sel):
            j = jnp.clip(sel[(b * NSA_KV_HEADS + g) * N_SELECT + t], 0, n_past_blocks - 1)
            return (pt[b * n_pages + j // per_page], g, 0)
        return pl.BlockSpec((None, GRP_W, PAGE_SIZE), imap)

    vec = lambda w: pl.BlockSpec((1, 1, w), lambda b, pt, sel: (b, 0, 0))
    grp = pl.BlockSpec((1, NSA_KV_HEADS, SUB, LANES), lambda b, pt, sel: (b, 0, 0, 0))
    grid_spec = pltpu.PrefetchScalarGridSpec(
        num_scalar_prefetch=2,
        grid=(db,),
        in_specs=[grp, vec(LANES), grp, vec(NSA_KV_W), vec(NSA_KV_W),
                  pl.BlockSpec(win_new_t.shape, lambda b, pt, sel: (0, 0)),
                  pl.BlockSpec((1, NSA_KV_W, nw), lambda b, pt, sel: (b, 0, 0))]
                 + [blk_spec(g, t) for g in range(NSA_KV_HEADS) for t in range(N_SELECT)],
        out_specs=(pl.BlockSpec((1, NSA_HEADS, LANES), lambda b, pt, sel: (b, 0, 0)),
                   pl.BlockSpec((1, NSA_KV_W, nw), lambda b, pt, sel: (b, 0, 0))),
    )
    r3 = lambda a: a.reshape(db, 1, a.shape[-1])
    return pl.pallas_call(
        functools.partial(_sel_decode_kernel, p_len=p_len),
        grid_spec=grid_spec,
        out_shape=(jax.ShapeDtypeStruct((db, NSA_HEADS, LANES), F32),
                   jax.ShapeDtypeStruct(win_buf_t.shape, win_buf_t.dtype)),
        compiler_params=_cparams(("parallel",)),
        name="nsa_sel_decode",
    )(page_table.reshape(-1), sel_idx, qn_dec, r3(gl), o_c, r3(slc_new), r3(win_new), win_new_t, win_buf_t,
      *([slc_pool_t] * (NSA_KV_HEADS * N_SELECT)))


TM_PROMPT = 256
TQ_MLA, TK_MLA = 512, 512
TQ_NSA, TK_NSA = 128, 512


def _mla_rows(rows):
    return jnp.concatenate([rows[:, :MLA_KV_RANK], rows[:, MLA_KV_RANK:MLA_KV_RANK + HALF_ROPE],
                            rows[:, MLA_KV_RANK + LANES:MLA_KV_RANK + LANES + HALF_ROPE]], axis=1)


def kernel(x_prompt, x_sample, cache_mla, cache_nsa_cmp, cache_nsa_slc, state_nsa_win, state_ret, page_table,
           norm_even, w_in_even, mla_gq, mla_gkv, mla_wuq, mla_wuk, mla_wuv, nsa_cmp_w, nsa_cmp_pe, w_out_even,
           norm_odd, w_in_odd, ret_gn, w_out_odd, final_norm):
    b, s, d = x_prompt.shape
    db, n_new, _ = x_sample.shape
    assert n_new == 1 and norm_even.shape[0] == 1 and norm_odd.shape[0] == 1
    n_pages = page_table.shape[1]
    p_len = n_pages * PAGE_SIZE
    kv_row = (NSA_KV_HEADS, 2, NSA_DH)
    tm = min(TM_PROMPT, b * s)

    w_ext, wuq_ext, wuk_ext, wuv_ext, wo_ext = _prep_even_weights(
        w_in_even[0], mla_wuq[0], mla_wuk[0], mla_wuv[0], w_out_even[0])
    w_big, pe_big = _prep_cmp_weights(nsa_cmp_w[0], nsa_cmp_pe[0])
    w_odd = w_in_odd[0].astype(BF16)
    wo_odd = w_out_odd[0].astype(BF16)
    gn_e, gq, gkv = norm_even[0][None], mla_gq[0][None], mla_gkv[0][None]
    gn_o, fn = norm_odd[0][None], final_norm[None]

    xp = x_prompt.reshape(b * s, d)
    pos_p = jnp.arange(s)
    cos_m, sin_m = _rope_tables_mla(pos_p)
    (qext, rows, kext, gmla, qn, cmp, _, slcp, win, winp, gl, gnsa, cmp_t, slc_t) = _even_project(
        xp, gn_e, w_ext, gq, gkv, wuq_ext, wuk_ext, cos_m, sin_m, tm, s)
    lat = _mla_prompt_attention(qext, kext, b, s, min(TQ_MLA, s), min(TK_MLA, s))
    kvc = _cmp_prompt(cmp, w_big, pe_big, b, s)
    onsa = _nsa_prompt(qn, gl, kvc, slcp, winp, b, s, TQ_NSA, min(TK_NSA, s))
    x1 = _even_output(xp, lat, gmla, onsa, gnsa, wuv_ext, wo_ext, tm)
    cos_r, sin_r = _rope_tables_ret(pos_p)
    q, k, v, g = _odd_project(x1, gn_o, w_odd, cos_r, sin_r, tm)
    y, ret_p = _retention_prompt(q, k, v, g, ret_gn[0], b, s)
    y_prompt = _odd_output(x1, y, wo_odd, fn, tm).reshape(b, s, d)
    nwin = min(WINDOW, s)
    mla_p = _mla_rows(rows).reshape(1, b, s, MLA_KV_RANK + MLA_ROPE)
    cache_rows = lambda a_t: jnp.transpose(a_t.reshape((b,) + kv_row + (s,)), (0, 4, 1, 2, 3))[None]
    cmp_p = cache_rows(cmp_t)
    slc_p = cache_rows(slc_t)
    win_p = win.reshape((b, s) + kv_row)[:, s - nwin:][None]

    xs = x_sample.reshape(db, d)
    pos_s = p_len + jnp.arange(n_new)
    cos_s, sin_s = [jnp.broadcast_to(t, (db, LANES)) for t in _rope_tables_mla(pos_s)]
    (qext_s, rows_s, _, gmla_s, qn_s, cmp_s, slc_s, _, win_s, _, gl_s, gnsa_s, _, _) = _even_project(
        xs, gn_e, w_ext, gq, gkv, wuq_ext, wuk_ext, cos_s, sin_s, db, db)
    rows_new = _mla_rows(rows_s)
    rope = lambda a: a.astype(F32).reshape(MLA_HEADS, db, MLA_HEADS, HALF_ROPE).sum(2).astype(BF16)
    q_dec = jnp.concatenate([qext_s[:, :, :MLA_KV_RANK], rope(qext_s[:, :, MLA_KV_RANK:MLA_KV_RANK + LANES]),
                             rope(qext_s[:, :, MLA_KV_RANK + LANES:])], axis=-1)
    pool = cache_nsa_cmp.shape[1]
    feat_major = lambda a, n: jnp.transpose(a[0], (0, 2, 3, 4, 1)).reshape(n, NSA_KV_W, a.shape[2])
    mla_pool_t = jnp.transpose(cache_mla[0], (0, 2, 1))
    lat_s = _mla_decode(jnp.transpose(q_dec, (1, 0, 2)), rows_new, mla_pool_t, page_table)
    qn_dec = jnp.pad(qn_s.reshape(db, NSA_KV_HEADS, NSA_GROUP, LANES), ((0, 0), (0, 0), (0, SUB - NSA_GROUP), (0, 0)))
    w_pair, pe_pair = _prep_cmp_pair_weights(nsa_cmp_w[0], nsa_cmp_pe[0])
    o_c, s_slc = _cmp_decode(qn_dec, cmp_s, feat_major(cache_nsa_cmp, pool), page_table, w_pair,
                             _cmp_bias(w_pair, pe_pair))
    sel = _topk_decode(s_slc[:, :, 0, :].reshape(db * NSA_KV_HEADS, -1), p_len)
    sel_idx = sel[:, :N_SELECT].reshape(-1)
    onsa_s, win_new_t = _sel_decode(qn_dec, gl_s, o_c, sel_idx, slc_s, win_s, feat_major(cache_nsa_slc, pool),
                                    feat_major(state_nsa_win, db), page_table)
    nw = win_new_t.shape[2]
    win_so = jnp.transpose(win_new_t.reshape((db,) + kv_row + (nw,)), (0, 4, 1, 2, 3))[None]
    x1s = _even_output(xs, jnp.transpose(lat_s, (1, 0, 2)), gmla_s, onsa_s.reshape(db, NSA_HEADS * LANES), gnsa_s,
                       wuv_ext, wo_ext, db)
    cos_rs, sin_rs = [jnp.broadcast_to(t, (db, RET_DK)) for t in _rope_tables_ret(pos_s)]
    q, k, v, g = _odd_project(x1s, gn_o, w_odd, cos_rs, sin_rs, db)
    ys, ret_s = _retention_decode(q, k, v, g, ret_gn[0], state_ret.reshape(state_ret.shape[1:]))
    y_sample = _odd_output(x1s, ys, wo_odd, fn, db).reshape(db, n_new, d)
    mla_s = rows_new.reshape(1, db, n_new, MLA_KV_RANK + MLA_ROPE)
    cmp_so = cmp_s.reshape((1, db, n_new) + kv_row)
    slc_so = slc_s.reshape((1, db, n_new) + kv_row)
    return (y_prompt, y_sample, mla_p, cmp_p, slc_p, win_p, ret_p[None],
            mla_s, cmp_so, slc_so, win_so, ret_s[None])
```
